```python
import math
import jax, jax.numpy as jnp
from jax import lax
import numpy as np

D_MODEL = 1024
BATCH = 16
SEQ = 4096
DEPTH = 2

GRID_W = 64
CTX_LEN = 256
EPS = 1e-6
ROPE_BASE = 10000.0
NEG_INF = -1e30

MLA_HEADS = 4
MLA_NOPE = 64
MLA_ROPE = 32
MLA_V = 64
MLA_Q_RANK = 256
MLA_KV_RANK = 128
MLA_SCALE = (MLA_NOPE + MLA_ROPE) ** -0.5
SWA_HEADS = 4
SWA_KV_HEADS = 2
SWA_HEAD_DIM = 64
WINDOW = 128
BLOCK = 128
LRU_WIDTH = 512
LRU_BLOCKS = 8
LRU_BW = LRU_WIDTH // LRU_BLOCKS
CONV_W = 4
CONV_LEFT = CONV_W // 2
CONV_RIGHT = CONV_W - 1 - CONV_W // 2
LRU_C = 8.0
N_GROUPS = 4
EXPERTS_PER_GROUP = 8
N_EXPERTS = N_GROUPS * EXPERTS_PER_GROUP
TOP_K = 2
D_EXPERT = 256

MLA_OUT = MLA_HEADS * MLA_V
SWA_OUT = SWA_HEADS * SWA_HEAD_DIM
MIX_WIDTH = MLA_OUT + SWA_OUT + LRU_WIDTH
IN_SIZES = (MLA_Q_RANK, MLA_KV_RANK, MLA_ROPE, SWA_HEADS * SWA_HEAD_DIM, SWA_KV_HEADS * SWA_HEAD_DIM, SWA_KV_HEADS * SWA_HEAD_DIM, LRU_WIDTH, LRU_WIDTH)
IN_COLS = sum(IN_SIZES)

kernel_name = 'hybrid_mla_swa_rglru_hmoe_dit'


def rmsnorm(x, g):
    xf = x.astype(jnp.float32)
    y = xf * lax.rsqrt(jnp.mean(xf * xf, axis=-1, keepdims=True) + EPS)
    return (y * g.astype(jnp.float32)).astype(x.dtype)


def rope_1d(x, pos):
    half = x.shape[-1] // 2
    freqs = ROPE_BASE ** (-jnp.arange(half, dtype=jnp.float32) / half)
    ang = pos.astype(jnp.float32)[:, None] * freqs[None, :]
    cos = jnp.cos(ang)[:, None, :]
    sin = jnp.sin(ang)[:, None, :]
    xf = x.astype(jnp.float32)
    x1, x2 = xf[..., :half], xf[..., half:]
    return jnp.concatenate([x1 * cos - x2 * sin, x2 * cos + x1 * sin], axis=-1).astype(x.dtype)


def rope_2d(x, rows, cols):
    d = x.shape[-1] // 2
    return jnp.concatenate([rope_1d(x[..., :d], rows), rope_1d(x[..., d:], cols)], axis=-1)


def split_cols(z):
    out = []
    o = 0
    for n in IN_SIZES:
        out.append(z[..., o:o + n])
        o += n
    return out


def mla_q(z_cq, g_cq, w_uq, rows, cols):
    B, S = z_cq.shape[:2]
    q = (rmsnorm(z_cq, g_cq) @ w_uq).reshape(B, S, MLA_HEADS, MLA_NOPE + MLA_ROPE)
    q_nope, q_rope = q[..., :MLA_NOPE], q[..., MLA_NOPE:]
    if rows is not None:
        q_rope = rope_2d(q_rope, rows, cols)
    return jnp.concatenate([q_nope, q_rope], axis=-1)


def mla_kv(z_ckv, z_kr, g_ckv, w_ukv, rows, cols):
    B, S = z_ckv.shape[:2]
    kv = (rmsnorm(z_ckv, g_ckv) @ w_ukv).reshape(B, S, MLA_HEADS, MLA_NOPE + MLA_V)
    k_nope, v = kv[..., :MLA_NOPE], kv[..., MLA_NOPE:]
    k_rope = z_kr[:, :, None, :]
    if rows is not None:
        k_rope = rope_2d(k_rope, rows, cols)
    k = jnp.concatenate([k_nope, jnp.broadcast_to(k_rope, (B, S, MLA_HEADS, MLA_ROPE))], axis=-1)
    return k, v


def dense_attention(q, k, v, scale):
    B, L, H, _ = q.shape
    s = jnp.einsum('blhd,bjhd->bhlj', q, k).astype(jnp.float32) * scale
    p = jax.nn.softmax(s, axis=-1).astype(v.dtype)
    return jnp.einsum('bhlj,bjhd->blhd', p, v).reshape(B, L, H * v.shape[-1])


def blockwise_attention(q, k, v, scale):
    B, S, H, dq = q.shape
    nb = S // BLOCK
    qb = jnp.moveaxis(q.reshape(B, nb, BLOCK, H, dq), 1, 0)

    def one(qblk):
        s = jnp.einsum('bqhd,bjhd->bhqj', qblk, k).astype(jnp.float32) * scale
        p = jax.nn.softmax(s, axis=-1).astype(v.dtype)
        return jnp.einsum('bhqj,bjhd->bqhd', p, v)

    o = lax.map(one, qb)
    return jnp.moveaxis(o, 0, 1).reshape(B, S, H * v.shape[-1])


def sink_attention(q, k, v, sink):
    B, L, Hq, hd = q.shape
    KV = k.shape[2]
    G = Hq // KV
    qg = q.reshape(B, L, KV, G, hd)
    s = jnp.einsum('blkgd,bjkd->bkglj', qg, k).astype(jnp.float32) * (hd ** -0.5)
    s_sink = jnp.broadcast_to(sink.astype(jnp.float32).reshape(KV, G)[None, :, :, None, None], s.shape[:-1] + (1,))
    p = jax.nn.softmax(jnp.concatenate([s, s_sink], axis=-1), axis=-1)[..., :-1].astype(v.dtype)
    return jnp.einsum('bkglj,bjkd->blkgd', p, v).reshape(B, L, Hq * hd)


def banded_sink_attention(q, k, v, k_ctx, v_ctx, sink):
    B, S, Hq, hd = q.shape
    KV = k.shape[2]
    G = Hq // KV
    L = k_ctx.shape[1]
    nb = S // BLOCK
    span = BLOCK + 2 * WINDOW
    scale = hd ** -0.5
    pad = ((0, 0), (WINDOW, WINDOW), (0, 0), (0, 0))
    kpad = jnp.pad(k, pad)
    vpad = jnp.pad(v, pad)
    qb = jnp.moveaxis(q.reshape(B, nb, BLOCK, KV, G, hd), 1, 0)
    sink_kg = sink.astype(jnp.float32).reshape(KV, G)

    def one(args):
        n, qblk = args
        start = n * BLOCK
        kw = lax.dynamic_slice_in_dim(kpad, start, span, axis=1)
        vw = lax.dynamic_slice_in_dim(vpad, start, span, axis=1)
        qpos = start + jnp.arange(BLOCK)
        kpos = start - WINDOW + jnp.arange(span)
        valid = (jnp.abs(qpos[:, None] - kpos[None, :]) <= WINDOW) & (kpos >= 0)[None, :] & (kpos < S)[None, :]
        s_loc = jnp.einsum('bqkgd,bjkd->bkgqj', qblk, kw).astype(jnp.float32) * scale
        s_loc = jnp.where(valid, s_loc, NEG_INF)
        s_ctx = jnp.einsum('bqkgd,bjkd->bkgqj', qblk, k_ctx).astype(jnp.float32) * scale
        s_sink = jnp.broadcast_to(sink_kg[None, :, :, None, None], s_loc.shape[:-1] + (1,))
        p = jax.nn.softmax(jnp.concatenate([s_loc, s_ctx, s_sink], axis=-1), axis=-1).astype(v.dtype)
        return (jnp.einsum('bkgqj,bjkd->bqkgd', p[..., :span], vw)
                + jnp.einsum('bkgqj,bjkd->bqkgd', p[..., span:span + L], v_ctx))

    o = lax.map(one, (jnp.arange(nb), qb))
    return jnp.moveaxis(o, 0, 1).reshape(B, S, Hq * hd)


def centred_conv(z, w, b):
    S = z.shape[1]
    zp = jnp.pad(z, ((0, 0), (CONV_LEFT, CONV_RIGHT), (0, 0)))
    y = b
    for tap in range(CONV_W):
        y = y + zp[:, tap:tap + S] * w[tap]
    return y


def linear_scan(a, b, h0, reverse):
    def comb(l, r):
        return (l[0] * r[0], r[0] * l[1] + r[1])
    A, Bc = lax.associative_scan(comb, (a, b), axis=1, reverse=reverse)
    return A * h0[:, None, :] + Bc


def rglru_direction(u, wa, ba, wx, bx, lam, h0, reverse):
    B, S, W = u.shape
    ub = u.reshape(B, S, LRU_BLOCKS, LRU_BW)
    r = jax.nn.sigmoid((jnp.einsum('bsnc,ncd->bsnd', ub, wa).reshape(B, S, W) + ba).astype(jnp.float32))
    i = jax.nn.sigmoid((jnp.einsum('bsnc,ncd->bsnd', ub, wx).reshape(B, S, W) + bx).astype(jnp.float32))
    log_a = -LRU_C * r * jax.nn.softplus(-lam.astype(jnp.float32))
    a = jnp.exp(log_a)
    bt = jnp.sqrt(-jnp.expm1(2.0 * log_a)) * (i * u.astype(jnp.float32))
    return linear_scan(a, bt, h0, reverse)


def merge_groups(o_mla, o_swa, o_lru, g):
    return jnp.concatenate([
        rmsnorm(o_mla, g[:MLA_OUT]),
        rmsnorm(o_swa, g[MLA_OUT:MLA_OUT + SWA_OUT]),
        rmsnorm(o_lru, g[MLA_OUT + SWA_OUT:]),
    ], axis=-1)


def mixing_layer(h, hc, rows, cols, w_in, g_cq, w_uq, g_ckv, w_ukv, sink, conv_w, conv_b,
                 wa, ba, wx, bx, lam, g_grp, w_out, need_ctx_out):
    B, S, _ = h.shape
    L = hc.shape[1]
    z = split_cols(h @ w_in)
    zc = split_cols(hc @ w_in)
    k_c, v_c = mla_kv(zc[1], zc[2], g_ckv, w_ukv, None, None)
    sk_c = zc[4].reshape(B, L, SWA_KV_HEADS, SWA_HEAD_DIM)
    sv_c = zc[5].reshape(B, L, SWA_KV_HEADS, SWA_HEAD_DIM)
    u_c = centred_conv(zc[6], conv_w, conv_b)
    h0 = jnp.zeros((B, LRU_WIDTH), jnp.float32)
    hf_c = rglru_direction(u_c, wa[0], ba[0], wx[0], bx[0], lam[0], h0, False)
    hb_c = rglru_direction(u_c, wa[1], ba[1], wx[1], bx[1], lam[1], h0, True)
    q = mla_q(z[0], g_cq, w_uq, rows, cols)
    k, v = mla_kv(z[1], z[2], g_ckv, w_ukv, rows, cols)
    o_mla = blockwise_attention(q, jnp.concatenate([k_c, k], axis=1), jnp.concatenate([v_c, v], axis=1), MLA_SCALE)
    sq = rope_2d(z[3].reshape(B, S, SWA_HEADS, SWA_HEAD_DIM), rows, cols)
    sk = rope_2d(z[4].reshape(B, S, SWA_KV_HEADS, SWA_HEAD_DIM), rows, cols)
    sv = z[5].reshape(B, S, SWA_KV_HEADS, SWA_HEAD_DIM)
    o_swa = banded_sink_attention(sq, sk, sv, sk_c, sv_c, sink)
    u = centred_conv(z[6], conv_w, conv_b)
    hf = rglru_direction(u, wa[0], ba[0], wx[0], bx[0], lam[0], hf_c[:, -1], False)
    hb = rglru_direction(u, wa[1], ba[1], wx[1], bx[1], lam[1], hb_c[:, 0], True)
    o_lru = (hf + hb).astype(h.dtype) * jax.nn.gelu(z[7])
    out = merge_groups(o_mla, o_swa, o_lru, g_grp) @ w_out
    if not need_ctx_out:
        return out, None
    q_c = mla_q(zc[0], g_cq, w_uq, None, None)
    oc_mla = dense_attention(q_c, k_c, v_c, MLA_SCALE)
    oc_swa = sink_attention(zc[3].reshape(B, L, SWA_HEADS, SWA_HEAD_DIM), sk_c, sv_c, sink)
    oc_lru = (hf_c + hb_c).astype(hc.dtype) * jax.nn.gelu(zc[7])
    out_c = merge_groups(oc_mla, oc_swa, oc_lru, g_grp) @ w_out
    return out, out_c


def hmoe(h, w_g1, b_g1, w_g2, b_g2, w_eg, w_eu, w_ed):
    shp = h.shape
    t = h.reshape(-1, shp[-1])
    T = t.shape[0]
    pg = jax.nn.softmax((t @ w_g1 + b_g1).astype(jnp.float32), axis=-1)
    pg_top, g_idx = lax.top_k(pg, 1)
    le = jnp.einsum('td,gde->tge', t, w_g2) + b_g2
    le_sel = le[jnp.arange(T), g_idx[:, 0]]
    pe = jax.nn.softmax(le_sel.astype(jnp.float32), axis=-1)
    pe_top, e_idx = lax.top_k(pe, TOP_K)
    w = pg_top * pe_top / jnp.sum(pe_top, axis=-1, keepdims=True)
    eid = g_idx * EXPERTS_PER_GROUP + e_idx
    combine = jnp.sum(jax.nn.one_hot(eid, N_EXPERTS, dtype=jnp.float32) * w[..., None], axis=1).astype(t.dtype)
    y = jnp.zeros_like(t)
    for e in range(N_EXPERTS):
        act = jax.nn.silu(t @ w_eg[e]) * (t @ w_eu[e])
        y = y + (combine[:, e:e + 1] * act) @ w_ed[e]
    return y.reshape(shp)


def setup_inputs(seed: int = 0) -> dict:
    key = jax.random.key(seed)
    ks = iter(jax.random.split(key, 40))
    f32 = jnp.float32

    def nrm(shape, scale):
        return jax.random.normal(next(ks), shape, f32) * scale

    def gain(shape):
        return 1.0 + nrm(shape, 0.01)

    u = jax.random.uniform(next(ks), (DEPTH, 2, LRU_WIDTH), f32, minval=0.9, maxval=0.999)
    a0 = u ** (1.0 / LRU_C)
    lru_lam = jnp.log(a0) - jnp.log1p(-a0)
    return {
        'x': nrm((BATCH, SEQ, D_MODEL), 1.0),
        'c': nrm((BATCH, D_MODEL), 1.0),
        'ctx': nrm((BATCH, CTX_LEN, D_MODEL), 1.0),
        'c_ctx': nrm((D_MODEL,), 1.0),
        'w_ada': nrm((DEPTH, D_MODEL, 6 * D_MODEL), 0.5 * D_MODEL ** -0.5),
        'b_ada': nrm((DEPTH, 6 * D_MODEL), 0.01),
        'g_norm1': gain((DEPTH, D_MODEL)),
        'g_norm2': gain((DEPTH, D_MODEL)),
        'w_in': nrm((DEPTH, D_MODEL, IN_COLS), D_MODEL ** -0.5),
        'g_cq': gain((DEPTH, MLA_Q_RANK)),
        'w_uq': nrm((DEPTH, MLA_Q_RANK, MLA_HEADS * (MLA_NOPE + MLA_ROPE)), MLA_Q_RANK ** -0.5),
        'g_ckv': gain((DEPTH, MLA_KV_RANK)),
        'w_ukv': nrm((DEPTH, MLA_KV_RANK, MLA_HEADS * (MLA_NOPE + MLA_V)), MLA_KV_RANK ** -0.5),
        'swa_sink': nrm((DEPTH, SWA_HEADS), 1.0),
        'conv_w': nrm((DEPTH, CONV_W, LRU_WIDTH), CONV_W ** -0.5),
        'conv_b': nrm((DEPTH, LRU_WIDTH), 0.01),
        'lru_wa': nrm((DEPTH, 2, LRU_BLOCKS, LRU_BW, LRU_BW), LRU_BW ** -0.5),
        'lru_ba': nrm((DEPTH, 2, LRU_WIDTH), 0.01),
        'lru_wx': nrm((DEPTH, 2, LRU_BLOCKS, LRU_BW, LRU_BW), LRU_BW ** -0.5),
        'lru_bx': nrm((DEPTH, 2, LRU_WIDTH), 0.01),
        'lru_lam': lru_lam,
        'g_grp': gain((DEPTH, MIX_WIDTH)),
        'w_out': nrm((DEPTH, MIX_WIDTH, D_MODEL), MIX_WIDTH ** -0.5),
        'w_g1': nrm((DEPTH, D_MODEL, N_GROUPS), D_MODEL ** -0.5),
        'b_g1': nrm((DEPTH, N_GROUPS), 0.01),
        'w_g2': nrm((DEPTH, N_GROUPS, D_MODEL, EXPERTS_PER_GROUP), D_MODEL ** -0.5),
        'b_g2': nrm((DEPTH, N_GROUPS, EXPERTS_PER_GROUP), 0.01),
        'w_e_gate': nrm((DEPTH, N_EXPERTS, D_MODEL, D_EXPERT), D_MODEL ** -0.5),
        'w_e_up': nrm((DEPTH, N_EXPERTS, D_MODEL, D_EXPERT), D_MODEL ** -0.5),
        'w_e_down': nrm((DEPTH, N_EXPERTS, D_EXPERT, D_MODEL), D_EXPERT ** -0.5),
        'g_final': gain((D_MODEL,)),
    }


def reference(x, c, ctx, c_ctx, w_ada, b_ada, g_norm1, g_norm2, w_in, g_cq, w_uq, g_ckv, w_ukv,
              swa_sink, conv_w, conv_b, lru_wa, lru_ba, lru_wx, lru_bx, lru_lam, g_grp, w_out,
              w_g1, b_g1, w_g2, b_g2, w_e_gate, w_e_up, w_e_down, g_final):
    S = x.shape[1]
    ROWS = S // GRID_W
    rows = jnp.repeat(jnp.arange(ROWS, dtype=jnp.int32), GRID_W)
    cols = jnp.tile(jnp.arange(GRID_W, dtype=jnp.int32), ROWS)
    xc = ctx
    for l in range(DEPTH):
        last = l == DEPTH - 1
        mod = (jax.nn.silu(c) @ w_ada[l] + b_ada[l])[:, None, :]
        mod_c = jax.nn.silu(c_ctx) @ w_ada[l] + b_ada[l]
        sh1, sc1, gt1, sh2, sc2, gt2 = jnp.split(mod, 6, axis=-1)
        sh1c, sc1c, gt1c, sh2c, sc2c, gt2c = jnp.split(mod_c, 6, axis=-1)
        h = rmsnorm(x, g_norm1[l]) * (1.0 + sc1) + sh1
        hc = rmsnorm(xc, g_norm1[l]) * (1.0 + sc1c) + sh1c
        mix, mix_c = mixing_layer(h, hc, rows, cols, w_in[l], g_cq[l], w_uq[l], g_ckv[l], w_ukv[l],
                                  swa_sink[l], conv_w[l], conv_b[l], lru_wa[l], lru_ba[l], lru_wx[l],
                                  lru_bx[l], lru_lam[l], g_grp[l], w_out[l], not last)
        x = x + gt1 * mix
        h2 = rmsnorm(x, g_norm2[l]) * (1.0 + sc2) + sh2
        x = x + gt2 * hmoe(h2, w_g1[l], b_g1[l], w_g2[l], b_g2[l], w_e_gate[l], w_e_up[l], w_e_down[l])
        if not last:
            xc = xc + gt1c * mix_c
            h2c = rmsnorm(xc, g_norm2[l]) * (1.0 + sc2c) + sh2c
            xc = xc + gt2c * hmoe(h2c, w_g1[l], b_g1[l], w_g2[l], b_g2[l], w_e_gate[l], w_e_up[l], w_e_down[l])
    return rmsnorm(x, g_final)
```

```python
import functools

import jax
import jax.numpy as jnp
from jax import lax
from jax.experimental import pallas as pl
from jax.experimental.pallas import tpu as pltpu

GRID_W = 64
EPS = 1e-6
ROPE_BASE = 10000.0
NEG_INF = -1e30
MLA_HEADS = 4
MLA_NOPE = 64
MLA_ROPE = 32
MLA_V = 64
MLA_Q_RANK = 256
MLA_KV_RANK = 128
MLA_SCALE = (MLA_NOPE + MLA_ROPE) ** -0.5
SWA_HEADS = 4
SWA_KV_HEADS = 2
SWA_HEAD_DIM = 64
SWA_SCALE = SWA_HEAD_DIM ** -0.5
WINDOW = 128
LRU_WIDTH = 512
LRU_BLOCKS = 8
LRU_BW = LRU_WIDTH // LRU_BLOCKS
CONV_W = 4
LRU_C = 8.0
N_GROUPS = 4
EXPERTS_PER_GROUP = 8
N_EXPERTS = N_GROUPS * EXPERTS_PER_GROUP
D_EXPERT = 256
MLA_OUT = MLA_HEADS * MLA_V
SWA_OUT = SWA_HEADS * SWA_HEAD_DIM

LANES = 128
SUBLANES = 8
TM = 256
ZW = 2048
MOE_TM = 512
VMEM_LIMIT = 56 * 1024 * 1024

F32 = jnp.float32
BF16 = jnp.bfloat16


def _params(sem):
    return pltpu.CompilerParams(dimension_semantics=sem, vmem_limit_bytes=VMEM_LIMIT)


def _full(shape):
    n = len(shape)
    return pl.BlockSpec(shape, lambda *_: (0,) * n)


def _rms(x, width):
    return x * lax.rsqrt(jnp.sum(x * x, axis=-1, keepdims=True) * (1.0 / width) + EPS)


def _dot(a, b):
    return jnp.dot(a, b, preferred_element_type=F32)


def _dot_nt(a, b):
    return lax.dot_general(a, b, (((1,), (1,)), ((), ())), preferred_element_type=F32)


def _ada_kernel(c_ref, w_ref, b_ref, o_ref):
    c = c_ref[...]
    s = c * jax.nn.sigmoid(c)
    o_ref[0] = jnp.dot(s, w_ref[0], preferred_element_type=F32,
                       precision=lax.Precision.HIGHEST) + b_ref[0]


def _ada_call(cc, w_ada, b_ada):
    depth, d, n = w_ada.shape
    r = cc.shape[0]
    tn = 1536
    return pl.pallas_call(
        _ada_kernel,
        grid=(depth, n // tn),
        in_specs=[
            pl.BlockSpec((r, d), lambda l, j: (0, 0)),
            pl.BlockSpec((1, d, tn), lambda l, j: (l, 0, j)),
            pl.BlockSpec((1, 1, tn), lambda l, j: (l, 0, j)),
        ],
        out_specs=pl.BlockSpec((1, r, tn), lambda l, j: (l, 0, j)),
        out_shape=jax.ShapeDtypeStruct((depth, r, n), F32),
        compiler_params=_params(("arbitrary", "arbitrary")),
        name="adaln",
    )(cc, w_ada, b_ada.reshape(depth, 1, n))


def _rope(x, cos, sina, sinb, shift):
    n = x.shape[-1]
    reps = n // LANES
    if reps > 1:
        cos = jnp.concatenate([cos] * reps, axis=-1)
        sina = jnp.concatenate([sina] * reps, axis=-1)
        sinb = jnp.concatenate([sinb] * reps, axis=-1)
    return x * cos + pltpu.roll(x, n - shift, 1) * sina + pltpu.roll(x, shift, 1) * sinb


def _inproj_kernel(x_ref, sh_ref, sc_ref, g1_ref, win_ref, gcq_ref, wuq_ref, gckv_ref, wukv_ref,
                   mcos_ref, msa_ref, msb_ref, scos_ref, ssa_ref, ssb_ref,
                   qm_ref, km_ref, vm_ref, sq_ref, sk_ref, sv_ref, lx_ref, lg_ref):
    d = x_ref.shape[-1]
    x = x_ref[0]
    h = _rms(x, d) * g1_ref[...] * (1.0 + sc_ref[0]) + sh_ref[0]
    z = _dot(h.astype(BF16), win_ref[...])

    mcos, msa, msb = mcos_ref[...], msa_ref[...], msb_ref[...]
    scos, ssa, ssb = scos_ref[...], ssa_ref[...], ssb_ref[...]

    cq = _rms(z[:, 0:256], MLA_Q_RANK) * gcq_ref[...]
    q = _dot(cq.astype(BF16), wuq_ref[...])
    q = _rope(q, mcos, msa, msb, MLA_ROPE // 4)
    qm_ref[0] = (q * MLA_SCALE).astype(BF16)

    ckv = _rms(z[:, 256:384], MLA_KV_RANK) * gckv_ref[...]
    kv = _dot(ckv.astype(BF16), wukv_ref[...])
    kr = _rope(z[:, 384:512], mcos, msa, msb, MLA_ROPE // 4)
    km_ref[0] = (kv[:, 0:512] + jnp.concatenate([kr] * MLA_HEADS, axis=-1)).astype(BF16)
    vm_ref[0] = kv[:, 512:768].astype(BF16)

    sq_ref[0] = (_rope(z[:, 512:768], scos, ssa, ssb, SWA_HEAD_DIM // 4) * SWA_SCALE).astype(BF16)
    sk_ref[0] = _rope(z[:, 768:896], scos, ssa, ssb, SWA_HEAD_DIM // 4).astype(BF16)
    sv_ref[0] = z[:, 896:1024].astype(BF16)
    lx_ref[0] = z[:, 1024:1536].astype(BF16)
    lg_ref[0] = z[:, 1536:2048].astype(BF16)


def _mod_spec(j, b_rows, n_ctx_tiles, d, tile_major):
    if tile_major:
        return pl.BlockSpec((1, 1, d), lambda i, b: (jnp.where(i < n_ctx_tiles, b_rows, b), 0, j))
    return pl.BlockSpec((1, 1, d), lambda b, i: (jnp.where(i < n_ctx_tiles, b_rows, b), 0, j))


def _inproj_call(xs, mod, g1, win, gcq, wuq, gckv, wukv, tabs, n_ctx_tiles):
    bsz, j, d = xs.shape
    nt = j // TM
    tok = lambda w: pl.BlockSpec((1, TM, w), lambda i, b: (b, i, 0))
    tab = pl.BlockSpec((TM, LANES), lambda i, b: (i, 0))
    widths = (512, 512, 256, 256, 128, 128, 512, 512)
    return pl.pallas_call(
        _inproj_kernel,
        grid=(nt, bsz),
        in_specs=[
            tok(d),
            _mod_spec(0, bsz, n_ctx_tiles, d, True),
            _mod_spec(1, bsz, n_ctx_tiles, d, True),
            _full(g1.shape), _full(win.shape), _full(gcq.shape), _full(wuq.shape),
            _full(gckv.shape), _full(wukv.shape),
            tab, tab, tab, tab, tab, tab,
        ],
        out_specs=[tok(w) for w in widths],
        out_shape=[jax.ShapeDtypeStruct((bsz, j, w), BF16) for w in widths],
        compiler_params=_params(("arbitrary", "arbitrary")),
        name="inproj",
    )(xs, mod, mod, g1, win, gcq, wuq, gckv, wukv, *tabs)


def _mla_heads(q_ref, k_ref, v_ref, o_ref, klen):
    lane = lax.broadcasted_iota(jnp.int32, (q_ref.shape[1], LANES), 1)
    v = v_ref[0, 0:klen, :]
    outs = []
    for h in range(2):
        q = q_ref[0, :, h * LANES:(h + 1) * LANES]
        k = k_ref[0, 0:klen, h * LANES:(h + 1) * LANES]
        s = _dot_nt(q, k)
        m = jnp.max(s, axis=-1, keepdims=True)
        p = jnp.exp(s - m)
        l = jnp.sum(p, axis=-1, keepdims=True)
        outs.append(_dot(p.astype(BF16), v) / l)
    o_ref[0] = jnp.where(lane < MLA_V, outs[0], outs[1]).astype(o_ref.dtype)


def _mla_kernel(q_ref, k_ref, v_ref, o_ref, *, n_ctx_tiles, ctx_len):
    i = pl.program_id(2)

    @pl.when(i < n_ctx_tiles)
    def _():
        _mla_heads(q_ref, k_ref, v_ref, o_ref, ctx_len)

    @pl.when(i >= n_ctx_tiles)
    def _():
        _mla_heads(q_ref, k_ref, v_ref, o_ref, k_ref.shape[1])


def _mla_call(qm, km, vm, n_ctx_tiles, ctx_len):
    bsz, j, _ = qm.shape
    nt = j // TM
    return pl.pallas_call(
        functools.partial(_mla_kernel, n_ctx_tiles=n_ctx_tiles, ctx_len=ctx_len),
        grid=(bsz, MLA_HEADS // 2, nt),
        in_specs=[
            pl.BlockSpec((1, TM, 2 * LANES), lambda b, hp, i: (b, i, hp)),
            pl.BlockSpec((1, j, 2 * LANES), lambda b, hp, i: (b, 0, hp)),
            pl.BlockSpec((1, j, LANES), lambda b, hp, i: (b, 0, hp)),
        ],
        out_specs=pl.BlockSpec((1, TM, LANES), lambda b, hp, i: (b, i, hp)),
        out_shape=jax.ShapeDtypeStruct((bsz, j, MLA_OUT), BF16),
        compiler_params=_params(("arbitrary", "arbitrary", "arbitrary")),
        name="mla_attn",
    )(qm, km, vm)


def _swa_kernel(sink_ref, q_ref, k_ref, v_ref, o_ref, *, ctx_len):
    i = pl.program_id(1)
    j = k_ref.shape[1]
    span = TM + 2 * WINDOW
    row0 = pl.multiple_of(jnp.clip(i * TM - WINDOW, 0, j - span), WINDOW)
    qpos = i * TM - ctx_len + lax.broadcasted_iota(jnp.int32, (TM, span), 0)
    kpos = row0 - ctx_len + lax.broadcasted_iota(jnp.int32, (TM, span), 1)
    valid = (jnp.abs(qpos - kpos) <= WINDOW) & (kpos >= 0) & (qpos >= 0)
    outs = []
    for h in range(SWA_HEADS):
        kvh = h // (SWA_HEADS // SWA_KV_HEADS)
        cols = slice(kvh * SWA_HEAD_DIM, (kvh + 1) * SWA_HEAD_DIM)
        q = q_ref[0, :, h * SWA_HEAD_DIM:(h + 1) * SWA_HEAD_DIM]
        kl = k_ref[0, pl.ds(row0, span), cols]
        vl = v_ref[0, pl.ds(row0, span), cols]
        kc = k_ref[0, 0:ctx_len, cols]
        vc = v_ref[0, 0:ctx_len, cols]
        s_loc = jnp.where(valid, _dot_nt(q, kl), NEG_INF)
        s_ctx = _dot_nt(q, kc)
        sink = sink_ref[h]
        m = jnp.maximum(jnp.maximum(jnp.max(s_loc, axis=-1, keepdims=True),
                                    jnp.max(s_ctx, axis=-1, keepdims=True)), sink)
        p_loc = jnp.exp(s_loc - m)
        p_ctx = jnp.exp(s_ctx - m)
        l = (jnp.sum(p_loc, axis=-1, keepdims=True) + jnp.sum(p_ctx, axis=-1, keepdims=True)
             + jnp.exp(sink - m))
        o = _dot(p_loc.astype(BF16), vl) + _dot(p_ctx.astype(BF16), vc)
        outs.append(o / l)
    o_ref[0] = jnp.concatenate(outs, axis=-1).astype(o_ref.dtype)


def _swa_call(sink, sq, sk, sv, ctx_len):
    bsz, j, _ = sq.shape
    nt = j // TM
    return pl.pallas_call(
        functools.partial(_swa_kernel, ctx_len=ctx_len),
        grid=(bsz, nt),
        in_specs=[
            pl.BlockSpec(memory_space=pltpu.SMEM),
            pl.BlockSpec((1, TM, SWA_OUT), lambda b, i: (b, i, 0)),
            pl.BlockSpec((1, j, SWA_KV_HEADS * SWA_HEAD_DIM), lambda b, i: (b, 0, 0)),
            pl.BlockSpec((1, j, SWA_KV_HEADS * SWA_HEAD_DIM), lambda b, i: (b, 0, 0)),
        ],
        out_specs=pl.BlockSpec((1, TM, SWA_OUT), lambda b, i: (b, i, 0)),
        out_shape=jax.ShapeDtypeStruct((bsz, j, SWA_OUT), BF16),
        compiler_params=_params(("arbitrary", "arbitrary")),
        name="swa_attn",
    )(sink, sq, sk, sv)


def _lru_gates(prev_ref, cur_ref, next_ref, cw_ref, cb_ref, w_ref, b_ref, sp_ref, seg_start, seg_end):
    z = cur_ref[0].astype(F32)
    row = lax.broadcasted_iota(jnp.int32, z.shape, 0)
    keep_prev = jnp.where(seg_start, 0.0, 1.0)
    keep_next = jnp.where(seg_end, 0.0, 1.0)
    p2 = prev_ref[0, TM - 2:TM - 1, :].astype(F32) * keep_prev
    p1 = prev_ref[0, TM - 1:TM, :].astype(F32) * keep_prev
    n0 = next_ref[0, 0:1, :].astype(F32) * keep_next
    z_m1 = jnp.where(row == 0, p1, pltpu.roll(z, 1, 0))
    z_m2 = jnp.where(row == 0, p2, jnp.where(row == 1, p1, pltpu.roll(z, 2, 0)))
    z_p1 = jnp.where(row == TM - 1, n0, pltpu.roll(z, TM - 1, 0))
    u = cb_ref[...] + z_m2 * cw_ref[0:1, :]
    u = u + z_m1 * cw_ref[1:2, :]
    u = u + z * cw_ref[2:3, :]
    u = u + z_p1 * cw_ref[3:4, :]
    g = _dot(u.astype(BF16), w_ref[...]) + b_ref[...]
    r = jax.nn.sigmoid(g[:, 0:LRU_WIDTH])
    ig = jax.nn.sigmoid(g[:, LRU_WIDTH:2 * LRU_WIDTH])
    log_a = (-LRU_C) * r * sp_ref[...]
    a = jnp.exp(log_a)
    bt = jnp.sqrt(1.0 - a * a) * (ig * u)
    return a, bt


def _lru_scan(a, b, h0, reverse):
    t, w = a.shape
    g = t // SUBLANES
    a = a.reshape(g, SUBLANES, w)
    b = b.reshape(g, SUBLANES, w)
    sub = lax.broadcasted_iota(jnp.int32, (g, SUBLANES, w), 1)
    d = 1
    while d < SUBLANES:
        if reverse:
            shift, ok = SUBLANES - d, sub < SUBLANES - d
        else:
            shift, ok = d, sub >= d
        a_sh = pltpu.roll(a, shift, 1)
        b_sh = pltpu.roll(b, shift, 1)
        b = jnp.where(ok, a * b_sh + b, b)
        a = jnp.where(ok, a * a_sh, a)
        d *= 2
    hs = [None] * g
    h = h0
    order = range(g - 1, -1, -1) if reverse else range(g)
    for gi in order:
        hg = a[gi] * h + b[gi]
        hs[gi] = hg
        h = hg[0:1, :] if reverse else hg[SUBLANES - 1:SUBLANES, :]
    return jnp.concatenate(hs, axis=0), h


def _softplus_neg(lam):
    x = -lam
    return jnp.maximum(x, 0.0) + jnp.log1p(jnp.exp(-jnp.abs(x)))


def _lru_fwd_kernel(prev_ref, cur_ref, next_ref, cw_ref, cb_ref, w_ref, b_ref, lam_ref,
                    hf_ref, carry_ref, sp_ref, *, n_ctx_tiles):
    c = pl.program_id(1)
    nt = pl.num_programs(1)

    @pl.when(c == 0)
    def _():
        carry_ref[...] = jnp.zeros_like(carry_ref)
        sp_ref[...] = _softplus_neg(lam_ref[...])

    seg_start = (c == 0) | (c == n_ctx_tiles)
    seg_end = (c == n_ctx_tiles - 1) | (c == nt - 1)
    a, bt = _lru_gates(prev_ref, cur_ref, next_ref, cw_ref, cb_ref, w_ref, b_ref, sp_ref,
                       seg_start, seg_end)
    hs, h = _lru_scan(a, bt, carry_ref[...], False)
    carry_ref[...] = h
    hf_ref[0] = hs.astype(hf_ref.dtype)


def _bwd_chunk(s, nt, n_ctx_tiles):
    return jnp.where(s < n_ctx_tiles, n_ctx_tiles - 1 - s, nt - 1 - (s - n_ctx_tiles))


def _lru_bwd_kernel(prev_ref, cur_ref, next_ref, cw_ref, cb_ref, w_ref, b_ref, lam_ref,
                    hf_ref, lg_ref, o_ref, carry_ref, sp_ref, *, n_ctx_tiles):
    s = pl.program_id(1)
    nt = pl.num_programs(1)
    c = _bwd_chunk(s, nt, n_ctx_tiles)

    @pl.when(s == 0)
    def _():
        carry_ref[...] = jnp.zeros_like(carry_ref)
        sp_ref[...] = _softplus_neg(lam_ref[...])

    seg_start = (c == 0) | (c == n_ctx_tiles)
    seg_end = (c == n_ctx_tiles - 1) | (c == nt - 1)
    a, bt = _lru_gates(prev_ref, cur_ref, next_ref, cw_ref, cb_ref, w_ref, b_ref, sp_ref,
                       seg_start, seg_end)
    hs, h = _lru_scan(a, bt, carry_ref[...], True)
    carry_ref[...] = h
    gate = jax.nn.gelu(lg_ref[0].astype(F32), approximate=True)
    o_ref[0] = ((hf_ref[0].astype(F32) + hs) * gate).astype(o_ref.dtype)


def _lru_call(lx, lg, cw, cb, wf, bf, wb, bb, lam, n_ctx_tiles):
    bsz, j, w = lx.shape
    nt = j // TM
    consts = [_full(cw.shape), _full((1, w)), _full(wf.shape), _full(bf.shape), _full((1, w))]
    scratch = [pltpu.VMEM((1, w), F32), pltpu.VMEM((1, w), F32)]
    blk = lambda f: pl.BlockSpec((1, TM, w), f)
    hf = pl.pallas_call(
        functools.partial(_lru_fwd_kernel, n_ctx_tiles=n_ctx_tiles),
        grid=(bsz, nt),
        in_specs=[
            blk(lambda b, c: (b, jnp.maximum(c - 1, 0), 0)),
            blk(lambda b, c: (b, c, 0)),
            blk(lambda b, c: (b, jnp.minimum(c + 1, nt - 1), 0)),
        ] + consts,
        out_specs=blk(lambda b, c: (b, c, 0)),
        out_shape=jax.ShapeDtypeStruct((bsz, j, w), BF16),
        scratch_shapes=scratch,
        compiler_params=_params(("arbitrary", "arbitrary")),
        name="lru_fwd",
    )(lx, lx, lx, cw, cb, wf, bf, lam[0:1])
    cix = lambda s: _bwd_chunk(s, nt, n_ctx_tiles)
    return pl.pallas_call(
        functools.partial(_lru_bwd_kernel, n_ctx_tiles=n_ctx_tiles),
        grid=(bsz, nt),
        in_specs=[
            blk(lambda b, s: (b, jnp.maximum(cix(s) - 1, 0), 0)),
            blk(lambda b, s: (b, cix(s), 0)),
            blk(lambda b, s: (b, jnp.minimum(cix(s) + 1, nt - 1), 0)),
        ] + consts + [
            blk(lambda b, s: (b, cix(s), 0)),
            blk(lambda b, s: (b, cix(s), 0)),
        ],
        out_specs=blk(lambda b, s: (b, cix(s), 0)),
        out_shape=jax.ShapeDtypeStruct((bsz, j, w), BF16),
        scratch_shapes=scratch,
        compiler_params=_params(("arbitrary", "arbitrary")),
        name="lru_bwd",
    )(lx, lx, lx, cw, cb, wb, bb, lam[1:2], hf, lg)


def _route(logits):
    lane = lax.broadcasted_iota(jnp.int32, logits.shape, 1)
    big = jnp.int32(LANES)
    is_g = lane < N_GROUPS
    mg = jnp.max(jnp.where(is_g, logits, -jnp.inf), axis=-1, keepdims=True)
    eg = jnp.where(is_g, jnp.exp(logits - mg), 0.0)
    pg = eg / jnp.sum(eg, axis=-1, keepdims=True)
    pg_top = jnp.max(pg, axis=-1, keepdims=True)
    g_idx = jnp.min(jnp.where(is_g & (pg == pg_top), lane, big), axis=-1, keepdims=True)
    lo = N_GROUPS + g_idx * EXPERTS_PER_GROUP
    sel = (lane >= lo) & (lane < lo + EXPERTS_PER_GROUP)
    me = jnp.max(jnp.where(sel, logits, -jnp.inf), axis=-1, keepdims=True)
    ee = jnp.where(sel, jnp.exp(logits - me), 0.0)
    pe = ee / jnp.sum(ee, axis=-1, keepdims=True)
    p1 = jnp.max(jnp.where(sel, pe, -1.0), axis=-1, keepdims=True)
    i1 = jnp.min(jnp.where(sel & (pe == p1), lane, big), axis=-1, keepdims=True)
    sel2 = sel & (lane != i1)
    p2 = jnp.max(jnp.where(sel2, pe, -1.0), axis=-1, keepdims=True)
    i2 = jnp.min(jnp.where(sel2 & (pe == p2), lane, big), axis=-1, keepdims=True)
    den = p1 + p2
    return jnp.where(lane == i1, pg_top * p1 / den, 0.0) + jnp.where(lane == i2, pg_top * p2 / den, 0.0)


def _outproj_kernel(om_ref, os_ref, ol_ref, x_ref, gt1_ref, sh2_ref, sc2_ref, gg_ref, wout_ref,
                    g2_ref, wr_ref, br_ref, x1_ref, h2_ref, comb_ref):
    d = x_ref.shape[-1]
    gg = gg_ref[...]
    nm = _rms(om_ref[0].astype(F32), MLA_OUT) * gg[:, 0:MLA_OUT]
    ns = _rms(os_ref[0].astype(F32), SWA_OUT) * gg[:, MLA_OUT:MLA_OUT + SWA_OUT]
    nl = _rms(ol_ref[0].astype(F32), LRU_WIDTH) * gg[:, MLA_OUT + SWA_OUT:]
    merged = jnp.concatenate([nm, ns, nl], axis=-1).astype(BF16)
    mix = _dot(merged, wout_ref[...])
    x1 = x_ref[0] + gt1_ref[0] * mix
    x1_ref[0] = x1
    h2 = (_rms(x1, d) * g2_ref[...] * (1.0 + sc2_ref[0]) + sh2_ref[0]).astype(BF16)
    h2_ref[0] = h2
    comb_ref[0] = _route(_dot(h2, wr_ref[...]) + br_ref[...])


def _outproj_call(om, osw, ol, xs, mod, gg, wout, g2, wr, br, n_ctx_tiles):
    bsz, j, d = xs.shape
    nt = j // TM
    tok = lambda w: pl.BlockSpec((1, TM, w), lambda i, b: (b, i, 0))
    return pl.pallas_call(
        _outproj_kernel,
        grid=(nt, bsz),
        in_specs=[
            tok(MLA_OUT), tok(SWA_OUT), tok(LRU_WIDTH), tok(d),
            _mod_spec(2, bsz, n_ctx_tiles, d, True),
            _mod_spec(3, bsz, n_ctx_tiles, d, True),
            _mod_spec(4, bsz, n_ctx_tiles, d, True),
            _full(gg.shape), _full(wout.shape), _full(g2.shape), _full(wr.shape), _full(br.shape),
        ],
        out_specs=[tok(d), tok(d), tok(LANES)],
        out_shape=[
            jax.ShapeDtypeStruct((bsz, j, d), F32),
            jax.ShapeDtypeStruct((bsz, j, d), BF16),
            jax.ShapeDtypeStruct((bsz, j, LANES), F32),
        ],
        compiler_params=_params(("arbitrary", "arbitrary")),
        name="outproj_route",
    )(om, osw, ol, xs, mod, mod, mod, gg, wout, g2, wr, br)


def _moe_kernel(h_ref, comb_ref, x_ref, gt2_ref, wg_ref, wu_ref, wd_ref, o_ref, acc_ref):
    e = pl.program_id(2)

    @pl.when(e == 0)
    def _():
        acc_ref[...] = jnp.zeros_like(acc_ref)

    h = h_ref[0]
    act = jax.nn.silu(_dot(h, wg_ref[0])) * _dot(h, wu_ref[0])
    comb = comb_ref[0]
    lane = lax.broadcasted_iota(jnp.int32, comb.shape, 1)
    c = jnp.sum(jnp.where(lane == e + N_GROUPS, comb, 0.0), axis=-1, keepdims=True)
    acc_ref[...] += _dot((c * act).astype(BF16), wd_ref[0])

    @pl.when(e == pl.num_programs(2) - 1)
    def _():
        o_ref[0] = x_ref[0] + gt2_ref[0] * acc_ref[...]


def _moe_call(h2, comb, x1, mod, wg, wu, wd, n_ctx_tiles):
    bsz, j, d = x1.shape
    tm = MOE_TM if (j % MOE_TM == 0 and (n_ctx_tiles * TM) % MOE_TM == 0) else TM
    nt = j // tm
    nc = n_ctx_tiles * TM // tm
    tok = lambda w: pl.BlockSpec((1, tm, w), lambda b, i, e: (b, i, 0))
    return pl.pallas_call(
        _moe_kernel,
        grid=(bsz, nt, N_EXPERTS),
        in_specs=[
            tok(d), tok(LANES), tok(d),
            pl.BlockSpec((1, 1, d), lambda b, i, e: (jnp.where(i < nc, bsz, b), 0, 5)),
            pl.BlockSpec((1, d, D_EXPERT), lambda b, i, e: (e, 0, 0)),
            pl.BlockSpec((1, d, D_EXPERT), lambda b, i, e: (e, 0, 0)),
            pl.BlockSpec((1, D_EXPERT, d), lambda b, i, e: (e, 0, 0)),
        ],
        out_specs=tok(d),
        out_shape=jax.ShapeDtypeStruct((bsz, j, d), F32),
        scratch_shapes=[pltpu.VMEM((tm, d), F32)],
        compiler_params=_params(("arbitrary", "arbitrary", "arbitrary")),
        name="moe_dense",
    )(h2, comb, x1, mod, wg, wu, wd)


def _final_kernel(x_ref, g_ref, o_ref):
    x = x_ref[0]
    o_ref[0] = _rms(x, x.shape[-1]) * g_ref[...]


def _final_call(xs, g, n_ctx_tiles):
    bsz, j, d = xs.shape
    nt = j // TM - n_ctx_tiles
    return pl.pallas_call(
        _final_kernel,
        grid=(bsz, nt),
        in_specs=[pl.BlockSpec((1, TM, d), lambda b, i: (b, i + n_ctx_tiles, 0)), _full(g.shape)],
        out_specs=pl.BlockSpec((1, TM, d), lambda b, i: (b, i, 0)),
        out_shape=jax.ShapeDtypeStruct((bsz, nt * TM, d), F32),
        compiler_params=_params(("arbitrary", "arbitrary")),
        name="final_norm",
    )(xs, g)


def _rope_tables(ctx_len, seq, groups, shift):
    t = jnp.arange(seq, dtype=jnp.int32)
    pos = {False: (t // GRID_W).astype(F32), True: (t % GRID_W).astype(F32)}
    freqs = ROPE_BASE ** (-jnp.arange(shift, dtype=F32) / shift)
    cos = jnp.ones((seq, LANES), F32)
    sina = jnp.zeros((seq, LANES), F32)
    sinb = jnp.zeros((seq, LANES), F32)
    for start, use_cols in groups:
        ang = pos[use_cols][:, None] * freqs[None, :]
        c, s = jnp.cos(ang), jnp.sin(ang)
        cos = cos.at[:, start:start + shift].set(c).at[:, start + shift:start + 2 * shift].set(c)
        sina = sina.at[:, start:start + shift].set(-s)
        sinb = sinb.at[:, start + shift:start + 2 * shift].set(s)
    pad = lambda a, v: jnp.concatenate([jnp.full((ctx_len, LANES), v, F32), a], axis=0)
    return pad(cos, 1.0), pad(sina, 0.0), pad(sinb, 0.0)


def _block_diag(w):
    n, c, d = w.shape
    eye = jnp.eye(n, dtype=w.dtype)
    return (w[:, :, None, :] * eye[:, None, :, None]).reshape(n * c, n * d)


def _prep_layer(l, w_in, g_cq, w_uq, g_ckv, w_ukv, conv_w, conv_b, lru_wa, lru_ba, lru_wx, lru_bx,
                g_grp, w_out, w_g1, b_g1, w_g2, b_g2):
    d = w_in.shape[1]
    wi = w_in[l]
    zeros = lambda n: jnp.zeros((d, n), wi.dtype)
    win = jnp.concatenate([
        wi[:, 0:384], zeros(64), wi[:, 384:416], zeros(32), wi[:, 416:1952]], axis=1).astype(BF16)
    hq = MLA_NOPE + MLA_ROPE
    wuq = jnp.concatenate(
        [jnp.pad(w_uq[l][:, h * hq:(h + 1) * hq], ((0, 0), (0, LANES - hq))) for h in range(MLA_HEADS)],
        axis=1).astype(BF16)
    wkv = w_ukv[l].reshape(MLA_KV_RANK, MLA_HEADS, MLA_NOPE + MLA_V)
    wuk = jnp.pad(wkv[:, :, :MLA_NOPE], ((0, 0), (0, 0), (0, LANES - MLA_NOPE))).reshape(MLA_KV_RANK, -1)
    wuv = wkv[:, :, MLA_NOPE:].reshape(MLA_KV_RANK, -1)
    wukv = jnp.concatenate([wuk, wuv], axis=1).astype(BF16)
    lru = []
    for dr in range(2):
        lru.append(jnp.concatenate([_block_diag(lru_wa[l, dr]), _block_diag(lru_wx[l, dr])], axis=1).astype(BF16))
        lru.append(jnp.concatenate([lru_ba[l, dr], lru_bx[l, dr]])[None, :])
    wr = jnp.concatenate([w_g1[l], jnp.moveaxis(w_g2[l], 0, 1).reshape(d, N_EXPERTS)], axis=1)
    wr = jnp.pad(wr, ((0, 0), (0, LANES - wr.shape[1]))).astype(BF16)
    br = jnp.pad(jnp.concatenate([b_g1[l], b_g2[l].reshape(-1)]), (0, LANES - N_GROUPS - N_EXPERTS))[None, :]
    return dict(win=win, gcq=g_cq[l][None, :], wuq=wuq, gckv=g_ckv[l][None, :], wukv=wukv,
                cw=conv_w[l], cb=conv_b[l][None, :], wf=lru[0], bf=lru[1], wb=lru[2], bb=lru[3],
                gg=g_grp[l][None, :], wout=w_out[l].astype(BF16), wr=wr, br=br)


def kernel(x, c, ctx, c_ctx, w_ada, b_ada, g_norm1, g_norm2, w_in, g_cq, w_uq, g_ckv, w_ukv, swa_sink,
           conv_w, conv_b, lru_wa, lru_ba, lru_wx, lru_bx, lru_lam, g_grp, w_out, w_g1, b_g1, w_g2, b_g2,
           w_e_gate, w_e_up, w_e_down, g_final):
    bsz, seq, d = x.shape
    ctx_len = ctx.shape[1]
    depth = w_ada.shape[0]
    assert seq % TM == 0 and ctx_len % TM == 0 and seq % GRID_W == 0
    n_ctx_tiles = ctx_len // TM

    rows = -(-(bsz + 1) // SUBLANES) * SUBLANES
    cc = jnp.pad(jnp.concatenate([c, c_ctx[None, :]], axis=0), ((0, rows - bsz - 1), (0, 0)))
    mods = _ada_call(cc, w_ada, b_ada)

    q4 = MLA_ROPE // 4
    s4 = SWA_HEAD_DIM // 4
    mla_tabs = _rope_tables(ctx_len, seq, [(MLA_NOPE, False), (MLA_NOPE + 2 * q4, True)], q4)
    swa_tabs = _rope_tables(
        ctx_len, seq,
        [(hh * SWA_HEAD_DIM + ax * 2 * s4, bool(ax)) for hh in range(LANES // SWA_HEAD_DIM) for ax in range(2)], s4)
    tabs = mla_tabs + swa_tabs

    xs = jnp.concatenate([ctx, x], axis=1)
    for l in range(depth):
        p = _prep_layer(l, w_in, g_cq, w_uq, g_ckv, w_ukv, conv_w, conv_b, lru_wa, lru_ba, lru_wx, lru_bx,
                        g_grp, w_out, w_g1, b_g1, w_g2, b_g2)
        mod = mods[l].reshape(rows, 1, 6 * d)
        qm, km, vm, sq, sk, sv, lx, lg = _inproj_call(
            xs, mod, g_norm1[l][None, :], p["win"], p["gcq"], p["wuq"], p["gckv"], p["wukv"], tabs, n_ctx_tiles)
        om = _mla_call(qm, km, vm, n_ctx_tiles, ctx_len)
        osw = _swa_call(swa_sink[l], sq, sk, sv, ctx_len)
        ol = _lru_call(lx, lg, p["cw"], p["cb"], p["wf"], p["bf"], p["wb"], p["bb"], lru_lam[l], n_ctx_tiles)
        x1, h2, comb = _outproj_call(om, osw, ol, xs, mod, p["gg"], p["wout"], g_norm2[l][None, :],
                                     p["wr"], p["br"], n_ctx_tiles)
        xs = _moe_call(h2, comb, x1, mod, w_e_gate[l].astype(BF16), w_e_up[l].astype(BF16),
                       w_e_down[l].astype(BF16), n_ctx_tiles)
    return _final_call(xs, g_final[None, :], n_ctx_tiles)
```

```python
import functools

import jax
import jax.numpy as jnp
from jax import lax
from jax.experimental import pallas as pl
from jax.experimental.pallas import tpu as pltpu

GRID_W = 64
EPS = 1e-6
ROPE_BASE = 10000.0
NEG_INF = -1e30
MLA_HEADS = 4
MLA_NOPE = 64
MLA_ROPE = 32
MLA_V = 64
MLA_Q_RANK = 256
MLA_KV_RANK = 128
MLA_SCALE = (MLA_NOPE + MLA_ROPE) ** -0.5
SWA_HEADS = 4
SWA_KV_HEADS = 2
SWA_HEAD_DIM = 64
SWA_SCALE = SWA_HEAD_DIM ** -0.5
WINDOW = 128
LRU_WIDTH = 512
LRU_BLOCKS = 8
LRU_BW = LRU_WIDTH // LRU_BLOCKS
CONV_W = 4
LRU_C = 8.0
N_GROUPS = 4
EXPERTS_PER_GROUP = 8
N_EXPERTS = N_GROUPS * EXPERTS_PER_GROUP
D_EXPERT = 256
MLA_OUT = MLA_HEADS * MLA_V
SWA_OUT = SWA_HEADS * SWA_HEAD_DIM

LANES = 128
SUBLANES = 8
TM = 256
ZW = 2048
N_PAIRS = EXPERTS_PER_GROUP * (EXPERTS_PER_GROUP - 1) // 2
N_BUCKETS = N_GROUPS * N_PAIRS
MOE_TG = 256
MOE_TD = 512
LOG2E = 1.4426950408889634
VMEM_LIMIT = 56 * 1024 * 1024

F32 = jnp.float32
BF16 = jnp.bfloat16


def _params(sem):
    return pltpu.CompilerParams(dimension_semantics=sem, vmem_limit_bytes=VMEM_LIMIT)


def _full(shape):
    n = len(shape)
    return pl.BlockSpec(shape, lambda *_: (0,) * n)


def _rms(x, width):
    return x * lax.rsqrt(jnp.sum(x * x, axis=-1, keepdims=True) * (1.0 / width) + EPS)


def _dot(a, b):
    return jnp.dot(a, b, preferred_element_type=F32)


def _dot_nt(a, b):
    return lax.dot_general(a, b, (((1,), (1,)), ((), ())), preferred_element_type=F32)


def _ada_kernel(c_ref, w_ref, b_ref, o_ref):
    c = c_ref[...]
    s = c * jax.nn.sigmoid(c)
    o_ref[0] = jnp.dot(s, w_ref[0], preferred_element_type=F32,
                       precision=lax.Precision.HIGHEST) + b_ref[0]


def _ada_call(cc, w_ada, b_ada):
    depth, d, n = w_ada.shape
    r = cc.shape[0]
    tn = 1536
    return pl.pallas_call(
        _ada_kernel,
        grid=(depth, n // tn),
        in_specs=[
            pl.BlockSpec((r, d), lambda l, j: (0, 0)),
            pl.BlockSpec((1, d, tn), lambda l, j: (l, 0, j)),
            pl.BlockSpec((1, 1, tn), lambda l, j: (l, 0, j)),
        ],
        out_specs=pl.BlockSpec((1, r, tn), lambda l, j: (l, 0, j)),
        out_shape=jax.ShapeDtypeStruct((depth, r, n), F32),
        compiler_params=_params(("arbitrary", "arbitrary")),
        name="adaln",
    )(cc, w_ada, b_ada.reshape(depth, 1, n))


def _rope(x, cos, sina, sinb, shift):
    n = x.shape[-1]
    reps = n // LANES
    if reps > 1:
        cos = jnp.concatenate([cos] * reps, axis=-1)
        sina = jnp.concatenate([sina] * reps, axis=-1)
        sinb = jnp.concatenate([sinb] * reps, axis=-1)
    return x * cos + pltpu.roll(x, n - shift, 1) * sina + pltpu.roll(x, shift, 1) * sinb


def _inproj_kernel(x_ref, sh_ref, sc_ref, g1_ref, win_ref, gcq_ref, wuq_ref, gckv_ref, wukv_ref,
                   mcos_ref, msa_ref, msb_ref, scos_ref, ssa_ref, ssb_ref,
                   qm_ref, km_ref, vm_ref, sq_ref, sk_ref, sv_ref, lx_ref, lg_ref):
    d = x_ref.shape[-1]
    x = x_ref[0]
    h = _rms(x, d) * g1_ref[...] * (1.0 + sc_ref[0]) + sh_ref[0]
    z = _dot(h.astype(BF16), win_ref[...])

    mcos, msa, msb = mcos_ref[...], msa_ref[...], msb_ref[...]
    scos, ssa, ssb = scos_ref[...], ssa_ref[...], ssb_ref[...]

    cq = _rms(z[:, 0:256], MLA_Q_RANK) * gcq_ref[...]
    q = _dot(cq.astype(BF16), wuq_ref[...])
    q = _rope(q, mcos, msa, msb, MLA_ROPE // 4)
    qm_ref[0] = (q * (MLA_SCALE * LOG2E)).astype(BF16)

    ckv = _rms(z[:, 256:384], MLA_KV_RANK) * gckv_ref[...]
    kv = _dot(ckv.astype(BF16), wukv_ref[...])
    kr = _rope(z[:, 384:512], mcos, msa, msb, MLA_ROPE // 4)
    km_ref[0] = (kv[:, 0:512] + jnp.concatenate([kr] * MLA_HEADS, axis=-1)).astype(BF16)
    ones = jnp.ones((kv.shape[0], LANES), F32)
    vm_ref[0] = jnp.concatenate([kv[:, 512:640], ones, kv[:, 640:768], ones], axis=-1).astype(BF16)

    sq_ref[0] = (_rope(z[:, 512:768], scos, ssa, ssb, SWA_HEAD_DIM // 4) * SWA_SCALE).astype(BF16)
    sk_ref[0] = _rope(z[:, 768:896], scos, ssa, ssb, SWA_HEAD_DIM // 4).astype(BF16)
    sv_ref[0] = z[:, 896:1024].astype(BF16)
    lx_ref[0] = z[:, 1024:1536].astype(BF16)
    lg_ref[0] = z[:, 1536:2048].astype(BF16)


def _mod_spec(j, b_rows, n_ctx_tiles, d, tile_major):
    if tile_major:
        return pl.BlockSpec((1, 1, d), lambda i, b: (jnp.where(i < n_ctx_tiles, b_rows, b), 0, j))
    return pl.BlockSpec((1, 1, d), lambda b, i: (jnp.where(i < n_ctx_tiles, b_rows, b), 0, j))


def _inproj_call(xs, mod, g1, win, gcq, wuq, gckv, wukv, tabs, n_ctx_tiles):
    bsz, j, d = xs.shape
    nt = j // TM
    tok = lambda w: pl.BlockSpec((1, TM, w), lambda i, b: (b, i, 0))
    tab = pl.BlockSpec((TM, LANES), lambda i, b: (i, 0))
    widths = (512, 512, 512, 256, 128, 128, 512, 512)
    return pl.pallas_call(
        _inproj_kernel,
        grid=(nt, bsz),
        in_specs=[
            tok(d),
            _mod_spec(0, bsz, n_ctx_tiles, d, True),
            _mod_spec(1, bsz, n_ctx_tiles, d, True),
            _full(g1.shape), _full(win.shape), _full(gcq.shape), _full(wuq.shape),
            _full(gckv.shape), _full(wukv.shape),
            tab, tab, tab, tab, tab, tab,
        ],
        out_specs=[tok(w) for w in widths],
        out_shape=[jax.ShapeDtypeStruct((bsz, j, w), BF16) for w in widths],
        compiler_params=_params(("arbitrary", "arbitrary")),
        name="inproj",
    )(xs, mod, mod, g1, win, gcq, wuq, gckv, wukv, *tabs)


def _mla_heads(q_ref, k_ref, v_ref, o_ref, klen):
    lane = lax.broadcasted_iota(jnp.int32, (q_ref.shape[1], LANES), 1)
    for hp in range(MLA_HEADS // 2):
        v = v_ref[0, 0:klen, 2 * hp * LANES:2 * (hp + 1) * LANES]
        outs = []
        for h in range(2 * hp, 2 * hp + 2):
            q = q_ref[0, :, h * LANES:(h + 1) * LANES]
            k = k_ref[0, 0:klen, h * LANES:(h + 1) * LANES]
            s = _dot_nt(q, k)
            m = jnp.max(s, axis=-1, keepdims=True)
            o = _dot(jnp.exp2(s - m).astype(BF16), v)
            outs.append(o[:, 0:LANES] / o[:, LANES:LANES + 1])
        o_ref[0, :, hp * LANES:(hp + 1) * LANES] = jnp.where(lane < MLA_V, outs[0], outs[1]).astype(o_ref.dtype)


def _mla_kernel(q_ref, k_ref, v_ref, o_ref, *, n_ctx_tiles, ctx_len):
    i = pl.program_id(1)

    @pl.when(i < n_ctx_tiles)
    def _():
        _mla_heads(q_ref, k_ref, v_ref, o_ref, ctx_len)

    @pl.when(i >= n_ctx_tiles)
    def _():
        _mla_heads(q_ref, k_ref, v_ref, o_ref, k_ref.shape[1])


def _mla_call(qm, km, vm, n_ctx_tiles, ctx_len):
    bsz, j, w = qm.shape
    nt = j // TM
    return pl.pallas_call(
        functools.partial(_mla_kernel, n_ctx_tiles=n_ctx_tiles, ctx_len=ctx_len),
        grid=(bsz, nt),
        in_specs=[
            pl.BlockSpec((1, TM, w), lambda b, i: (b, i, 0)),
            pl.BlockSpec((1, j, w), lambda b, i: (b, 0, 0)),
            pl.BlockSpec((1, j, w), lambda b, i: (b, 0, 0)),
        ],
        out_specs=pl.BlockSpec((1, TM, MLA_OUT), lambda b, i: (b, i, 0)),
        out_shape=jax.ShapeDtypeStruct((bsz, j, MLA_OUT), BF16),
        compiler_params=_params(("arbitrary", "arbitrary")),
        name="mla_attn",
    )(qm, km, vm)


def _swa_kernel(sink_ref, q_ref, k_ref, v_ref, o_ref, *, ctx_len):
    i = pl.program_id(1)
    j = k_ref.shape[1]
    span = TM + 2 * WINDOW
    row0 = pl.multiple_of(jnp.clip(i * TM - WINDOW, 0, j - span), WINDOW)
    qpos = i * TM - ctx_len + lax.broadcasted_iota(jnp.int32, (TM, span), 0)
    kpos = row0 - ctx_len + lax.broadcasted_iota(jnp.int32, (TM, span), 1)
    valid = (jnp.abs(qpos - kpos) <= WINDOW) & (kpos >= 0) & (qpos >= 0)
    outs = []
    for h in range(SWA_HEADS):
        kvh = h // (SWA_HEADS // SWA_KV_HEADS)
        cols = slice(kvh * SWA_HEAD_DIM, (kvh + 1) * SWA_HEAD_DIM)
        q = q_ref[0, :, h * SWA_HEAD_DIM:(h + 1) * SWA_HEAD_DIM]
        kl = k_ref[0, pl.ds(row0, span), cols]
        vl = v_ref[0, pl.ds(row0, span), cols]
        kc = k_ref[0, 0:ctx_len, cols]
        vc = v_ref[0, 0:ctx_len, cols]
        s_loc = jnp.where(valid, _dot_nt(q, kl), NEG_INF)
        s_ctx = _dot_nt(q, kc)
        sink = sink_ref[h]
        m = jnp.maximum(jnp.maximum(jnp.max(s_loc, axis=-1, keepdims=True),
                                    jnp.max(s_ctx, axis=-1, keepdims=True)), sink)
        p_loc = jnp.exp(s_loc - m)
        p_ctx = jnp.exp(s_ctx - m)
        l = (jnp.sum(p_loc, axis=-1, keepdims=True) + jnp.sum(p_ctx, axis=-1, keepdims=True)
             + jnp.exp(sink - m))
        o = _dot(p_loc.astype(BF16), vl) + _dot(p_ctx.astype(BF16), vc)
        outs.append(o / l)
    o_ref[0] = jnp.concatenate(outs, axis=-1).astype(o_ref.dtype)


def _swa_call(sink, sq, sk, sv, ctx_len):
    bsz, j, _ = sq.shape
    nt = j // TM
    return pl.pallas_call(
        functools.partial(_swa_kernel, ctx_len=ctx_len),
        grid=(bsz, nt),
        in_specs=[
            pl.BlockSpec(memory_space=pltpu.SMEM),
            pl.BlockSpec((1, TM, SWA_OUT), lambda b, i: (b, i, 0)),
            pl.BlockSpec((1, j, SWA_KV_HEADS * SWA_HEAD_DIM), lambda b, i: (b, 0, 0)),
            pl.BlockSpec((1, j, SWA_KV_HEADS * SWA_HEAD_DIM), lambda b, i: (b, 0, 0)),
        ],
        out_specs=pl.BlockSpec((1, TM, SWA_OUT), lambda b, i: (b, i, 0)),
        out_shape=jax.ShapeDtypeStruct((bsz, j, SWA_OUT), BF16),
        compiler_params=_params(("arbitrary", "arbitrary")),
        name="swa_attn",
    )(sink, sq, sk, sv)


def _lru_gates(prev_ref, cur_ref, next_ref, cw_ref, cb_ref, w_ref, b_ref, sp_ref, seg_start, seg_end):
    z = cur_ref[0].astype(F32)
    row = lax.broadcasted_iota(jnp.int32, z.shape, 0)
    keep_prev = jnp.where(seg_start, 0.0, 1.0)
    keep_next = jnp.where(seg_end, 0.0, 1.0)
    p2 = prev_ref[0, TM - 2:TM - 1, :].astype(F32) * keep_prev
    p1 = prev_ref[0, TM - 1:TM, :].astype(F32) * keep_prev
    n0 = next_ref[0, 0:1, :].astype(F32) * keep_next
    z_m1 = jnp.where(row == 0, p1, pltpu.roll(z, 1, 0))
    z_m2 = jnp.where(row == 0, p2, jnp.where(row == 1, p1, pltpu.roll(z, 2, 0)))
    z_p1 = jnp.where(row == TM - 1, n0, pltpu.roll(z, TM - 1, 0))
    u = cb_ref[...] + z_m2 * cw_ref[0:1, :]
    u = u + z_m1 * cw_ref[1:2, :]
    u = u + z * cw_ref[2:3, :]
    u = u + z_p1 * cw_ref[3:4, :]
    g = _dot(u.astype(BF16), w_ref[...]) + b_ref[...]
    r = jax.nn.sigmoid(g[:, 0:LRU_WIDTH])
    ig = jax.nn.sigmoid(g[:, LRU_WIDTH:2 * LRU_WIDTH])
    log_a = (-LRU_C) * r * sp_ref[...]
    a = jnp.exp(log_a)
    bt = jnp.sqrt(1.0 - a * a) * (ig * u)
    return a, bt


def _lru_scan(a, b, h0, reverse):
    t, w = a.shape
    g = t // SUBLANES
    a = a.reshape(g, SUBLANES, w)
    b = b.reshape(g, SUBLANES, w)
    sub = lax.broadcasted_iota(jnp.int32, (g, SUBLANES, w), 1)
    d = 1
    while d < SUBLANES:
        if reverse:
            shift, ok = SUBLANES - d, sub < SUBLANES - d
        else:
            shift, ok = d, sub >= d
        a_sh = pltpu.roll(a, shift, 1)
        b_sh = pltpu.roll(b, shift, 1)
        b = jnp.where(ok, a * b_sh + b, b)
        a = jnp.where(ok, a * a_sh, a)
        d *= 2
    hs = [None] * g
    h = h0
    order = range(g - 1, -1, -1) if reverse else range(g)
    for gi in order:
        hg = a[gi] * h + b[gi]
        hs[gi] = hg
        h = hg[0:1, :] if reverse else hg[SUBLANES - 1:SUBLANES, :]
    return jnp.concatenate(hs, axis=0), h


def _softplus_neg(lam):
    x = -lam
    return jnp.maximum(x, 0.0) + jnp.log1p(jnp.exp(-jnp.abs(x)))


def _lru_fwd_kernel(prev_ref, cur_ref, next_ref, cw_ref, cb_ref, w_ref, b_ref, lam_ref,
                    hf_ref, carry_ref, sp_ref, *, n_ctx_tiles):
    c = pl.program_id(1)
    nt = pl.num_programs(1)

    @pl.when(c == 0)
    def _():
        carry_ref[...] = jnp.zeros_like(carry_ref)
        sp_ref[...] = _softplus_neg(lam_ref[...])

    seg_start = (c == 0) | (c == n_ctx_tiles)
    seg_end = (c == n_ctx_tiles - 1) | (c == nt - 1)
    a, bt = _lru_gates(prev_ref, cur_ref, next_ref, cw_ref, cb_ref, w_ref, b_ref, sp_ref,
                       seg_start, seg_end)
    hs, h = _lru_scan(a, bt, carry_ref[...], False)
    carry_ref[...] = h
    hf_ref[0] = hs.astype(hf_ref.dtype)


def _bwd_chunk(s, nt, n_ctx_tiles):
    return jnp.where(s < n_ctx_tiles, n_ctx_tiles - 1 - s, nt - 1 - (s - n_ctx_tiles))


def _lru_bwd_kernel(prev_ref, cur_ref, next_ref, cw_ref, cb_ref, w_ref, b_ref, lam_ref,
                    hf_ref, lg_ref, o_ref, carry_ref, sp_ref, *, n_ctx_tiles):
    s = pl.program_id(1)
    nt = pl.num_programs(1)
    c = _bwd_chunk(s, nt, n_ctx_tiles)

    @pl.when(s == 0)
    def _():
        carry_ref[...] = jnp.zeros_like(carry_ref)
        sp_ref[...] = _softplus_neg(lam_ref[...])

    seg_start = (c == 0) | (c == n_ctx_tiles)
    seg_end = (c == n_ctx_tiles - 1) | (c == nt - 1)
    a, bt = _lru_gates(prev_ref, cur_ref, next_ref, cw_ref, cb_ref, w_ref, b_ref, sp_ref,
                       seg_start, seg_end)
    hs, h = _lru_scan(a, bt, carry_ref[...], True)
    carry_ref[...] = h
    gate = jax.nn.gelu(lg_ref[0].astype(F32), approximate=True)
    o_ref[0] = ((hf_ref[0].astype(F32) + hs) * gate).astype(o_ref.dtype)


def _lru_call(lx, lg, cw, cb, wf, bf, wb, bb, lam, n_ctx_tiles):
    bsz, j, w = lx.shape
    nt = j // TM
    consts = [_full(cw.shape), _full((1, w)), _full(wf.shape), _full(bf.shape), _full((1, w))]
    scratch = [pltpu.VMEM((1, w), F32), pltpu.VMEM((1, w), F32)]
    blk = lambda f: pl.BlockSpec((1, TM, w), f)
    hf = pl.pallas_call(
        functools.partial(_lru_fwd_kernel, n_ctx_tiles=n_ctx_tiles),
        grid=(bsz, nt),
        in_specs=[
            blk(lambda b, c: (b, jnp.maximum(c - 1, 0), 0)),
            blk(lambda b, c: (b, c, 0)),
            blk(lambda b, c: (b, jnp.minimum(c + 1, nt - 1), 0)),
        ] + consts,
        out_specs=blk(lambda b, c: (b, c, 0)),
        out_shape=jax.ShapeDtypeStruct((bsz, j, w), BF16),
        scratch_shapes=scratch,
        compiler_params=_params(("arbitrary", "arbitrary")),
        name="lru_fwd",
    )(lx, lx, lx, cw, cb, wf, bf, lam[0:1])
    cix = lambda s: _bwd_chunk(s, nt, n_ctx_tiles)
    return pl.pallas_call(
        functools.partial(_lru_bwd_kernel, n_ctx_tiles=n_ctx_tiles),
        grid=(bsz, nt),
        in_specs=[
            blk(lambda b, s: (b, jnp.maximum(cix(s) - 1, 0), 0)),
            blk(lambda b, s: (b, cix(s), 0)),
            blk(lambda b, s: (b, jnp.minimum(cix(s) + 1, nt - 1), 0)),
        ] + consts + [
            blk(lambda b, s: (b, cix(s), 0)),
            blk(lambda b, s: (b, cix(s), 0)),
        ],
        out_specs=blk(lambda b, s: (b, cix(s), 0)),
        out_shape=jax.ShapeDtypeStruct((bsz, j, w), BF16),
        scratch_shapes=scratch,
        compiler_params=_params(("arbitrary", "arbitrary")),
        name="lru_bwd",
    )(lx, lx, lx, cw, cb, wb, bb, lam[1:2], hf, lg)


def _route(logits, carry):
    tm = logits.shape[0]
    lane = lax.broadcasted_iota(jnp.int32, logits.shape, 1)
    big = jnp.int32(LANES)
    is_g = (lane >= N_EXPERTS) & (lane < N_EXPERTS + N_GROUPS)
    mg = jnp.max(jnp.where(is_g, logits, -jnp.inf), axis=-1, keepdims=True)
    eg = jnp.where(is_g, jnp.exp(logits - mg), 0.0)
    pg = eg / jnp.sum(eg, axis=-1, keepdims=True)
    pg_top = jnp.max(pg, axis=-1, keepdims=True)
    g_idx = jnp.min(jnp.where(is_g & (pg == pg_top), lane, big), axis=-1, keepdims=True) - N_EXPERTS
    lo = g_idx * EXPERTS_PER_GROUP
    sel = (lane >= lo) & (lane < lo + EXPERTS_PER_GROUP)
    me = jnp.max(jnp.where(sel, logits, -jnp.inf), axis=-1, keepdims=True)
    ee = jnp.where(sel, jnp.exp(logits - me), 0.0)
    pe = ee / jnp.sum(ee, axis=-1, keepdims=True)
    p1 = jnp.max(jnp.where(sel, pe, -1.0), axis=-1, keepdims=True)
    i1 = jnp.min(jnp.where(sel & (pe == p1), lane, big), axis=-1, keepdims=True)
    sel2 = sel & (lane != i1)
    p2 = jnp.max(jnp.where(sel2, pe, -1.0), axis=-1, keepdims=True)
    i2 = jnp.min(jnp.where(sel2 & (pe == p2), lane, big), axis=-1, keepdims=True)
    den = p1 + p2
    w1 = pg_top * p1 / den
    w2 = pg_top * p2 / den
    first_lo = i1 < i2
    ia = (jnp.where(first_lo, i1, i2) - lo).astype(F32)
    ib = (jnp.where(first_lo, i2, i1) - lo).astype(F32)
    wa = jnp.where(first_lo, w1, w2)
    wb = jnp.where(first_lo, w2, w1)
    pair = ia * (2 * EXPERTS_PER_GROUP - 1 - ia) * 0.5 + (ib - ia - 1.0)
    bucket = (g_idx.astype(F32) * N_PAIRS + pair).astype(jnp.int32)
    mine = lane == bucket
    picks = jnp.where(mine, 1.0, 0.0)
    tri = jnp.where(lax.broadcasted_iota(jnp.int32, (tm, tm), 0) > lax.broadcasted_iota(jnp.int32, (tm, tm), 1),
                    1.0, 0.0).astype(BF16)
    before = _dot(tri, picks.astype(BF16)) + carry
    rank = jnp.sum(jnp.where(mine, before, 0.0), axis=-1, keepdims=True)
    cols = jnp.where(lane == 0, bucket.astype(F32), jnp.where(lane == 1, rank, 0.0))
    wcols = jnp.where(lane == 0, wa, jnp.where(lane == 1, wb, 0.0))
    return cols, wcols, carry + jnp.sum(picks, axis=0, keepdims=True)


def _outproj_kernel(om_ref, os_ref, ol_ref, x_ref, gt1_ref, sh2_ref, sc2_ref, gg_ref, wout_ref,
                    g2_ref, wr_ref, br_ref, x1_ref, h2_ref, meta_ref, cnt_ref, carry_ref):
    d = x_ref.shape[-1]

    @pl.when((pl.program_id(0) == 0) & (pl.program_id(1) == 0))
    def _():
        carry_ref[...] = jnp.zeros_like(carry_ref)

    gg = gg_ref[...]
    nm = _rms(om_ref[0].astype(F32), MLA_OUT) * gg[:, 0:MLA_OUT]
    ns = _rms(os_ref[0].astype(F32), SWA_OUT) * gg[:, MLA_OUT:MLA_OUT + SWA_OUT]
    nl = _rms(ol_ref[0].astype(F32), LRU_WIDTH) * gg[:, MLA_OUT + SWA_OUT:]
    merged = jnp.concatenate([nm, ns, nl], axis=-1).astype(BF16)
    mix = _dot(merged, wout_ref[...])
    x1 = x_ref[0] + gt1_ref[0] * mix
    x1_ref[0] = x1
    h2 = _rms(x1, d) * g2_ref[...] * (1.0 + sc2_ref[0]) + sh2_ref[0]
    cols, wcols, carry = _route(_dot(h2.astype(BF16), wr_ref[...]) + br_ref[...], carry_ref[...])
    h2_ref[0] = jnp.concatenate([h2, wcols], axis=-1)
    carry_ref[...] = carry
    cnt_ref[...] = carry.astype(jnp.int32)
    meta = cols.T[0:SUBLANES, :].astype(jnp.int32)
    for m in range(TM // LANES):
        meta_ref[m] = meta[:, m * LANES:(m + 1) * LANES]


def _outproj_call(om, osw, ol, xs, mod, gg, wout, g2, wr, br, n_ctx_tiles):
    bsz, j, d = xs.shape
    nt = j // TM
    tok = lambda w: pl.BlockSpec((1, TM, w), lambda i, b: (b, i, 0))
    return pl.pallas_call(
        _outproj_kernel,
        grid=(nt, bsz),
        in_specs=[
            tok(MLA_OUT), tok(SWA_OUT), tok(LRU_WIDTH), tok(d),
            _mod_spec(2, bsz, n_ctx_tiles, d, True),
            _mod_spec(3, bsz, n_ctx_tiles, d, True),
            _mod_spec(4, bsz, n_ctx_tiles, d, True),
            _full(gg.shape), _full(wout.shape), _full(g2.shape), _full(wr.shape), _full(br.shape),
        ],
        out_specs=[
            tok(d), tok(d + LANES),
            pl.BlockSpec((TM // LANES, SUBLANES, LANES), lambda i, b: (b * nt + i, 0, 0)),
            pl.BlockSpec((1, LANES), lambda i, b: (0, 0)),
        ],
        out_shape=[
            jax.ShapeDtypeStruct((bsz, j, d), F32),
            jax.ShapeDtypeStruct((bsz, j, d + LANES), F32),
            jax.ShapeDtypeStruct((bsz * j // LANES, SUBLANES, LANES), jnp.int32),
            jax.ShapeDtypeStruct((1, LANES), jnp.int32),
        ],
        scratch_shapes=[pltpu.VMEM((1, LANES), F32)],
        compiler_params=_params(("arbitrary", "arbitrary")),
        name="outproj_route",
    )(om, osw, ol, xs, mod, mod, mod, gg, wout, g2, wr, br)


def _padded_tiles(count):
    return lax.shift_right_logical(count + (MOE_TG - 1), MOE_TG.bit_length() - 1)


def _plan_kernel(cnt_ref, off_ref, tea_ref, teb_ref, nu_ref, *, n_tiles_max):
    off = jnp.int32(0)
    ti = jnp.int32(0)
    bucket = 0
    for g in range(N_GROUPS):
        for a in range(EXPERTS_PER_GROUP):
            for b in range(a + 1, EXPERTS_PER_GROUP):
                n_q = _padded_tiles(cnt_ref[0, bucket])
                off_ref[bucket] = off

                def fill(k, _, base=ti, ea=g * EXPERTS_PER_GROUP + a, eb=g * EXPERTS_PER_GROUP + b):
                    tea_ref[base + k] = ea
                    teb_ref[base + k] = eb
                    return 0

                lax.fori_loop(0, n_q, fill, 0)
                off = off + n_q * MOE_TG
                ti = ti + n_q
                bucket += 1
    nu_ref[0] = ti

    def rest(k, _):
        tea_ref[k] = N_EXPERTS - 2
        teb_ref[k] = N_EXPERTS - 1
        return 0

    lax.fori_loop(ti, n_tiles_max, rest, 0)


def _plan_call(cnt, n_tiles_max):
    smem = pl.BlockSpec(memory_space=pltpu.SMEM)
    return pl.pallas_call(
        functools.partial(_plan_kernel, n_tiles_max=n_tiles_max),
        in_specs=[smem],
        out_specs=[smem, smem, smem, smem],
        out_shape=[
            jax.ShapeDtypeStruct((N_BUCKETS,), jnp.int32),
            jax.ShapeDtypeStruct((n_tiles_max,), jnp.int32),
            jax.ShapeDtypeStruct((n_tiles_max,), jnp.int32),
            jax.ShapeDtypeStruct((1,), jnp.int32),
        ],
        name="moe_plan",
    )(cnt)


def _row_copy(src_ref, src_row, dst_ref, dst_row, sem):
    return pltpu.make_async_copy(src_ref.at[pl.ds(src_row, 1)], dst_ref.at[pl.ds(dst_row, 1)], sem)


def _dispatch_kernel(off_ref, nu_ref, h_ref, meta_ref, cnt_ref, xs_ref, zero_ref, sem, tile_sem, *, tile):
    n_tiles_max = xs_ref.shape[0] // MOE_TG

    @pl.when(pl.program_id(0) == 0)
    def _():
        zero_ref[...] = jnp.zeros_like(zero_ref)

        def per_bucket(q, n):
            c = cnt_ref[0, q]
            padded = _padded_tiles(c) * MOE_TG
            base = off_ref[q]

            def fill(r, _):
                _row_copy(zero_ref, 0, xs_ref, base + r, sem).start()
                return 0

            lax.fori_loop(c, padded, fill, 0)
            return n + (padded - c)

        n_pad = lax.fori_loop(0, N_BUCKETS, per_bucket, jnp.int32(0))

        def drain(r, _):
            _row_copy(zero_ref, 0, xs_ref, 0, sem).wait()
            return 0

        lax.fori_loop(0, n_pad, drain, 0)

        def unused_tile(ti, _):
            cp = pltpu.make_async_copy(zero_ref, xs_ref.at[pl.ds(ti * MOE_TG, MOE_TG)], tile_sem)
            cp.start()
            cp.wait()
            return 0

        lax.fori_loop(nu_ref[0], n_tiles_max, unused_tile, 0)

    for m in range(tile // LANES):
        def send(r8, _):
            for jj in range(SUBLANES):
                r = r8 * SUBLANES + jj
                dst = off_ref[meta_ref[m, 0, r]] + meta_ref[m, 1, r]
                _row_copy(h_ref, m * LANES + r, xs_ref, dst, sem).start()
            return 0

        lax.fori_loop(0, LANES // SUBLANES, send, 0)

    def drain_rows(r, _):
        _row_copy(h_ref, 0, xs_ref, 0, sem).wait()
        return 0

    lax.fori_loop(0, tile, drain_rows, 0, unroll=8)


def _dispatch_call(off, nu, h2, meta, cnt, rows_max):
    t, hw = h2.shape
    tile = MOE_TD if t % MOE_TD == 0 else TM
    return pl.pallas_call(
        functools.partial(_dispatch_kernel, tile=tile),
        grid_spec=pltpu.PrefetchScalarGridSpec(
            num_scalar_prefetch=2,
            grid=(t // tile,),
            in_specs=[
                pl.BlockSpec((tile, hw), lambda i, off, nu: (i, 0)),
                pl.BlockSpec((tile // LANES, SUBLANES, LANES), lambda i, off, nu: (i, 0, 0),
                             memory_space=pltpu.SMEM),
                pl.BlockSpec(memory_space=pltpu.SMEM),
            ],
            out_specs=pl.BlockSpec(memory_space=pl.ANY),
            scratch_shapes=[pltpu.VMEM((MOE_TG, hw), F32), pltpu.SemaphoreType.DMA, pltpu.SemaphoreType.DMA],
        ),
        out_shape=jax.ShapeDtypeStruct((rows_max, hw), F32),
        compiler_params=_params(("arbitrary",)),
        name="moe_dispatch",
    )(off, nu, h2, meta, cnt)


def _expert_mlp(x, wg_ref, wu_ref, wd_ref):
    act = jax.nn.silu(_dot(x, wg_ref[0])) * _dot(x, wu_ref[0])
    return _dot(act.astype(BF16), wd_ref[0])


def _expert_kernel(tea_ref, teb_ref, nu_ref, x_ref, wga_ref, wua_ref, wda_ref, wgb_ref, wub_ref, wdb_ref, y_ref):
    d = y_ref.shape[-1]

    @pl.when(pl.program_id(0) < nu_ref[0])
    def _():
        x = x_ref[:, 0:d].astype(BF16)
        wa = x_ref[:, d:d + 1]
        wb = x_ref[:, d + 1:d + 2]
        y_ref[...] = wa * _expert_mlp(x, wga_ref, wua_ref, wda_ref) + wb * _expert_mlp(x, wgb_ref, wub_ref, wdb_ref)

    @pl.when(pl.program_id(0) >= nu_ref[0])
    def _():
        y_ref[...] = jnp.zeros_like(y_ref)


def _expert_call(tea, teb, nu, xs, wg, wu, wd):
    rows_max, hw = xs.shape
    d = hw - LANES
    n_tiles = rows_max // MOE_TG
    used = lambda i, nu: jnp.minimum(i, nu[0] - 1)
    wspec = lambda shape, te_pos: pl.BlockSpec(
        shape, lambda i, tea, teb, nu: ((tea, teb)[te_pos][used(i, nu)], 0, 0))
    return pl.pallas_call(
        _expert_kernel,
        grid_spec=pltpu.PrefetchScalarGridSpec(
            num_scalar_prefetch=3,
            grid=(n_tiles,),
            in_specs=[
                pl.BlockSpec((MOE_TG, hw), lambda i, tea, teb, nu: (used(i, nu), 0)),
                wspec((1, d, D_EXPERT), 0), wspec((1, d, D_EXPERT), 0), wspec((1, D_EXPERT, d), 0),
                wspec((1, d, D_EXPERT), 1), wspec((1, d, D_EXPERT), 1), wspec((1, D_EXPERT, d), 1),
            ],
            out_specs=pl.BlockSpec((MOE_TG, d), lambda i, tea, teb, nu: (i, 0)),
        ),
        out_shape=jax.ShapeDtypeStruct((rows_max, d), F32),
        compiler_params=_params(("arbitrary",)),
        name="moe_experts",
    )(tea, teb, nu, xs, wg, wu, wd, wg, wu, wd)


def _combine_kernel(off_ref, meta_ref, ys_ref, x_ref, gt2_ref, gf_ref, o_ref, y_ref, sem, *, final, n_ctx_tiles,
                    tiles_per_batch):
    def body():
        for m in range(TM // LANES):
            def fetch(r8, _):
                for jj in range(SUBLANES):
                    r = r8 * SUBLANES + jj
                    src = off_ref[meta_ref[m, 0, r]] + meta_ref[m, 1, r]
                    _row_copy(ys_ref, src, y_ref, m * LANES + r, sem).start()
                return 0

            lax.fori_loop(0, LANES // SUBLANES, fetch, 0)

        def drain(r, _):
            _row_copy(ys_ref, 0, y_ref, 0, sem).wait()
            return 0

        lax.fori_loop(0, TM, drain, 0, unroll=8)
        x2 = x_ref[...] + gt2_ref[0] * y_ref[...]
        if final:
            x2 = _rms(x2, x2.shape[-1]) * gf_ref[...]
        o_ref[...] = x2

    if final:
        pl.when(pl.program_id(0) % tiles_per_batch >= n_ctx_tiles)(body)
    else:
        body()


def _combine_call(off, meta, ys, x1, mod, gf, tiles_per_batch, n_ctx_tiles, bsz, final):
    t, d = x1.shape
    lat = tiles_per_batch - n_ctx_tiles
    mod_row = lambda i, off: (jnp.where(i % tiles_per_batch < n_ctx_tiles, bsz, i // tiles_per_batch), 0, 5)
    if final:
        out_rows = bsz * lat * TM
        out_ix = lambda i, off: ((i // tiles_per_batch) * lat + jnp.maximum(i % tiles_per_batch - n_ctx_tiles, 0), 0)
    else:
        out_rows = t
        out_ix = lambda i, off: (i, 0)
    return pl.pallas_call(
        functools.partial(_combine_kernel, final=final, n_ctx_tiles=n_ctx_tiles, tiles_per_batch=tiles_per_batch),
        grid_spec=pltpu.PrefetchScalarGridSpec(
            num_scalar_prefetch=1,
            grid=(t // TM,),
            in_specs=[
                pl.BlockSpec((TM // LANES, SUBLANES, LANES), lambda i, off: (i, 0, 0), memory_space=pltpu.SMEM),
                pl.BlockSpec(memory_space=pl.ANY),
                pl.BlockSpec((TM, d), lambda i, off: (i, 0)),
                pl.BlockSpec((1, 1, d), mod_row),
                pl.BlockSpec((1, d), lambda i, off: (0, 0)),
            ],
            out_specs=pl.BlockSpec((TM, d), out_ix),
            scratch_shapes=[pltpu.VMEM((TM, d), F32), pltpu.SemaphoreType.DMA],
        ),
        out_shape=jax.ShapeDtypeStruct((out_rows, d), F32),
        compiler_params=_params(("arbitrary",)),
        name="moe_combine",
    )(off, meta, ys, x1, mod, gf)


def _moe_call(h2, meta, cnt, x1, mod, gf, wg, wu, wd, n_ctx_tiles, final):
    bsz, j, d = x1.shape
    t = bsz * j
    rows_max = -(-t // MOE_TG) * MOE_TG + N_BUCKETS * MOE_TG
    off, tea, teb, nu = _plan_call(cnt, rows_max // MOE_TG)
    xs = _dispatch_call(off, nu, h2.reshape(t, d + LANES), meta, cnt, rows_max)
    ys = _expert_call(tea, teb, nu, xs, wg, wu, wd)
    out = _combine_call(off, meta, ys, x1.reshape(t, d), mod, gf, j // TM, n_ctx_tiles, bsz, final)
    return out.reshape(bsz, -1, d)


def _rope_tables(ctx_len, seq, groups, shift):
    t = jnp.arange(seq, dtype=jnp.int32)
    pos = {False: (t // GRID_W).astype(F32), True: (t % GRID_W).astype(F32)}
    freqs = ROPE_BASE ** (-jnp.arange(shift, dtype=F32) / shift)
    cos = jnp.ones((seq, LANES), F32)
    sina = jnp.zeros((seq, LANES), F32)
    sinb = jnp.zeros((seq, LANES), F32)
    for start, use_cols in groups:
        ang = pos[use_cols][:, None] * freqs[None, :]
        c, s = jnp.cos(ang), jnp.sin(ang)
        cos = cos.at[:, start:start + shift].set(c).at[:, start + shift:start + 2 * shift].set(c)
        sina = sina.at[:, start:start + shift].set(-s)
        sinb = sinb.at[:, start + shift:start + 2 * shift].set(s)
    pad = lambda a, v: jnp.concatenate([jnp.full((ctx_len, LANES), v, F32), a], axis=0)
    return pad(cos, 1.0), pad(sina, 0.0), pad(sinb, 0.0)


def _block_diag(w):
    n, c, d = w.shape
    eye = jnp.eye(n, dtype=w.dtype)
    return (w[:, :, None, :] * eye[:, None, :, None]).reshape(n * c, n * d)


def _prep_layer(l, w_in, g_cq, w_uq, g_ckv, w_ukv, conv_w, conv_b, lru_wa, lru_ba, lru_wx, lru_bx,
                g_grp, w_out, w_g1, b_g1, w_g2, b_g2):
    d = w_in.shape[1]
    wi = w_in[l]
    zeros = lambda n: jnp.zeros((d, n), wi.dtype)
    win = jnp.concatenate([
        wi[:, 0:384], zeros(64), wi[:, 384:416], zeros(32), wi[:, 416:1952]], axis=1).astype(BF16)
    hq = MLA_NOPE + MLA_ROPE
    wuq = jnp.concatenate(
        [jnp.pad(w_uq[l][:, h * hq:(h + 1) * hq], ((0, 0), (0, LANES - hq))) for h in range(MLA_HEADS)],
        axis=1).astype(BF16)
    wkv = w_ukv[l].reshape(MLA_KV_RANK, MLA_HEADS, MLA_NOPE + MLA_V)
    wuk = jnp.pad(wkv[:, :, :MLA_NOPE], ((0, 0), (0, 0), (0, LANES - MLA_NOPE))).reshape(MLA_KV_RANK, -1)
    wuv = wkv[:, :, MLA_NOPE:].reshape(MLA_KV_RANK, -1)
    wukv = jnp.concatenate([wuk, wuv], axis=1).astype(BF16)
    lru = []
    for dr in range(2):
        lru.append(jnp.concatenate([_block_diag(lru_wa[l, dr]), _block_diag(lru_wx[l, dr])], axis=1).astype(BF16))
        lru.append(jnp.concatenate([lru_ba[l, dr], lru_bx[l, dr]])[None, :])
    wr = jnp.concatenate([jnp.moveaxis(w_g2[l], 0, 1).reshape(d, N_EXPERTS), w_g1[l]], axis=1)
    wr = jnp.pad(wr, ((0, 0), (0, LANES - wr.shape[1]))).astype(BF16)
    br = jnp.pad(jnp.concatenate([b_g2[l].reshape(-1), b_g1[l]]), (0, LANES - N_GROUPS - N_EXPERTS))[None, :]
    return dict(win=win, gcq=g_cq[l][None, :], wuq=wuq, gckv=g_ckv[l][None, :], wukv=wukv,
                cw=conv_w[l], cb=conv_b[l][None, :], wf=lru[0], bf=lru[1], wb=lru[2], bb=lru[3],
                gg=g_grp[l][None, :], wout=w_out[l].astype(BF16), wr=wr, br=br)


def kernel(x, c, ctx, c_ctx, w_ada, b_ada, g_norm1, g_norm2, w_in, g_cq, w_uq, g_ckv, w_ukv, swa_sink,
           conv_w, conv_b, lru_wa, lru_ba, lru_wx, lru_bx, lru_lam, g_grp, w_out, w_g1, b_g1, w_g2, b_g2,
           w_e_gate, w_e_up, w_e_down, g_final):
    bsz, seq, d = x.shape
    ctx_len = ctx.shape[1]
    depth = w_ada.shape[0]
    assert seq % TM == 0 and ctx_len % TM == 0 and seq % GRID_W == 0
    n_ctx_tiles = ctx_len // TM

    rows = -(-(bsz + 1) // SUBLANES) * SUBLANES
    cc = jnp.pad(jnp.concatenate([c, c_ctx[None, :]], axis=0), ((0, rows - bsz - 1), (0, 0)))
    mods = _ada_call(cc, w_ada, b_ada)

    q4 = MLA_ROPE // 4
    s4 = SWA_HEAD_DIM // 4
    mla_tabs = _rope_tables(ctx_len, seq, [(MLA_NOPE, False), (MLA_NOPE + 2 * q4, True)], q4)
    swa_tabs = _rope_tables(
        ctx_len, seq,
        [(hh * SWA_HEAD_DIM + ax * 2 * s4, bool(ax)) for hh in range(LANES // SWA_HEAD_DIM) for ax in range(2)], s4)
    tabs = mla_tabs + swa_tabs

    xs = jnp.concatenate([ctx, x], axis=1)
    for l in range(depth):
        p = _prep_layer(l, w_in, g_cq, w_uq, g_ckv, w_ukv, conv_w, conv_b, lru_wa, lru_ba, lru_wx, lru_bx,
                        g_grp, w_out, w_g1, b_g1, w_g2, b_g2)
        mod = mods[l].reshape(rows, 1, 6 * d)
        qm, km, vm, sq, sk, sv, lx, lg = _inproj_call(
            xs, mod, g_norm1[l][None, :], p["win"], p["gcq"], p["wuq"], p["gckv"], p["wukv"], tabs, n_ctx_tiles)
        om = _mla_call(qm, km, vm, n_ctx_tiles, ctx_len)
        osw = _swa_call(swa_sink[l], sq, sk, sv, ctx_len)
        ol = _lru_call(lx, lg, p["cw"], p["cb"], p["wf"], p["bf"], p["wb"], p["bb"], lru_lam[l], n_ctx_tiles)
        x1, h2, meta, cnt = _outproj_call(om, osw, ol, xs, mod, p["gg"], p["wout"], g_norm2[l][None, :],
                                          p["wr"], p["br"], n_ctx_tiles)
        xs = _moe_call(h2, meta, cnt, x1, mod, g_final[None, :], w_e_gate[l].astype(BF16),
                       w_e_up[l].astype(BF16), w_e_down[l].astype(BF16), n_ctx_tiles, l == depth - 1)
    return xs
```

```python
import functools

import jax
import jax.numpy as jnp
from jax import lax
from jax.experimental import pallas as pl
from jax.experimental.pallas import tpu as pltpu

GRID_W = 64
EPS = 1e-6
ROPE_BASE = 10000.0
NEG_INF = -1e30
MLA_HEADS = 4
MLA_NOPE = 64
MLA_ROPE = 32
MLA_V = 64
MLA_Q_RANK = 256
MLA_KV_RANK = 128
MLA_SCALE = (MLA_NOPE + MLA_ROPE) ** -0.5
SWA_HEADS = 4
SWA_KV_HEADS = 2
SWA_HEAD_DIM = 64
SWA_SCALE = SWA_HEAD_DIM ** -0.5
WINDOW = 128
LRU_WIDTH = 512
LRU_BLOCKS = 8
LRU_BW = LRU_WIDTH // LRU_BLOCKS
CONV_W = 4
LRU_C = 8.0
N_GROUPS = 4
EXPERTS_PER_GROUP = 8
N_EXPERTS = N_GROUPS * EXPERTS_PER_GROUP
D_EXPERT = 256
MLA_OUT = MLA_HEADS * MLA_V
SWA_OUT = SWA_HEADS * SWA_HEAD_DIM

LANES = 128
SUBLANES = 8
TM = 256
ZW = 2048
N_PAIRS = EXPERTS_PER_GROUP * (EXPERTS_PER_GROUP - 1) // 2
N_BUCKETS = N_GROUPS * N_PAIRS
MOE_TG = 256
MOE_TD = 512
LOG2E = 1.4426950408889634
VMEM_LIMIT = 56 * 1024 * 1024

F32 = jnp.float32
BF16 = jnp.bfloat16


def _params(sem):
    return pltpu.CompilerParams(dimension_semantics=sem, vmem_limit_bytes=VMEM_LIMIT)


def _full(shape):
    n = len(shape)
    return pl.BlockSpec(shape, lambda *_: (0,) * n)


def _rms(x, width):
    return x * lax.rsqrt(jnp.sum(x * x, axis=-1, keepdims=True) * (1.0 / width) + EPS)


def _dot(a, b):
    return jnp.dot(a, b, preferred_element_type=F32)


def _sigmoid(x):
    return 0.5 * jnp.tanh(0.5 * x) + 0.5


def _pack_pairs(x):
    w = x.shape[-1] // 2
    bits = pltpu.bitcast(x.astype(F32), jnp.uint32)
    return (bits[:, w:] & jnp.uint32(0xFFFF0000)) | (bits[:, :w] >> 16)


def _unpack_pairs(u):
    lo = pltpu.bitcast(u << 16, F32)
    hi = pltpu.bitcast(u & jnp.uint32(0xFFFF0000), F32)
    return jnp.concatenate([lo, hi], axis=-1).astype(BF16)


def _dot_nt(a, b):
    return lax.dot_general(a, b, (((1,), (1,)), ((), ())), preferred_element_type=F32)


def _ada_kernel(c_ref, w_ref, b_ref, o_ref):
    c = c_ref[...]
    s = c * jax.nn.sigmoid(c)
    o_ref[0] = jnp.dot(s, w_ref[0], preferred_element_type=F32,
                       precision=lax.Precision.HIGHEST) + b_ref[0]


def _ada_call(cc, w_ada, b_ada):
    depth, d, n = w_ada.shape
    r = cc.shape[0]
    tn = 1536
    return pl.pallas_call(
        _ada_kernel,
        grid=(depth, n // tn),
        in_specs=[
            pl.BlockSpec((r, d), lambda l, j: (0, 0)),
            pl.BlockSpec((1, d, tn), lambda l, j: (l, 0, j)),
            pl.BlockSpec((1, 1, tn), lambda l, j: (l, 0, j)),
        ],
        out_specs=pl.BlockSpec((1, r, tn), lambda l, j: (l, 0, j)),
        out_shape=jax.ShapeDtypeStruct((depth, r, n), F32),
        compiler_params=_params(("arbitrary", "arbitrary")),
        name="adaln",
    )(cc, w_ada, b_ada.reshape(depth, 1, n))


def _rope(x, cos, sina, sinb, shift):
    n = x.shape[-1]
    reps = n // LANES
    if reps > 1:
        cos = jnp.concatenate([cos] * reps, axis=-1)
        sina = jnp.concatenate([sina] * reps, axis=-1)
        sinb = jnp.concatenate([sinb] * reps, axis=-1)
    return x * cos + pltpu.roll(x, n - shift, 1) * sina + pltpu.roll(x, shift, 1) * sinb


def _inproj_kernel(x_ref, sh_ref, sc_ref, g1_ref, win_ref, gcq_ref, wuq_ref, gckv_ref, wukv_ref,
                   mcos_ref, msa_ref, msb_ref, scos_ref, ssa_ref, ssb_ref,
                   qm_ref, km_ref, vm_ref, sq_ref, sk_ref, sv_ref, lx_ref, lg_ref):
    d = x_ref.shape[-1]
    x = x_ref[0]
    h = _rms(x, d) * g1_ref[...] * (1.0 + sc_ref[0]) + sh_ref[0]
    z = _dot(h.astype(BF16), win_ref[...])

    mcos, msa, msb = mcos_ref[...], msa_ref[...], msb_ref[...]
    scos, ssa, ssb = scos_ref[...], ssa_ref[...], ssb_ref[...]

    cq = _rms(z[:, 0:256], MLA_Q_RANK) * gcq_ref[...]
    q = _dot(cq.astype(BF16), wuq_ref[...])
    q = _rope(q, mcos, msa, msb, MLA_ROPE // 4)
    qm_ref[0] = (q * (MLA_SCALE * LOG2E)).astype(BF16)

    ckv = _rms(z[:, 256:384], MLA_KV_RANK) * gckv_ref[...]
    kv = _dot(ckv.astype(BF16), wukv_ref[...])
    kr = _rope(z[:, 384:512], mcos, msa, msb, MLA_ROPE // 4)
    km_ref[0] = (kv[:, 0:512] + jnp.concatenate([kr] * MLA_HEADS, axis=-1)).astype(BF16)
    ones = jnp.ones((kv.shape[0], LANES), F32)
    vm_ref[0] = jnp.concatenate([kv[:, 512:640], ones, kv[:, 640:768], ones], axis=-1).astype(BF16)

    sq_ref[0] = (_rope(z[:, 512:768], scos, ssa, ssb, SWA_HEAD_DIM // 4) * (SWA_SCALE * LOG2E)).astype(BF16)
    sk = _rope(z[:, 768:896], scos, ssa, ssb, SWA_HEAD_DIM // 4)
    sv = z[:, 896:1024]
    lane = lax.broadcasted_iota(jnp.int32, sk.shape, 1)
    low = lane < SWA_HEAD_DIM
    k0 = jnp.where(low, sk, 0.0)
    k1 = jnp.where(low, 0.0, sk)
    sk_ref[0] = jnp.concatenate([k0 + pltpu.roll(k0, SWA_HEAD_DIM, 1), k1 + pltpu.roll(k1, SWA_HEAD_DIM, 1)],
                                axis=-1).astype(BF16)
    v0 = jnp.where(low, sv, 0.0)
    v1 = jnp.where(low, 0.0, sv)
    one_hi = jnp.where(low, 0.0, 1.0)
    one_lo = jnp.where(low, 1.0, 0.0)
    sv_ref[0] = jnp.concatenate([v0 + one_hi, pltpu.roll(v0, SWA_HEAD_DIM, 1) + one_lo,
                                 pltpu.roll(v1, SWA_HEAD_DIM, 1) + one_hi, v1 + one_lo], axis=-1).astype(BF16)
    lx_ref[0] = z[:, 1024:1536].astype(BF16)
    lg_ref[0] = z[:, 1536:2048].astype(BF16)


def _mod_spec(j, b_rows, n_ctx_tiles, d, tile_major):
    if tile_major:
        return pl.BlockSpec((1, 1, d), lambda i, b: (jnp.where(i < n_ctx_tiles, b_rows, b), 0, j))
    return pl.BlockSpec((1, 1, d), lambda b, i: (jnp.where(i < n_ctx_tiles, b_rows, b), 0, j))


def _inproj_call(xs, mod, g1, win, gcq, wuq, gckv, wukv, tabs, n_ctx_tiles):
    bsz, j, d = xs.shape
    nt = j // TM
    tok = lambda w: pl.BlockSpec((1, TM, w), lambda i, b: (b, i, 0))
    tab = pl.BlockSpec((TM, LANES), lambda i, b: (i, 0))
    widths = (512, 512, 512, 256, 256, 512, 512, 512)
    return pl.pallas_call(
        _inproj_kernel,
        grid=(nt, bsz),
        in_specs=[
            tok(d),
            _mod_spec(0, bsz, n_ctx_tiles, d, True),
            _mod_spec(1, bsz, n_ctx_tiles, d, True),
            _full(g1.shape), _full(win.shape), _full(gcq.shape), _full(wuq.shape),
            _full(gckv.shape), _full(wukv.shape),
            tab, tab, tab, tab, tab, tab,
        ],
        out_specs=[tok(w) for w in widths],
        out_shape=[jax.ShapeDtypeStruct((bsz, j, w), BF16) for w in widths],
        compiler_params=_params(("arbitrary", "arbitrary")),
        name="inproj",
    )(xs, mod, mod, g1, win, gcq, wuq, gckv, wukv, *tabs)


def _mla_heads(q_ref, k_ref, v_ref, o_ref, klen):
    lane = lax.broadcasted_iota(jnp.int32, (q_ref.shape[1], LANES), 1)
    for hp in range(MLA_HEADS // 2):
        v = v_ref[0, 0:klen, 2 * hp * LANES:2 * (hp + 1) * LANES]
        outs = []
        for h in range(2 * hp, 2 * hp + 2):
            q = q_ref[0, :, h * LANES:(h + 1) * LANES]
            k = k_ref[0, 0:klen, h * LANES:(h + 1) * LANES]
            s = _dot_nt(q, k)
            m = jnp.max(s, axis=-1, keepdims=True)
            o = _dot(jnp.exp2(s - m).astype(BF16), v)
            outs.append(o[:, 0:LANES] / o[:, LANES:LANES + 1])
        o_ref[0, :, hp * LANES:(hp + 1) * LANES] = jnp.where(lane < MLA_V, outs[0], outs[1]).astype(o_ref.dtype)


def _mla_kernel(q_ref, k_ref, v_ref, o_ref, *, n_ctx_tiles, ctx_len):
    i = pl.program_id(1)

    @pl.when(i < n_ctx_tiles)
    def _():
        _mla_heads(q_ref, k_ref, v_ref, o_ref, ctx_len)

    @pl.when(i >= n_ctx_tiles)
    def _():
        _mla_heads(q_ref, k_ref, v_ref, o_ref, k_ref.shape[1])


def _mla_call(qm, km, vm, n_ctx_tiles, ctx_len):
    bsz, j, w = qm.shape
    nt = j // TM
    return pl.pallas_call(
        functools.partial(_mla_kernel, n_ctx_tiles=n_ctx_tiles, ctx_len=ctx_len),
        grid=(bsz, nt),
        in_specs=[
            pl.BlockSpec((1, TM, w), lambda b, i: (b, i, 0)),
            pl.BlockSpec((1, j, w), lambda b, i: (b, 0, 0)),
            pl.BlockSpec((1, j, w), lambda b, i: (b, 0, 0)),
        ],
        out_specs=pl.BlockSpec((1, TM, MLA_OUT), lambda b, i: (b, i, 0)),
        out_shape=jax.ShapeDtypeStruct((bsz, j, MLA_OUT), BF16),
        compiler_params=_params(("arbitrary", "arbitrary")),
        name="mla_attn",
    )(qm, km, vm)


def _swa_kernel(sink_ref, q_ref, k_ref, v_ref, o_ref, *, ctx_len):
    i = pl.program_id(1)
    j = k_ref.shape[1]
    span = TM + 2 * WINDOW
    row0 = pl.multiple_of(jnp.clip(i * TM - WINDOW, 0, j - span), WINDOW)
    qpos = i * TM - ctx_len + lax.broadcasted_iota(jnp.int32, (TM, span), 0)
    kpos = row0 - ctx_len + lax.broadcasted_iota(jnp.int32, (TM, span), 1)
    valid = (jnp.abs(qpos - kpos) <= WINDOW) & (kpos >= 0) & (qpos >= 0)
    low = lax.broadcasted_iota(jnp.int32, (TM, LANES), 1) < SWA_HEAD_DIM
    group = SWA_HEADS // SWA_KV_HEADS
    for kvh in range(SWA_KV_HEADS):
        kcols = slice(kvh * LANES, (kvh + 1) * LANES)
        qpair = q_ref[0, :, kcols]
        kl = k_ref[0, pl.ds(row0, span), kcols]
        kc = k_ref[0, 0:ctx_len, kcols]
        halves = []
        for g in range(group):
            vcols = slice((kvh * group + g) * LANES, (kvh * group + g + 1) * LANES)
            q = jnp.where(low if g == 0 else ~low, qpair, jnp.zeros_like(qpair))
            s_loc = jnp.where(valid, _dot_nt(q, kl), NEG_INF)
            s_ctx = _dot_nt(q, kc)
            sink = sink_ref[kvh * group + g] * LOG2E
            m = jnp.maximum(jnp.maximum(jnp.max(s_loc, axis=-1, keepdims=True),
                                        jnp.max(s_ctx, axis=-1, keepdims=True)), sink)
            o = (_dot(jnp.exp2(s_loc - m).astype(BF16), v_ref[0, pl.ds(row0, span), vcols])
                 + _dot(jnp.exp2(s_ctx - m).astype(BF16), v_ref[0, 0:ctx_len, vcols]))
            den = o[:, SWA_HEAD_DIM:SWA_HEAD_DIM + 1] if g == 0 else o[:, 0:1]
            halves.append(o / (den + jnp.exp2(sink - m)))
        o_ref[0, :, kcols] = jnp.where(low, halves[0], halves[1]).astype(o_ref.dtype)


def _swa_call(sink, sq, sk, sv, ctx_len):
    bsz, j, _ = sq.shape
    nt = j // TM
    return pl.pallas_call(
        functools.partial(_swa_kernel, ctx_len=ctx_len),
        grid=(bsz, nt),
        in_specs=[
            pl.BlockSpec(memory_space=pltpu.SMEM),
            pl.BlockSpec((1, TM, SWA_OUT), lambda b, i: (b, i, 0)),
            pl.BlockSpec((1, j, sk.shape[-1]), lambda b, i: (b, 0, 0)),
            pl.BlockSpec((1, j, sv.shape[-1]), lambda b, i: (b, 0, 0)),
        ],
        out_specs=pl.BlockSpec((1, TM, SWA_OUT), lambda b, i: (b, i, 0)),
        out_shape=jax.ShapeDtypeStruct((bsz, j, SWA_OUT), BF16),
        compiler_params=_params(("arbitrary", "arbitrary")),
        name="swa_attn",
    )(sink, sq, sk, sv)


def _lru_gates(prev_ref, cur_ref, next_ref, cw_ref, cb_ref, w_ref, b_ref, sp_ref, seg_start, seg_end):
    z = cur_ref[0].astype(F32)
    row = lax.broadcasted_iota(jnp.int32, z.shape, 0)
    keep_prev = jnp.where(seg_start, 0.0, 1.0)
    keep_next = jnp.where(seg_end, 0.0, 1.0)
    p2 = prev_ref[0, TM - 2:TM - 1, :].astype(F32) * keep_prev
    p1 = prev_ref[0, TM - 1:TM, :].astype(F32) * keep_prev
    n0 = next_ref[0, 0:1, :].astype(F32) * keep_next
    z_m1 = jnp.where(row == 0, p1, pltpu.roll(z, 1, 0))
    z_m2 = jnp.where(row == 0, p2, jnp.where(row == 1, p1, pltpu.roll(z, 2, 0)))
    z_p1 = jnp.where(row == TM - 1, n0, pltpu.roll(z, TM - 1, 0))
    u = cb_ref[...] + z_m2 * cw_ref[0:1, :]
    u = u + z_m1 * cw_ref[1:2, :]
    u = u + z * cw_ref[2:3, :]
    u = u + z_p1 * cw_ref[3:4, :]
    g = _dot(u.astype(BF16), w_ref[...]) + b_ref[...]
    r = _sigmoid(g[:, 0:LRU_WIDTH])
    ig = _sigmoid(g[:, LRU_WIDTH:2 * LRU_WIDTH])
    log_a = (-LRU_C) * r * sp_ref[...]
    a = jnp.exp(log_a)
    om = 1.0 - a * a
    bt = jnp.where(om > 0.0, om * lax.rsqrt(om), 0.0) * (ig * u)
    return a, bt


def _lru_scan(a, b, h0, reverse):
    t, w = a.shape
    g = t // SUBLANES
    a = a.reshape(g, SUBLANES, w)
    b = b.reshape(g, SUBLANES, w)
    sub = lax.broadcasted_iota(jnp.int32, (g, SUBLANES, w), 1)
    d = 1
    while d < SUBLANES:
        if reverse:
            shift, ok = SUBLANES - d, sub < SUBLANES - d
        else:
            shift, ok = d, sub >= d
        a_sh = pltpu.roll(a, shift, 1)
        b_sh = pltpu.roll(b, shift, 1)
        b = jnp.where(ok, a * b_sh + b, b)
        a = jnp.where(ok, a * a_sh, a)
        d *= 2
    hs = [None] * g
    h = h0
    order = range(g - 1, -1, -1) if reverse else range(g)
    for gi in order:
        hg = a[gi] * h + b[gi]
        hs[gi] = hg
        h = hg[0:1, :] if reverse else hg[SUBLANES - 1:SUBLANES, :]
    return jnp.concatenate(hs, axis=0), h


def _softplus_neg(lam):
    x = -lam
    return jnp.maximum(x, 0.0) + jnp.log1p(jnp.exp(-jnp.abs(x)))


def _lru_fwd_kernel(prev_ref, cur_ref, next_ref, cw_ref, cb_ref, w_ref, b_ref, lam_ref,
                    hf_ref, carry_ref, sp_ref, *, n_ctx_tiles):
    c = pl.program_id(1)
    nt = pl.num_programs(1)

    @pl.when(c == 0)
    def _():
        carry_ref[...] = jnp.zeros_like(carry_ref)
        sp_ref[...] = _softplus_neg(lam_ref[...])

    seg_start = (c == 0) | (c == n_ctx_tiles)
    seg_end = (c == n_ctx_tiles - 1) | (c == nt - 1)
    a, bt = _lru_gates(prev_ref, cur_ref, next_ref, cw_ref, cb_ref, w_ref, b_ref, sp_ref,
                       seg_start, seg_end)
    hs, h = _lru_scan(a, bt, carry_ref[...], False)
    carry_ref[...] = h
    hf_ref[0] = hs.astype(hf_ref.dtype)


def _bwd_chunk(s, nt, n_ctx_tiles):
    return jnp.where(s < n_ctx_tiles, n_ctx_tiles - 1 - s, nt - 1 - (s - n_ctx_tiles))


def _lru_bwd_kernel(prev_ref, cur_ref, next_ref, cw_ref, cb_ref, w_ref, b_ref, lam_ref,
                    hf_ref, lg_ref, o_ref, carry_ref, sp_ref, *, n_ctx_tiles):
    s = pl.program_id(1)
    nt = pl.num_programs(1)
    c = _bwd_chunk(s, nt, n_ctx_tiles)

    @pl.when(s == 0)
    def _():
        carry_ref[...] = jnp.zeros_like(carry_ref)
        sp_ref[...] = _softplus_neg(lam_ref[...])

    seg_start = (c == 0) | (c == n_ctx_tiles)
    seg_end = (c == n_ctx_tiles - 1) | (c == nt - 1)
    a, bt = _lru_gates(prev_ref, cur_ref, next_ref, cw_ref, cb_ref, w_ref, b_ref, sp_ref,
                       seg_start, seg_end)
    hs, h = _lru_scan(a, bt, carry_ref[...], True)
    carry_ref[...] = h
    gate = jax.nn.gelu(lg_ref[0].astype(F32), approximate=True)
    o_ref[0] = ((hf_ref[0].astype(F32) + hs) * gate).astype(o_ref.dtype)


def _lru_call(lx, lg, cw, cb, wf, bf, wb, bb, lam, n_ctx_tiles):
    bsz, j, w = lx.shape
    nt = j // TM
    consts = [_full(cw.shape), _full((1, w)), _full(wf.shape), _full(bf.shape), _full((1, w))]
    scratch = [pltpu.VMEM((1, w), F32), pltpu.VMEM((1, w), F32)]
    blk = lambda f: pl.BlockSpec((1, TM, w), f)
    hf = pl.pallas_call(
        functools.partial(_lru_fwd_kernel, n_ctx_tiles=n_ctx_tiles),
        grid=(bsz, nt),
        in_specs=[
            blk(lambda b, c: (b, jnp.maximum(c - 1, 0), 0)),
            blk(lambda b, c: (b, c, 0)),
            blk(lambda b, c: (b, jnp.minimum(c + 1, nt - 1), 0)),
        ] + consts,
        out_specs=blk(lambda b, c: (b, c, 0)),
        out_shape=jax.ShapeDtypeStruct((bsz, j, w), BF16),
        scratch_shapes=scratch,
        compiler_params=_params(("arbitrary", "arbitrary")),
        name="lru_fwd",
    )(lx, lx, lx, cw, cb, wf, bf, lam[0:1])
    cix = lambda s: _bwd_chunk(s, nt, n_ctx_tiles)
    return pl.pallas_call(
        functools.partial(_lru_bwd_kernel, n_ctx_tiles=n_ctx_tiles),
        grid=(bsz, nt),
        in_specs=[
            blk(lambda b, s: (b, jnp.maximum(cix(s) - 1, 0), 0)),
            blk(lambda b, s: (b, cix(s), 0)),
            blk(lambda b, s: (b, jnp.minimum(cix(s) + 1, nt - 1), 0)),
        ] + consts + [
            blk(lambda b, s: (b, cix(s), 0)),
            blk(lambda b, s: (b, cix(s), 0)),
        ],
        out_specs=blk(lambda b, s: (b, cix(s), 0)),
        out_shape=jax.ShapeDtypeStruct((bsz, j, w), BF16),
        scratch_shapes=scratch,
        compiler_params=_params(("arbitrary", "arbitrary")),
        name="lru_bwd",
    )(lx, lx, lx, cw, cb, wb, bb, lam[1:2], hf, lg)


def _route(logits, carry):
    lt = logits.T
    tm = lt.shape[1]
    big = jnp.int32(LANES)
    lg = lt[N_EXPERTS:N_EXPERTS + SUBLANES]
    rg = lax.broadcasted_iota(jnp.int32, lg.shape, 0)
    is_g = rg < N_GROUPS
    mg = jnp.max(jnp.where(is_g, lg, -jnp.inf), axis=0, keepdims=True)
    eg = jnp.where(is_g, jnp.exp(lg - mg), 0.0)
    pg = eg / jnp.sum(eg, axis=0, keepdims=True)
    pg_top = jnp.max(pg, axis=0, keepdims=True)
    g_idx = jnp.min(jnp.where(is_g & (pg == pg_top), rg, big), axis=0, keepdims=True)
    le = lt[0:N_EXPERTS]
    re = lax.broadcasted_iota(jnp.int32, le.shape, 0)
    lo = g_idx * EXPERTS_PER_GROUP
    sel = (re >= lo) & (re < lo + EXPERTS_PER_GROUP)
    me = jnp.max(jnp.where(sel, le, -jnp.inf), axis=0, keepdims=True)
    ee = jnp.where(sel, jnp.exp(le - me), 0.0)
    pe = ee / jnp.sum(ee, axis=0, keepdims=True)
    p1 = jnp.max(jnp.where(sel, pe, -1.0), axis=0, keepdims=True)
    i1 = jnp.min(jnp.where(sel & (pe == p1), re, big), axis=0, keepdims=True)
    sel2 = sel & (re != i1)
    p2 = jnp.max(jnp.where(sel2, pe, -1.0), axis=0, keepdims=True)
    i2 = jnp.min(jnp.where(sel2 & (pe == p2), re, big), axis=0, keepdims=True)
    den = p1 + p2
    w1 = pg_top * p1 / den
    w2 = pg_top * p2 / den
    first_lo = i1 < i2
    ia = (jnp.where(first_lo, i1, i2) - lo).astype(F32)
    ib = (jnp.where(first_lo, i2, i1) - lo).astype(F32)
    wa = jnp.where(first_lo, w1, w2)
    wb = jnp.where(first_lo, w2, w1)
    pair = ia * (2 * EXPERTS_PER_GROUP - 1 - ia) * 0.5 + (ib - ia - 1.0)
    bucket = (g_idx.astype(F32) * N_PAIRS + pair).astype(jnp.int32)
    rb = lax.broadcasted_iota(jnp.int32, (LANES, tm), 0)
    mine = rb == bucket
    picks = jnp.where(mine, 1.0, 0.0)
    tri = jnp.where(lax.broadcasted_iota(jnp.int32, (tm, tm), 0) < lax.broadcasted_iota(jnp.int32, (tm, tm), 1),
                    1.0, 0.0).astype(BF16)
    before = _dot(picks.astype(BF16), tri) + carry
    rank = jnp.sum(jnp.where(mine, before, 0.0), axis=0, keepdims=True)
    zeros = jnp.zeros((SUBLANES - 2, tm), F32)
    meta = jnp.concatenate([bucket.astype(F32), rank, zeros], axis=0).astype(jnp.int32)
    wrows = jnp.concatenate([wa, wb, jnp.zeros((LANES - 2, tm), F32)], axis=0)
    return meta, wrows.T, carry + jnp.sum(picks, axis=1, keepdims=True)


def _outproj_kernel(om_ref, os_ref, ol_ref, x_ref, gt1_ref, sh2_ref, sc2_ref, gg_ref, wout_ref,
                    g2_ref, wr_ref, br_ref, x1_ref, h2_ref, meta_ref, cnt_ref, carry_ref):
    d = x_ref.shape[-1]

    @pl.when((pl.program_id(0) == 0) & (pl.program_id(1) == 0))
    def _():
        carry_ref[...] = jnp.zeros_like(carry_ref)

    gg = gg_ref[...]
    nm = _rms(om_ref[0].astype(F32), MLA_OUT) * gg[:, 0:MLA_OUT]
    ns = _rms(os_ref[0].astype(F32), SWA_OUT) * gg[:, MLA_OUT:MLA_OUT + SWA_OUT]
    nl = _rms(ol_ref[0].astype(F32), LRU_WIDTH) * gg[:, MLA_OUT + SWA_OUT:]
    merged = jnp.concatenate([nm, ns, nl], axis=-1).astype(BF16)
    mix = _dot(merged, wout_ref[...])
    x1 = x_ref[0] + gt1_ref[0] * mix
    x1_ref[0] = x1
    h2 = (_rms(x1, d) * g2_ref[...] * (1.0 + sc2_ref[0]) + sh2_ref[0]).astype(BF16)
    meta, wcols, carry = _route(_dot(h2, wr_ref[...]) + br_ref[...], carry_ref[...])
    h2_ref[0] = jnp.concatenate([_pack_pairs(h2), pltpu.bitcast(wcols, jnp.uint32)], axis=-1)
    carry_ref[...] = carry
    cnt_ref[...] = jnp.broadcast_to(carry, (LANES, LANES)).T[0:1, :].astype(jnp.int32)
    for m in range(TM // LANES):
        meta_ref[m] = meta[:, m * LANES:(m + 1) * LANES]


def _outproj_call(om, osw, ol, xs, mod, gg, wout, g2, wr, br, n_ctx_tiles):
    bsz, j, d = xs.shape
    nt = j // TM
    tok = lambda w: pl.BlockSpec((1, TM, w), lambda i, b: (b, i, 0))
    return pl.pallas_call(
        _outproj_kernel,
        grid=(nt, bsz),
        in_specs=[
            tok(MLA_OUT), tok(SWA_OUT), tok(LRU_WIDTH), tok(d),
            _mod_spec(2, bsz, n_ctx_tiles, d, True),
            _mod_spec(3, bsz, n_ctx_tiles, d, True),
            _mod_spec(4, bsz, n_ctx_tiles, d, True),
            _full(gg.shape), _full(wout.shape), _full(g2.shape), _full(wr.shape), _full(br.shape),
        ],
        out_specs=[
            tok(d), tok(d // 2 + LANES),
            pl.BlockSpec((TM // LANES, SUBLANES, LANES), lambda i, b: (b * nt + i, 0, 0)),
            pl.BlockSpec((1, LANES), lambda i, b: (0, 0)),
        ],
        out_shape=[
            jax.ShapeDtypeStruct((bsz, j, d), F32),
            jax.ShapeDtypeStruct((bsz, j, d // 2 + LANES), jnp.uint32),
            jax.ShapeDtypeStruct((bsz * j // LANES, SUBLANES, LANES), jnp.int32),
            jax.ShapeDtypeStruct((1, LANES), jnp.int32),
        ],
        scratch_shapes=[pltpu.VMEM((LANES, 1), F32)],
        compiler_params=_params(("arbitrary", "arbitrary")),
        name="outproj_route",
    )(om, osw, ol, xs, mod, mod, mod, gg, wout, g2, wr, br)


def _padded_tiles(count):
    return lax.shift_right_logical(count + (MOE_TG - 1), MOE_TG.bit_length() - 1)


def _plan_kernel(cnt_ref, off_ref, tea_ref, teb_ref, nu_ref, *, n_tiles_max):
    off = jnp.int32(0)
    ti = jnp.int32(0)
    bucket = 0
    for g in range(N_GROUPS):
        for a in range(EXPERTS_PER_GROUP):
            for b in range(a + 1, EXPERTS_PER_GROUP):
                n_q = _padded_tiles(cnt_ref[0, bucket])
                off_ref[bucket] = off

                def fill(k, _, base=ti, ea=g * EXPERTS_PER_GROUP + a, eb=g * EXPERTS_PER_GROUP + b):
                    tea_ref[base + k] = ea
                    teb_ref[base + k] = eb
                    return 0

                lax.fori_loop(0, n_q, fill, 0)
                off = off + n_q * MOE_TG
                ti = ti + n_q
                bucket += 1
    nu_ref[0] = ti

    def rest(k, _):
        tea_ref[k] = N_EXPERTS - 2
        teb_ref[k] = N_EXPERTS - 1
        return 0

    lax.fori_loop(ti, n_tiles_max, rest, 0)


def _plan_call(cnt, n_tiles_max):
    smem = pl.BlockSpec(memory_space=pltpu.SMEM)
    return pl.pallas_call(
        functools.partial(_plan_kernel, n_tiles_max=n_tiles_max),
        in_specs=[smem],
        out_specs=[smem, smem, smem, smem],
        out_shape=[
            jax.ShapeDtypeStruct((N_BUCKETS,), jnp.int32),
            jax.ShapeDtypeStruct((n_tiles_max,), jnp.int32),
            jax.ShapeDtypeStruct((n_tiles_max,), jnp.int32),
            jax.ShapeDtypeStruct((1,), jnp.int32),
        ],
        name="moe_plan",
    )(cnt)


def _row_copy(src_ref, src_row, dst_ref, dst_row, sem):
    return pltpu.make_async_copy(src_ref.at[pl.ds(src_row, 1)], dst_ref.at[pl.ds(dst_row, 1)], sem)


def _dispatch_kernel(off_ref, nu_ref, h_ref, meta_ref, cnt_ref, xs_ref, zero_ref, sem, tile_sem, *, tile):
    n_tiles_max = xs_ref.shape[0] // MOE_TG

    @pl.when(pl.program_id(0) == 0)
    def _():
        zero_ref[...] = jnp.zeros_like(zero_ref)

        def per_bucket(q, n):
            c = cnt_ref[0, q]
            padded = _padded_tiles(c) * MOE_TG
            base = off_ref[q]

            def fill(r, _):
                _row_copy(zero_ref, 0, xs_ref, base + r, sem).start()
                return 0

            lax.fori_loop(c, padded, fill, 0)
            return n + (padded - c)

        n_pad = lax.fori_loop(0, N_BUCKETS, per_bucket, jnp.int32(0))

        def drain(r, _):
            _row_copy(zero_ref, 0, xs_ref, 0, sem).wait()
            return 0

        lax.fori_loop(0, n_pad, drain, 0)

        def unused_tile(ti, _):
            cp = pltpu.make_async_copy(zero_ref, xs_ref.at[pl.ds(ti * MOE_TG, MOE_TG)], tile_sem)
            cp.start()
            cp.wait()
            return 0

        lax.fori_loop(nu_ref[0], n_tiles_max, unused_tile, 0)

    for m in range(tile // LANES):
        def send(r8, _):
            for jj in range(SUBLANES):
                r = r8 * SUBLANES + jj
                dst = off_ref[meta_ref[m, 0, r]] + meta_ref[m, 1, r]
                _row_copy(h_ref, m * LANES + r, xs_ref, dst, sem).start(priority=jj % 2)
            return 0

        lax.fori_loop(0, LANES // SUBLANES, send, 0)

    def drain_rows(r, _):
        _row_copy(h_ref, 0, xs_ref, 0, sem).wait()
        return 0

    lax.fori_loop(0, tile, drain_rows, 0, unroll=8)


def _dispatch_call(off, nu, h2, meta, cnt, rows_max):
    t, hw = h2.shape
    tile = MOE_TD if t % MOE_TD == 0 else TM
    return pl.pallas_call(
        functools.partial(_dispatch_kernel, tile=tile),
        grid_spec=pltpu.PrefetchScalarGridSpec(
            num_scalar_prefetch=2,
            grid=(t // tile,),
            in_specs=[
                pl.BlockSpec((tile, hw), lambda i, off, nu: (i, 0)),
                pl.BlockSpec((tile // LANES, SUBLANES, LANES), lambda i, off, nu: (i, 0, 0),
                             memory_space=pltpu.SMEM),
                pl.BlockSpec(memory_space=pltpu.SMEM),
            ],
            out_specs=pl.BlockSpec(memory_space=pl.ANY),
            scratch_shapes=[pltpu.VMEM((MOE_TG, hw), h2.dtype), pltpu.SemaphoreType.DMA, pltpu.SemaphoreType.DMA],
        ),
        out_shape=jax.ShapeDtypeStruct((rows_max, hw), h2.dtype),
        compiler_params=_params(("arbitrary",)),
        name="moe_dispatch",
    )(off, nu, h2, meta, cnt)


def _expert_mlp(x, wg_ref, wu_ref, wd_ref):
    gate = _dot(x, wg_ref[0])
    act = gate * _sigmoid(gate) * _dot(x, wu_ref[0])
    return _dot(act.astype(BF16), wd_ref[0])


def _expert_kernel(tea_ref, teb_ref, nu_ref, x_ref, wga_ref, wua_ref, wda_ref, wgb_ref, wub_ref, wdb_ref, y_ref):
    hw = y_ref.shape[-1]

    @pl.when(pl.program_id(0) < nu_ref[0])
    def _():
        x = _unpack_pairs(x_ref[:, 0:hw])
        wts = pltpu.bitcast(x_ref[:, hw:hw + LANES], F32)
        y = (wts[:, 0:1] * _expert_mlp(x, wga_ref, wua_ref, wda_ref)
             + wts[:, 1:2] * _expert_mlp(x, wgb_ref, wub_ref, wdb_ref))
        y_ref[...] = _pack_pairs(y.astype(BF16))

    @pl.when(pl.program_id(0) >= nu_ref[0])
    def _():
        y_ref[...] = jnp.zeros_like(y_ref)


def _expert_call(tea, teb, nu, xs, wg, wu, wd):
    rows_max, hw = xs.shape
    d = 2 * (hw - LANES)
    n_tiles = rows_max // MOE_TG
    used = lambda i, nu: jnp.minimum(i, nu[0] - 1)
    wspec = lambda shape, te_pos: pl.BlockSpec(
        shape, lambda i, tea, teb, nu: ((tea, teb)[te_pos][used(i, nu)], 0, 0))
    return pl.pallas_call(
        _expert_kernel,
        grid_spec=pltpu.PrefetchScalarGridSpec(
            num_scalar_prefetch=3,
            grid=(n_tiles,),
            in_specs=[
                pl.BlockSpec((MOE_TG, hw), lambda i, tea, teb, nu: (used(i, nu), 0)),
                wspec((1, d, D_EXPERT), 0), wspec((1, d, D_EXPERT), 0), wspec((1, D_EXPERT, d), 0),
                wspec((1, d, D_EXPERT), 1), wspec((1, d, D_EXPERT), 1), wspec((1, D_EXPERT, d), 1),
            ],
            out_specs=pl.BlockSpec((MOE_TG, d // 2), lambda i, tea, teb, nu: (i, 0)),
        ),
        out_shape=jax.ShapeDtypeStruct((rows_max, d // 2), jnp.uint32),
        compiler_params=_params(("arbitrary",)),
        name="moe_experts",
    )(tea, teb, nu, xs, wg, wu, wd, wg, wu, wd)


def _combine_kernel(off_ref, meta_ref, ys_ref, x_ref, gt2_ref, gf_ref, o_ref, y_ref, sem, *, final, n_ctx_tiles,
                    tiles_per_batch):
    def body():
        for m in range(TM // LANES):
            def fetch(r8, _):
                for jj in range(SUBLANES):
                    r = r8 * SUBLANES + jj
                    src = off_ref[meta_ref[m, 0, r]] + meta_ref[m, 1, r]
                    _row_copy(ys_ref, src, y_ref, m * LANES + r, sem).start(priority=jj % 2)
                return 0

            lax.fori_loop(0, LANES // SUBLANES, fetch, 0)

        def drain(r, _):
            _row_copy(ys_ref, 0, y_ref, 0, sem).wait()
            return 0

        lax.fori_loop(0, TM, drain, 0, unroll=8)
        x2 = x_ref[...] + gt2_ref[0] * _unpack_pairs(y_ref[...]).astype(F32)
        if final:
            x2 = _rms(x2, x2.shape[-1]) * gf_ref[...]
        o_ref[...] = x2

    if final:
        pl.when(pl.program_id(0) % tiles_per_batch >= n_ctx_tiles)(body)
    else:
        body()


def _combine_call(off, meta, ys, x1, mod, gf, tiles_per_batch, n_ctx_tiles, bsz, final):
    t, d = x1.shape
    lat = tiles_per_batch - n_ctx_tiles
    mod_row = lambda i, off: (jnp.where(i % tiles_per_batch < n_ctx_tiles, bsz, i // tiles_per_batch), 0, 5)
    if final:
        out_rows = bsz * lat * TM
        out_ix = lambda i, off: ((i // tiles_per_batch) * lat + jnp.maximum(i % tiles_per_batch - n_ctx_tiles, 0), 0)
    else:
        out_rows = t
        out_ix = lambda i, off: (i, 0)
    return pl.pallas_call(
        functools.partial(_combine_kernel, final=final, n_ctx_tiles=n_ctx_tiles, tiles_per_batch=tiles_per_batch),
        grid_spec=pltpu.PrefetchScalarGridSpec(
            num_scalar_prefetch=1,
            grid=(t // TM,),
            in_specs=[
                pl.BlockSpec((TM // LANES, SUBLANES, LANES), lambda i, off: (i, 0, 0), memory_space=pltpu.SMEM),
                pl.BlockSpec(memory_space=pl.ANY),
                pl.BlockSpec((TM, d), lambda i, off: (i, 0)),
                pl.BlockSpec((1, 1, d), mod_row),
                pl.BlockSpec((1, d), lambda i, off: (0, 0)),
            ],
            out_specs=pl.BlockSpec((TM, d), out_ix),
            scratch_shapes=[pltpu.VMEM((TM, d // 2), jnp.uint32), pltpu.SemaphoreType.DMA],
        ),
        out_shape=jax.ShapeDtypeStruct((out_rows, d), F32),
        compiler_params=_params(("arbitrary",)),
        name="moe_combine",
    )(off, meta, ys, x1, mod, gf)


def _moe_call(h2, meta, cnt, x1, mod, gf, wg, wu, wd, n_ctx_tiles, final):
    bsz, j, d = x1.shape
    t = bsz * j
    rows_max = -(-t // MOE_TG) * MOE_TG + N_BUCKETS * MOE_TG
    off, tea, teb, nu = _plan_call(cnt, rows_max // MOE_TG)
    xs = _dispatch_call(off, nu, h2.reshape(t, d // 2 + LANES), meta, cnt, rows_max)
    ys = _expert_call(tea, teb, nu, xs, wg, wu, wd)
    out = _combine_call(off, meta, ys, x1.reshape(t, d), mod, gf, j // TM, n_ctx_tiles, bsz, final)
    return out.reshape(bsz, -1, d)


def _rope_tables(ctx_len, seq, groups, shift):
    fidx = [0] * LANES
    cols = [False] * LANES
    role = [0] * LANES
    for start, use_cols in groups:
        for k in range(shift):
            for half in range(2):
                lane = start + half * shift + k
                fidx[lane], cols[lane], role[lane] = k, use_cols, half + 1
    fidx = jnp.asarray(fidx, F32)[None, :]
    cols = jnp.asarray(cols)[None, :]
    role = jnp.asarray(role, jnp.int32)[None, :]
    t = jnp.arange(ctx_len + seq, dtype=jnp.int32)[:, None] - ctx_len
    pos = jnp.where(cols, t % GRID_W, t // GRID_W).astype(F32)
    ang = pos * (ROPE_BASE ** (-fidx / shift))
    rot = (t >= 0) & (role > 0)
    c, s = jnp.cos(ang), jnp.sin(ang)
    return (jnp.where(rot, c, 1.0), jnp.where(rot & (role == 1), -s, 0.0), jnp.where(rot & (role == 2), s, 0.0))


def _block_diag(w):
    n, c, d = w.shape
    eye = jnp.eye(n, dtype=w.dtype)
    return (w[:, :, None, :] * eye[:, None, :, None]).reshape(n * c, n * d)


def _prep_layer(l, w_in, g_cq, w_uq, g_ckv, w_ukv, conv_w, conv_b, lru_wa, lru_ba, lru_wx, lru_bx,
                g_grp, w_out, w_g1, b_g1, w_g2, b_g2):
    d = w_in.shape[1]
    wi = w_in[l]
    zeros = lambda n: jnp.zeros((d, n), wi.dtype)
    win = jnp.concatenate([
        wi[:, 0:384], zeros(64), wi[:, 384:416], zeros(32), wi[:, 416:1952]], axis=1).astype(BF16)
    hq = MLA_NOPE + MLA_ROPE
    wuq = jnp.concatenate(
        [jnp.pad(w_uq[l][:, h * hq:(h + 1) * hq], ((0, 0), (0, LANES - hq))) for h in range(MLA_HEADS)],
        axis=1).astype(BF16)
    wkv = w_ukv[l].reshape(MLA_KV_RANK, MLA_HEADS, MLA_NOPE + MLA_V)
    wuk = jnp.pad(wkv[:, :, :MLA_NOPE], ((0, 0), (0, 0), (0, LANES - MLA_NOPE))).reshape(MLA_KV_RANK, -1)
    wuv = wkv[:, :, MLA_NOPE:].reshape(MLA_KV_RANK, -1)
    wukv = jnp.concatenate([wuk, wuv], axis=1).astype(BF16)
    lru = []
    for dr in range(2):
        lru.append(jnp.concatenate([_block_diag(lru_wa[l, dr]), _block_diag(lru_wx[l, dr])], axis=1).astype(BF16))
        lru.append(jnp.concatenate([lru_ba[l, dr], lru_bx[l, dr]])[None, :])
    wr = jnp.concatenate([jnp.moveaxis(w_g2[l], 0, 1).reshape(d, N_EXPERTS), w_g1[l]], axis=1)
    wr = jnp.pad(wr, ((0, 0), (0, LANES - wr.shape[1]))).astype(BF16)
    br = jnp.pad(jnp.concatenate([b_g2[l].reshape(-1), b_g1[l]]), (0, LANES - N_GROUPS - N_EXPERTS))[None, :]
    return dict(win=win, gcq=g_cq[l][None, :], wuq=wuq, gckv=g_ckv[l][None, :], wukv=wukv,
                cw=conv_w[l], cb=conv_b[l][None, :], wf=lru[0], bf=lru[1], wb=lru[2], bb=lru[3],
                gg=g_grp[l][None, :], wout=w_out[l].astype(BF16), wr=wr, br=br)


def kernel(x, c, ctx, c_ctx, w_ada, b_ada, g_norm1, g_norm2, w_in, g_cq, w_uq, g_ckv, w_ukv, swa_sink,
           conv_w, conv_b, lru_wa, lru_ba, lru_wx, lru_bx, lru_lam, g_grp, w_out, w_g1, b_g1, w_g2, b_g2,
           w_e_gate, w_e_up, w_e_down, g_final):
    bsz, seq, d = x.shape
    ctx_len = ctx.shape[1]
    depth = w_ada.shape[0]
    assert seq % TM == 0 and ctx_len % TM == 0 and seq % GRID_W == 0
    n_ctx_tiles = ctx_len // TM

    rows = -(-(bsz + 1) // SUBLANES) * SUBLANES
    cc = jnp.pad(jnp.concatenate([c, c_ctx[None, :]], axis=0), ((0, rows - bsz - 1), (0, 0)))
    mods = _ada_call(cc, w_ada, b_ada)

    q4 = MLA_ROPE // 4
    s4 = SWA_HEAD_DIM // 4
    mla_tabs = _rope_tables(ctx_len, seq, [(MLA_NOPE, False), (MLA_NOPE + 2 * q4, True)], q4)
    swa_tabs = _rope_tables(
        ctx_len, seq,
        [(hh * SWA_HEAD_DIM + ax * 2 * s4, bool(ax)) for hh in range(LANES // SWA_HEAD_DIM) for ax in range(2)], s4)
    tabs = mla_tabs + swa_tabs

    xs = jnp.concatenate([ctx, x], axis=1)
    for l in range(depth):
        p = _prep_layer(l, w_in, g_cq, w_uq, g_ckv, w_ukv, conv_w, conv_b, lru_wa, lru_ba, lru_wx, lru_bx,
                        g_grp, w_out, w_g1, b_g1, w_g2, b_g2)
        mod = mods[l].reshape(rows, 1, 6 * d)
        qm, km, vm, sq, sk, sv, lx, lg = _inproj_call(
            xs, mod, g_norm1[l][None, :], p["win"], p["gcq"], p["wuq"], p["gckv"], p["wukv"], tabs, n_ctx_tiles)
        om = _mla_call(qm, km, vm, n_ctx_tiles, ctx_len)
        osw = _swa_call(swa_sink[l], sq, sk, sv, ctx_len)
        ol = _lru_call(lx, lg, p["cw"], p["cb"], p["wf"], p["bf"], p["wb"], p["bb"], lru_lam[l], n_ctx_tiles)
        x1, h2, meta, cnt = _outproj_call(om, osw, ol, xs, mod, p["gg"], p["wout"], g_norm2[l][None, :],
                                          p["wr"], p["br"], n_ctx_tiles)
        xs = _moe_call(h2, meta, cnt, x1, mod, g_final[None, :], w_e_gate[l].astype(BF16),
                       w_e_up[l].astype(BF16), w_e_down[l].astype(BF16), n_ctx_tiles, l == depth - 1)
    return xs
```

```python
import functools

import jax
import jax.numpy as jnp
from jax import lax
from jax.experimental import pallas as pl
from jax.experimental.pallas import tpu as pltpu

GRID_W = 64
EPS = 1e-6
ROPE_BASE = 10000.0
NEG_INF = -1e30
MLA_HEADS = 4
MLA_NOPE = 64
MLA_ROPE = 32
MLA_V = 64
MLA_Q_RANK = 256
MLA_KV_RANK = 128
MLA_SCALE = (MLA_NOPE + MLA_ROPE) ** -0.5
SWA_HEADS = 4
SWA_KV_HEADS = 2
SWA_HEAD_DIM = 64
SWA_SCALE = SWA_HEAD_DIM ** -0.5
WINDOW = 128
LRU_WIDTH = 512
LRU_BLOCKS = 8
LRU_BW = LRU_WIDTH // LRU_BLOCKS
CONV_W = 4
LRU_C = 8.0
N_GROUPS = 4
EXPERTS_PER_GROUP = 8
N_EXPERTS = N_GROUPS * EXPERTS_PER_GROUP
D_EXPERT = 256
MLA_OUT = MLA_HEADS * MLA_V
SWA_OUT = SWA_HEADS * SWA_HEAD_DIM

LANES = 128
SUBLANES = 8
TM = 256
ZW = 2048
N_PAIRS = EXPERTS_PER_GROUP * (EXPERTS_PER_GROUP - 1) // 2
N_BUCKETS = N_GROUPS * N_PAIRS
MOE_TG = 256
MOE_TD = 512
LOG2E = 1.4426950408889634
VMEM_LIMIT = 56 * 1024 * 1024

F32 = jnp.float32
BF16 = jnp.bfloat16


def _params(sem):
    return pltpu.CompilerParams(dimension_semantics=sem, vmem_limit_bytes=VMEM_LIMIT)


def _full(shape):
    n = len(shape)
    return pl.BlockSpec(shape, lambda *_: (0,) * n)


def _rms(x, width):
    return x * lax.rsqrt(jnp.sum(x * x, axis=-1, keepdims=True) * (1.0 / width) + EPS)


def _dot(a, b):
    return jnp.dot(a, b, preferred_element_type=F32)


def _sigmoid(x):
    return 0.5 * jnp.tanh(0.5 * x) + 0.5


def _pack_pairs(x):
    w = x.shape[-1] // 2
    bits = pltpu.bitcast(x.astype(F32), jnp.uint32)
    return (bits[:, w:] & jnp.uint32(0xFFFF0000)) | (bits[:, :w] >> 16)


def _unpack_pairs(u):
    lo = pltpu.bitcast(u << 16, F32)
    hi = pltpu.bitcast(u & jnp.uint32(0xFFFF0000), F32)
    return jnp.concatenate([lo, hi], axis=-1).astype(BF16)


def _store_slabs(ref, x):
    n, w = x.shape
    for s in range(SUBLANES):
        piece = x[:, s * LANES:(s + 1) * LANES] if (s + 1) * LANES <= w else jnp.zeros((n, LANES), x.dtype)
        ref[pl.ds(s, n, stride=SUBLANES), :] = piece


def _load_slabs(ref, n, k):
    return jnp.concatenate([ref[pl.ds(s, n, stride=SUBLANES), :] for s in range(k)], axis=-1)


def _dot_nt(a, b):
    return lax.dot_general(a, b, (((1,), (1,)), ((), ())), preferred_element_type=F32)


def _ada_kernel(c_ref, w_ref, b_ref, o_ref):
    c = c_ref[...]
    s = c * jax.nn.sigmoid(c)
    o_ref[0] = jnp.dot(s, w_ref[0], preferred_element_type=F32,
                       precision=lax.Precision.HIGHEST) + b_ref[0]


def _ada_call(cc, w_ada, b_ada):
    depth, d, n = w_ada.shape
    r = cc.shape[0]
    tn = 1536
    return pl.pallas_call(
        _ada_kernel,
        grid=(depth, n // tn),
        in_specs=[
            pl.BlockSpec((r, d), lambda l, j: (0, 0)),
            pl.BlockSpec((1, d, tn), lambda l, j: (l, 0, j)),
            pl.BlockSpec((1, 1, tn), lambda l, j: (l, 0, j)),
        ],
        out_specs=pl.BlockSpec((1, r, tn), lambda l, j: (l, 0, j)),
        out_shape=jax.ShapeDtypeStruct((depth, r, n), F32),
        compiler_params=_params(("arbitrary", "arbitrary")),
        name="adaln",
    )(cc, w_ada, b_ada.reshape(depth, 1, n))


def _rope(x, cos, sina, sinb, shift):
    n = x.shape[-1]
    reps = n // LANES
    if reps > 1:
        cos = jnp.concatenate([cos] * reps, axis=-1)
        sina = jnp.concatenate([sina] * reps, axis=-1)
        sinb = jnp.concatenate([sinb] * reps, axis=-1)
    return x * cos + pltpu.roll(x, n - shift, 1) * sina + pltpu.roll(x, shift, 1) * sinb


def _inproj_kernel(x_ref, sh_ref, sc_ref, g1_ref, win_ref, gcq_ref, wuq_ref, gckv_ref, wukv_ref,
                   mcos_ref, msa_ref, msb_ref, scos_ref, ssa_ref, ssb_ref,
                   qm_ref, km_ref, vm_ref, sq_ref, sk_ref, sv_ref, lx_ref, lg_ref):
    d = x_ref.shape[-1]
    x = x_ref[0]
    h = _rms(x, d) * g1_ref[...] * (1.0 + sc_ref[0]) + sh_ref[0]
    z = _dot(h.astype(BF16), win_ref[...])

    mcos, msa, msb = mcos_ref[...], msa_ref[...], msb_ref[...]
    scos, ssa, ssb = scos_ref[...], ssa_ref[...], ssb_ref[...]

    cq = _rms(z[:, 0:256], MLA_Q_RANK) * gcq_ref[...]
    q = _dot(cq.astype(BF16), wuq_ref[...])
    q = _rope(q, mcos, msa, msb, MLA_ROPE // 4)
    qm_ref[0] = (q * (MLA_SCALE * LOG2E)).astype(BF16)

    ckv = _rms(z[:, 256:384], MLA_KV_RANK) * gckv_ref[...]
    kv = _dot(ckv.astype(BF16), wukv_ref[...])
    kr = _rope(z[:, 384:512], mcos, msa, msb, MLA_ROPE // 4)
    km_ref[0] = (kv[:, 0:512] + jnp.concatenate([kr] * MLA_HEADS, axis=-1)).astype(BF16)
    ones = jnp.ones((kv.shape[0], LANES), F32)
    vm_ref[0] = jnp.concatenate([kv[:, 512:640], ones, kv[:, 640:768], ones], axis=-1).astype(BF16)

    sq_ref[0] = (_rope(z[:, 512:768], scos, ssa, ssb, SWA_HEAD_DIM // 4) * (SWA_SCALE * LOG2E)).astype(BF16)
    sk = _rope(z[:, 768:896], scos, ssa, ssb, SWA_HEAD_DIM // 4)
    sv = z[:, 896:1024]
    lane = lax.broadcasted_iota(jnp.int32, sk.shape, 1)
    low = lane < SWA_HEAD_DIM
    k0 = jnp.where(low, sk, 0.0)
    k1 = jnp.where(low, 0.0, sk)
    sk_ref[0] = jnp.concatenate([k0 + pltpu.roll(k0, SWA_HEAD_DIM, 1), k1 + pltpu.roll(k1, SWA_HEAD_DIM, 1)],
                                axis=-1).astype(BF16)
    v0 = jnp.where(low, sv, 0.0)
    v1 = jnp.where(low, 0.0, sv)
    one_hi = jnp.where(low, 0.0, 1.0)
    one_lo = jnp.where(low, 1.0, 0.0)
    sv_ref[0] = jnp.concatenate([v0 + one_hi, pltpu.roll(v0, SWA_HEAD_DIM, 1) + one_lo,
                                 pltpu.roll(v1, SWA_HEAD_DIM, 1) + one_hi, v1 + one_lo], axis=-1).astype(BF16)
    lx_ref[0] = z[:, 1024:1536].astype(BF16)
    lg_ref[0] = z[:, 1536:2048].astype(BF16)


def _mod_spec(j, b_rows, n_ctx_tiles, d, tile_major):
    if tile_major:
        return pl.BlockSpec((1, 1, d), lambda i, b: (jnp.where(i < n_ctx_tiles, b_rows, b), 0, j))
    return pl.BlockSpec((1, 1, d), lambda b, i: (jnp.where(i < n_ctx_tiles, b_rows, b), 0, j))


def _inproj_call(xs, mod, g1, win, gcq, wuq, gckv, wukv, tabs, n_ctx_tiles):
    bsz, j, d = xs.shape
    nt = j // TM
    tok = lambda w: pl.BlockSpec((1, TM, w), lambda i, b: (b, i, 0))
    tab = pl.BlockSpec((TM, LANES), lambda i, b: (i, 0))
    widths = (512, 512, 512, 256, 256, 512, 512, 512)
    return pl.pallas_call(
        _inproj_kernel,
        grid=(nt, bsz),
        in_specs=[
            tok(d),
            _mod_spec(0, bsz, n_ctx_tiles, d, True),
            _mod_spec(1, bsz, n_ctx_tiles, d, True),
            _full(g1.shape), _full(win.shape), _full(gcq.shape), _full(wuq.shape),
            _full(gckv.shape), _full(wukv.shape),
            tab, tab, tab, tab, tab, tab,
        ],
        out_specs=[tok(w) for w in widths],
        out_shape=[jax.ShapeDtypeStruct((bsz, j, w), BF16) for w in widths],
        compiler_params=_params(("arbitrary", "arbitrary")),
        name="inproj",
    )(xs, mod, mod, g1, win, gcq, wuq, gckv, wukv, *tabs)


def _mla_heads(q_ref, k_ref, v_ref, o_ref, klen):
    lane = lax.broadcasted_iota(jnp.int32, (q_ref.shape[1], LANES), 1)
    for hp in range(MLA_HEADS // 2):
        v = v_ref[0, 0:klen, 2 * hp * LANES:2 * (hp + 1) * LANES]
        outs = []
        for h in range(2 * hp, 2 * hp + 2):
            q = q_ref[0, :, h * LANES:(h + 1) * LANES]
            k = k_ref[0, 0:klen, h * LANES:(h + 1) * LANES]
            s = _dot_nt(q, k)
            m = jnp.max(s, axis=-1, keepdims=True)
            o = _dot(jnp.exp2(s - m).astype(BF16), v)
            outs.append(o[:, 0:LANES] / o[:, LANES:LANES + 1])
        o_ref[0, :, hp * LANES:(hp + 1) * LANES] = jnp.where(lane < MLA_V, outs[0], outs[1]).astype(o_ref.dtype)


def _mla_kernel(q_ref, k_ref, v_ref, o_ref, *, n_ctx_tiles, ctx_len):
    i = pl.program_id(1)

    @pl.when(i < n_ctx_tiles)
    def _():
        _mla_heads(q_ref, k_ref, v_ref, o_ref, ctx_len)

    @pl.when(i >= n_ctx_tiles)
    def _():
        _mla_heads(q_ref, k_ref, v_ref, o_ref, k_ref.shape[1])


def _mla_call(qm, km, vm, n_ctx_tiles, ctx_len):
    bsz, j, w = qm.shape
    nt = j // TM
    return pl.pallas_call(
        functools.partial(_mla_kernel, n_ctx_tiles=n_ctx_tiles, ctx_len=ctx_len),
        grid=(bsz, nt),
        in_specs=[
            pl.BlockSpec((1, TM, w), lambda b, i: (b, i, 0)),
            pl.BlockSpec((1, j, w), lambda b, i: (b, 0, 0)),
            pl.BlockSpec((1, j, w), lambda b, i: (b, 0, 0)),
        ],
        out_specs=pl.BlockSpec((1, TM, MLA_OUT), lambda b, i: (b, i, 0)),
        out_shape=jax.ShapeDtypeStruct((bsz, j, MLA_OUT), BF16),
        compiler_params=_params(("arbitrary", "arbitrary")),
        name="mla_attn",
    )(qm, km, vm)


def _swa_kernel(sink_ref, q_ref, k_ref, v_ref, o_ref, *, ctx_len):
    i = pl.program_id(1)
    j = k_ref.shape[1]
    span = TM + 2 * WINDOW
    row0 = pl.multiple_of(jnp.clip(i * TM - WINDOW, 0, j - span), WINDOW)
    qpos = i * TM - ctx_len + lax.broadcasted_iota(jnp.int32, (TM, span), 0)
    kpos = row0 - ctx_len + lax.broadcasted_iota(jnp.int32, (TM, span), 1)
    valid = (jnp.abs(qpos - kpos) <= WINDOW) & (kpos >= 0) & (qpos >= 0)
    low = lax.broadcasted_iota(jnp.int32, (TM, LANES), 1) < SWA_HEAD_DIM
    group = SWA_HEADS // SWA_KV_HEADS
    for kvh in range(SWA_KV_HEADS):
        kcols = slice(kvh * LANES, (kvh + 1) * LANES)
        qpair = q_ref[0, :, kcols]
        kl = k_ref[0, pl.ds(row0, span), kcols]
        kc = k_ref[0, 0:ctx_len, kcols]
        halves = []
        for g in range(group):
            vcols = slice((kvh * group + g) * LANES, (kvh * group + g + 1) * LANES)
            q = jnp.where(low if g == 0 else ~low, qpair, jnp.zeros_like(qpair))
            s_loc = jnp.where(valid, _dot_nt(q, kl), NEG_INF)
            s_ctx = _dot_nt(q, kc)
            sink = sink_ref[kvh * group + g] * LOG2E
            m = jnp.maximum(jnp.maximum(jnp.max(s_loc, axis=-1, keepdims=True),
                                        jnp.max(s_ctx, axis=-1, keepdims=True)), sink)
            o = (_dot(jnp.exp2(s_loc - m).astype(BF16), v_ref[0, pl.ds(row0, span), vcols])
                 + _dot(jnp.exp2(s_ctx - m).astype(BF16), v_ref[0, 0:ctx_len, vcols]))
            den = o[:, SWA_HEAD_DIM:SWA_HEAD_DIM + 1] if g == 0 else o[:, 0:1]
            halves.append(o / (den + jnp.exp2(sink - m)))
        o_ref[0, :, kcols] = jnp.where(low, halves[0], halves[1]).astype(o_ref.dtype)


def _swa_call(sink, sq, sk, sv, ctx_len):
    bsz, j, _ = sq.shape
    nt = j // TM
    return pl.pallas_call(
        functools.partial(_swa_kernel, ctx_len=ctx_len),
        grid=(bsz, nt),
        in_specs=[
            pl.BlockSpec(memory_space=pltpu.SMEM),
            pl.BlockSpec((1, TM, SWA_OUT), lambda b, i: (b, i, 0)),
            pl.BlockSpec((1, j, sk.shape[-1]), lambda b, i: (b, 0, 0)),
            pl.BlockSpec((1, j, sv.shape[-1]), lambda b, i: (b, 0, 0)),
        ],
        out_specs=pl.BlockSpec((1, TM, SWA_OUT), lambda b, i: (b, i, 0)),
        out_shape=jax.ShapeDtypeStruct((bsz, j, SWA_OUT), BF16),
        compiler_params=_params(("arbitrary", "arbitrary")),
        name="swa_attn",
    )(sink, sq, sk, sv)


def _lru_gates(prev_ref, cur_ref, next_ref, cw_ref, cb_ref, w_ref, b_ref, sp_ref, seg_start, seg_end):
    z = cur_ref[0].astype(F32)
    row = lax.broadcasted_iota(jnp.int32, z.shape, 0)
    keep_prev = jnp.where(seg_start, 0.0, 1.0)
    keep_next = jnp.where(seg_end, 0.0, 1.0)
    p2 = prev_ref[0, TM - 2:TM - 1, :].astype(F32) * keep_prev
    p1 = prev_ref[0, TM - 1:TM, :].astype(F32) * keep_prev
    n0 = next_ref[0, 0:1, :].astype(F32) * keep_next
    z_m1 = jnp.where(row == 0, p1, pltpu.roll(z, 1, 0))
    z_m2 = jnp.where(row == 0, p2, jnp.where(row == 1, p1, pltpu.roll(z, 2, 0)))
    z_p1 = jnp.where(row == TM - 1, n0, pltpu.roll(z, TM - 1, 0))
    u = cb_ref[...] + z_m2 * cw_ref[0:1, :]
    u = u + z_m1 * cw_ref[1:2, :]
    u = u + z * cw_ref[2:3, :]
    u = u + z_p1 * cw_ref[3:4, :]
    g = _dot(u.astype(BF16), w_ref[...]) + b_ref[...]
    r = _sigmoid(g[:, 0:LRU_WIDTH])
    ig = _sigmoid(g[:, LRU_WIDTH:2 * LRU_WIDTH])
    log_a = (-LRU_C) * r * sp_ref[...]
    a = jnp.exp(log_a)
    om = 1.0 - a * a
    bt = jnp.where(om > 0.0, om * lax.rsqrt(om), 0.0) * (ig * u)
    return a, bt


def _lru_scan(a, b, h0, reverse):
    t, w = a.shape
    g = t // SUBLANES
    a = a.reshape(g, SUBLANES, w)
    b = b.reshape(g, SUBLANES, w)
    sub = lax.broadcasted_iota(jnp.int32, (g, SUBLANES, w), 1)
    d = 1
    while d < SUBLANES:
        if reverse:
            shift, ok = SUBLANES - d, sub < SUBLANES - d
        else:
            shift, ok = d, sub >= d
        a_sh = pltpu.roll(a, shift, 1)
        b_sh = pltpu.roll(b, shift, 1)
        b = jnp.where(ok, a * b_sh + b, b)
        a = jnp.where(ok, a * a_sh, a)
        d *= 2
    hs = [None] * g
    h = h0
    order = range(g - 1, -1, -1) if reverse else range(g)
    for gi in order:
        hg = a[gi] * h + b[gi]
        hs[gi] = hg
        h = hg[0:1, :] if reverse else hg[SUBLANES - 1:SUBLANES, :]
    return jnp.concatenate(hs, axis=0), h


def _softplus_neg(lam):
    x = -lam
    return jnp.maximum(x, 0.0) + jnp.log1p(jnp.exp(-jnp.abs(x)))


def _lru_fwd_kernel(prev_ref, cur_ref, next_ref, cw_ref, cb_ref, w_ref, b_ref, lam_ref,
                    hf_ref, carry_ref, sp_ref, *, n_ctx_tiles):
    c = pl.program_id(1)
    nt = pl.num_programs(1)

    @pl.when(c == 0)
    def _():
        carry_ref[...] = jnp.zeros_like(carry_ref)
        sp_ref[...] = _softplus_neg(lam_ref[...])

    seg_start = (c == 0) | (c == n_ctx_tiles)
    seg_end = (c == n_ctx_tiles - 1) | (c == nt - 1)
    a, bt = _lru_gates(prev_ref, cur_ref, next_ref, cw_ref, cb_ref, w_ref, b_ref, sp_ref,
                       seg_start, seg_end)
    hs, h = _lru_scan(a, bt, carry_ref[...], False)
    carry_ref[...] = h
    hf_ref[0] = hs.astype(hf_ref.dtype)


def _bwd_chunk(s, nt, n_ctx_tiles):
    return jnp.where(s < n_ctx_tiles, n_ctx_tiles - 1 - s, nt - 1 - (s - n_ctx_tiles))


def _lru_bwd_kernel(prev_ref, cur_ref, next_ref, cw_ref, cb_ref, w_ref, b_ref, lam_ref,
                    hf_ref, lg_ref, o_ref, carry_ref, sp_ref, *, n_ctx_tiles):
    s = pl.program_id(1)
    nt = pl.num_programs(1)
    c = _bwd_chunk(s, nt, n_ctx_tiles)

    @pl.when(s == 0)
    def _():
        carry_ref[...] = jnp.zeros_like(carry_ref)
        sp_ref[...] = _softplus_neg(lam_ref[...])

    seg_start = (c == 0) | (c == n_ctx_tiles)
    seg_end = (c == n_ctx_tiles - 1) | (c == nt - 1)
    a, bt = _lru_gates(prev_ref, cur_ref, next_ref, cw_ref, cb_ref, w_ref, b_ref, sp_ref,
                       seg_start, seg_end)
    hs, h = _lru_scan(a, bt, carry_ref[...], True)
    carry_ref[...] = h
    gate = jax.nn.gelu(lg_ref[0].astype(F32), approximate=True)
    o_ref[0] = ((hf_ref[0].astype(F32) + hs) * gate).astype(o_ref.dtype)


def _lru_call(lx, lg, cw, cb, wf, bf, wb, bb, lam, n_ctx_tiles):
    bsz, j, w = lx.shape
    nt = j // TM
    consts = [_full(cw.shape), _full((1, w)), _full(wf.shape), _full(bf.shape), _full((1, w))]
    scratch = [pltpu.VMEM((1, w), F32), pltpu.VMEM((1, w), F32)]
    blk = lambda f: pl.BlockSpec((1, TM, w), f)
    hf = pl.pallas_call(
        functools.partial(_lru_fwd_kernel, n_ctx_tiles=n_ctx_tiles),
        grid=(bsz, nt),
        in_specs=[
            blk(lambda b, c: (b, jnp.maximum(c - 1, 0), 0)),
            blk(lambda b, c: (b, c, 0)),
            blk(lambda b, c: (b, jnp.minimum(c + 1, nt - 1), 0)),
        ] + consts,
        out_specs=blk(lambda b, c: (b, c, 0)),
        out_shape=jax.ShapeDtypeStruct((bsz, j, w), BF16),
        scratch_shapes=scratch,
        compiler_params=_params(("arbitrary", "arbitrary")),
        name="lru_fwd",
    )(lx, lx, lx, cw, cb, wf, bf, lam[0:1])
    cix = lambda s: _bwd_chunk(s, nt, n_ctx_tiles)
    return pl.pallas_call(
        functools.partial(_lru_bwd_kernel, n_ctx_tiles=n_ctx_tiles),
        grid=(bsz, nt),
        in_specs=[
            blk(lambda b, s: (b, jnp.maximum(cix(s) - 1, 0), 0)),
            blk(lambda b, s: (b, cix(s), 0)),
            blk(lambda b, s: (b, jnp.minimum(cix(s) + 1, nt - 1), 0)),
        ] + consts + [
            blk(lambda b, s: (b, cix(s), 0)),
            blk(lambda b, s: (b, cix(s), 0)),
        ],
        out_specs=blk(lambda b, s: (b, cix(s), 0)),
        out_shape=jax.ShapeDtypeStruct((bsz, j, w), BF16),
        scratch_shapes=scratch,
        compiler_params=_params(("arbitrary", "arbitrary")),
        name="lru_bwd",
    )(lx, lx, lx, cw, cb, wb, bb, lam[1:2], hf, lg)


def _route(logits, carry):
    lt = logits.T
    tm = lt.shape[1]
    big = jnp.int32(LANES)
    lg = lt[N_EXPERTS:N_EXPERTS + SUBLANES]
    rg = lax.broadcasted_iota(jnp.int32, lg.shape, 0)
    is_g = rg < N_GROUPS
    mg = jnp.max(jnp.where(is_g, lg, -jnp.inf), axis=0, keepdims=True)
    eg = jnp.where(is_g, jnp.exp(lg - mg), 0.0)
    pg = eg / jnp.sum(eg, axis=0, keepdims=True)
    pg_top = jnp.max(pg, axis=0, keepdims=True)
    g_idx = jnp.min(jnp.where(is_g & (pg == pg_top), rg, big), axis=0, keepdims=True)
    le = lt[0:N_EXPERTS]
    re = lax.broadcasted_iota(jnp.int32, le.shape, 0)
    lo = g_idx * EXPERTS_PER_GROUP
    sel = (re >= lo) & (re < lo + EXPERTS_PER_GROUP)
    me = jnp.max(jnp.where(sel, le, -jnp.inf), axis=0, keepdims=True)
    ee = jnp.where(sel, jnp.exp(le - me), 0.0)
    pe = ee / jnp.sum(ee, axis=0, keepdims=True)
    p1 = jnp.max(jnp.where(sel, pe, -1.0), axis=0, keepdims=True)
    i1 = jnp.min(jnp.where(sel & (pe == p1), re, big), axis=0, keepdims=True)
    sel2 = sel & (re != i1)
    p2 = jnp.max(jnp.where(sel2, pe, -1.0), axis=0, keepdims=True)
    i2 = jnp.min(jnp.where(sel2 & (pe == p2), re, big), axis=0, keepdims=True)
    den = p1 + p2
    w1 = pg_top * p1 / den
    w2 = pg_top * p2 / den
    first_lo = i1 < i2
    ia = (jnp.where(first_lo, i1, i2) - lo).astype(F32)
    ib = (jnp.where(first_lo, i2, i1) - lo).astype(F32)
    wa = jnp.where(first_lo, w1, w2)
    wb = jnp.where(first_lo, w2, w1)
    pair = ia * (2 * EXPERTS_PER_GROUP - 1 - ia) * 0.5 + (ib - ia - 1.0)
    bucket = (g_idx.astype(F32) * N_PAIRS + pair).astype(jnp.int32)
    rb = lax.broadcasted_iota(jnp.int32, (LANES, tm), 0)
    mine = rb == bucket
    picks = jnp.where(mine, 1.0, 0.0)
    tri = jnp.where(lax.broadcasted_iota(jnp.int32, (tm, tm), 0) < lax.broadcasted_iota(jnp.int32, (tm, tm), 1),
                    1.0, 0.0).astype(BF16)
    before = _dot(picks.astype(BF16), tri) + carry
    rank = jnp.sum(jnp.where(mine, before, 0.0), axis=0, keepdims=True)
    zeros = jnp.zeros((SUBLANES - 2, tm), F32)
    meta = jnp.concatenate([bucket.astype(F32), rank, zeros], axis=0).astype(jnp.int32)
    wrows = jnp.concatenate([wa, wb, jnp.zeros((LANES - 2, tm), F32)], axis=0)
    return meta, wrows.T, carry + jnp.sum(picks, axis=1, keepdims=True)


def _outproj_kernel(om_ref, os_ref, ol_ref, x_ref, gt1_ref, sh2_ref, sc2_ref, gg_ref, wout_ref,
                    g2_ref, wr_ref, br_ref, x1_ref, h2_ref, meta_ref, cnt_ref, carry_ref):
    d = x_ref.shape[-1]

    @pl.when((pl.program_id(0) == 0) & (pl.program_id(1) == 0))
    def _():
        carry_ref[...] = jnp.zeros_like(carry_ref)

    gg = gg_ref[...]
    nm = _rms(om_ref[0].astype(F32), MLA_OUT) * gg[:, 0:MLA_OUT]
    ns = _rms(os_ref[0].astype(F32), SWA_OUT) * gg[:, MLA_OUT:MLA_OUT + SWA_OUT]
    nl = _rms(ol_ref[0].astype(F32), LRU_WIDTH) * gg[:, MLA_OUT + SWA_OUT:]
    merged = jnp.concatenate([nm, ns, nl], axis=-1).astype(BF16)
    mix = _dot(merged, wout_ref[...])
    x1 = x_ref[0] + gt1_ref[0] * mix
    x1_ref[0] = x1
    h2 = (_rms(x1, d) * g2_ref[...] * (1.0 + sc2_ref[0]) + sh2_ref[0]).astype(BF16)
    meta, wcols, carry = _route(_dot(h2, wr_ref[...]) + br_ref[...], carry_ref[...])
    _store_slabs(h2_ref, jnp.concatenate([_pack_pairs(h2), pltpu.bitcast(wcols, jnp.uint32)], axis=-1))
    carry_ref[...] = carry
    cnt_ref[...] = jnp.broadcast_to(carry, (LANES, LANES)).T[0:1, :].astype(jnp.int32)
    for m in range(TM // LANES):
        meta_ref[m] = meta[:, m * LANES:(m + 1) * LANES]


def _outproj_call(om, osw, ol, xs, mod, gg, wout, g2, wr, br, n_ctx_tiles):
    bsz, j, d = xs.shape
    nt = j // TM
    tok = lambda w: pl.BlockSpec((1, TM, w), lambda i, b: (b, i, 0))
    return pl.pallas_call(
        _outproj_kernel,
        grid=(nt, bsz),
        in_specs=[
            tok(MLA_OUT), tok(SWA_OUT), tok(LRU_WIDTH), tok(d),
            _mod_spec(2, bsz, n_ctx_tiles, d, True),
            _mod_spec(3, bsz, n_ctx_tiles, d, True),
            _mod_spec(4, bsz, n_ctx_tiles, d, True),
            _full(gg.shape), _full(wout.shape), _full(g2.shape), _full(wr.shape), _full(br.shape),
        ],
        out_specs=[
            tok(d), pl.BlockSpec((TM * SUBLANES, LANES), lambda i, b: (b * nt + i, 0)),
            pl.BlockSpec((TM // LANES, SUBLANES, LANES), lambda i, b: (b * nt + i, 0, 0)),
            pl.BlockSpec((1, LANES), lambda i, b: (0, 0)),
        ],
        out_shape=[
            jax.ShapeDtypeStruct((bsz, j, d), F32),
            jax.ShapeDtypeStruct((bsz * j * SUBLANES, LANES), jnp.uint32),
            jax.ShapeDtypeStruct((bsz * j // LANES, SUBLANES, LANES), jnp.int32),
            jax.ShapeDtypeStruct((1, LANES), jnp.int32),
        ],
        scratch_shapes=[pltpu.VMEM((LANES, 1), F32)],
        compiler_params=_params(("arbitrary", "arbitrary")),
        name="outproj_route",
    )(om, osw, ol, xs, mod, mod, mod, gg, wout, g2, wr, br)


def _padded_tiles(count):
    return lax.shift_right_logical(count + (MOE_TG - 1), MOE_TG.bit_length() - 1)


def _plan_kernel(cnt_ref, off_ref, tea_ref, teb_ref, nu_ref, *, n_tiles_max):
    off = jnp.int32(0)
    ti = jnp.int32(0)
    bucket = 0
    for g in range(N_GROUPS):
        for a in range(EXPERTS_PER_GROUP):
            for b in range(a + 1, EXPERTS_PER_GROUP):
                n_q = _padded_tiles(cnt_ref[0, bucket])
                off_ref[bucket] = off

                def fill(k, _, base=ti, ea=g * EXPERTS_PER_GROUP + a, eb=g * EXPERTS_PER_GROUP + b):
                    tea_ref[base + k] = ea
                    teb_ref[base + k] = eb
                    return 0

                lax.fori_loop(0, n_q, fill, 0)
                off = off + n_q * MOE_TG
                ti = ti + n_q
                bucket += 1
    nu_ref[0] = ti

    def rest(k, _):
        tea_ref[k] = N_EXPERTS - 2
        teb_ref[k] = N_EXPERTS - 1
        return 0

    lax.fori_loop(ti, n_tiles_max, rest, 0)


def _plan_call(cnt, n_tiles_max):
    smem = pl.BlockSpec(memory_space=pltpu.SMEM)
    return pl.pallas_call(
        functools.partial(_plan_kernel, n_tiles_max=n_tiles_max),
        in_specs=[smem],
        out_specs=[smem, smem, smem, smem],
        out_shape=[
            jax.ShapeDtypeStruct((N_BUCKETS,), jnp.int32),
            jax.ShapeDtypeStruct((n_tiles_max,), jnp.int32),
            jax.ShapeDtypeStruct((n_tiles_max,), jnp.int32),
            jax.ShapeDtypeStruct((1,), jnp.int32),
        ],
        name="moe_plan",
    )(cnt)


def _row_copy(src_ref, src_row, dst_ref, dst_row, sem):
    tile_of = lambda row: pl.ds(pl.multiple_of(row * SUBLANES, SUBLANES), SUBLANES)
    return pltpu.make_async_copy(src_ref.at[tile_of(src_row)], dst_ref.at[tile_of(dst_row)], sem)


def _dispatch_kernel(off_ref, nu_ref, h_ref, meta_ref, cnt_ref, xs_ref, zero_ref, sem, tile_sem, *, tile):
    tile_rows = MOE_TG * SUBLANES
    n_tiles_max = xs_ref.shape[0] // tile_rows

    @pl.when(pl.program_id(0) == 0)
    def _():
        zero_ref[...] = jnp.zeros_like(zero_ref)

        def per_bucket(q, n):
            c = cnt_ref[0, q]
            padded = _padded_tiles(c) * MOE_TG
            base = off_ref[q]

            def fill(r, _):
                _row_copy(zero_ref, 0, xs_ref, base + r, sem).start()
                return 0

            lax.fori_loop(c, padded, fill, 0)
            return n + (padded - c)

        n_pad = lax.fori_loop(0, N_BUCKETS, per_bucket, jnp.int32(0))

        def drain(r, _):
            _row_copy(zero_ref, 0, xs_ref, 0, sem).wait()
            return 0

        lax.fori_loop(0, n_pad, drain, 0)

        def unused_tile(ti, _):
            cp = pltpu.make_async_copy(
                zero_ref, xs_ref.at[pl.ds(pl.multiple_of(ti * tile_rows, tile_rows), tile_rows)], tile_sem)
            cp.start()
            cp.wait()
            return 0

        lax.fori_loop(nu_ref[0], n_tiles_max, unused_tile, 0)

    for m in range(tile // LANES):
        for r in range(LANES):
            dst = off_ref[meta_ref[m, 0, r]] + meta_ref[m, 1, r]
            _row_copy(h_ref, m * LANES + r, xs_ref, dst, sem).start(priority=r % 2)

    pltpu.make_async_copy(h_ref, xs_ref.at[pl.ds(0, tile * SUBLANES)], sem).wait()


def _dispatch_call(off, nu, h2, meta, cnt, rows_max):
    t = h2.shape[0] // SUBLANES
    tile = MOE_TD if t % MOE_TD == 0 else TM
    return pl.pallas_call(
        functools.partial(_dispatch_kernel, tile=tile),
        grid_spec=pltpu.PrefetchScalarGridSpec(
            num_scalar_prefetch=2,
            grid=(t // tile,),
            in_specs=[
                pl.BlockSpec((tile * SUBLANES, LANES), lambda i, off, nu: (i, 0)),
                pl.BlockSpec((tile // LANES, SUBLANES, LANES), lambda i, off, nu: (i, 0, 0),
                             memory_space=pltpu.SMEM),
                pl.BlockSpec(memory_space=pltpu.SMEM),
            ],
            out_specs=pl.BlockSpec(memory_space=pl.ANY),
            scratch_shapes=[pltpu.VMEM((MOE_TG * SUBLANES, LANES), h2.dtype), pltpu.SemaphoreType.DMA,
                            pltpu.SemaphoreType.DMA],
        ),
        out_shape=jax.ShapeDtypeStruct((rows_max * SUBLANES, LANES), h2.dtype),
        compiler_params=_params(("arbitrary",)),
        name="moe_dispatch",
    )(off, nu, h2, meta, cnt)


def _expert_mlp(x, wg_ref, wu_ref, wd_ref):
    gate = _dot(x, wg_ref[0])
    act = gate * _sigmoid(gate) * _dot(x, wu_ref[0])
    return _dot(act.astype(BF16), wd_ref[0])


def _expert_kernel(tea_ref, teb_ref, nu_ref, x_ref, wga_ref, wua_ref, wda_ref, wgb_ref, wub_ref, wdb_ref, y_ref):
    groups = wga_ref.shape[1] // (2 * LANES)

    @pl.when(pl.program_id(0) < nu_ref[0])
    def _():
        row = _load_slabs(x_ref, MOE_TG, groups + 1)
        x = _unpack_pairs(row[:, 0:groups * LANES])
        wts = pltpu.bitcast(row[:, groups * LANES:], F32)
        y = (wts[:, 0:1] * _expert_mlp(x, wga_ref, wua_ref, wda_ref)
             + wts[:, 1:2] * _expert_mlp(x, wgb_ref, wub_ref, wdb_ref))
        _store_slabs(y_ref, _pack_pairs(y.astype(BF16)))

    @pl.when(pl.program_id(0) >= nu_ref[0])
    def _():
        y_ref[...] = jnp.zeros_like(y_ref)


def _expert_call(tea, teb, nu, xs, wg, wu, wd):
    d = wg.shape[1]
    n_tiles = xs.shape[0] // (MOE_TG * SUBLANES)
    used = lambda i, nu: jnp.minimum(i, nu[0] - 1)
    wspec = lambda shape, te_pos: pl.BlockSpec(
        shape, lambda i, tea, teb, nu: ((tea, teb)[te_pos][used(i, nu)], 0, 0))
    return pl.pallas_call(
        _expert_kernel,
        grid_spec=pltpu.PrefetchScalarGridSpec(
            num_scalar_prefetch=3,
            grid=(n_tiles,),
            in_specs=[
                pl.BlockSpec((MOE_TG * SUBLANES, LANES), lambda i, tea, teb, nu: (used(i, nu), 0)),
                wspec((1, d, D_EXPERT), 0), wspec((1, d, D_EXPERT), 0), wspec((1, D_EXPERT, d), 0),
                wspec((1, d, D_EXPERT), 1), wspec((1, d, D_EXPERT), 1), wspec((1, D_EXPERT, d), 1),
            ],
            out_specs=pl.BlockSpec((MOE_TG * SUBLANES, LANES), lambda i, tea, teb, nu: (i, 0)),
        ),
        out_shape=jax.ShapeDtypeStruct(xs.shape, jnp.uint32),
        compiler_params=_params(("arbitrary",)),
        name="moe_experts",
    )(tea, teb, nu, xs, wg, wu, wd, wg, wu, wd)


def _combine_kernel(off_ref, meta_ref, ys_ref, x_ref, gt2_ref, gf_ref, o_ref, y_ref, sem, *, final, n_ctx_tiles,
                    tiles_per_batch):
    def body():
        for m in range(TM // LANES):
            for r in range(LANES):
                src = off_ref[meta_ref[m, 0, r]] + meta_ref[m, 1, r]
                _row_copy(ys_ref, src, y_ref, m * LANES + r, sem).start(priority=r % 2)

        pltpu.make_async_copy(ys_ref.at[pl.ds(0, TM * SUBLANES)], y_ref, sem).wait()
        groups = x_ref.shape[-1] // (2 * LANES)
        x2 = x_ref[...] + gt2_ref[0] * _unpack_pairs(_load_slabs(y_ref, TM, groups)).astype(F32)
        if final:
            x2 = _rms(x2, x2.shape[-1]) * gf_ref[...]
        o_ref[...] = x2

    if final:
        pl.when(pl.program_id(0) % tiles_per_batch >= n_ctx_tiles)(body)
    else:
        body()


def _combine_call(off, meta, ys, x1, mod, gf, tiles_per_batch, n_ctx_tiles, bsz, final):
    t, d = x1.shape
    lat = tiles_per_batch - n_ctx_tiles
    mod_row = lambda i, off: (jnp.where(i % tiles_per_batch < n_ctx_tiles, bsz, i // tiles_per_batch), 0, 5)
    if final:
        out_rows = bsz * lat * TM
        out_ix = lambda i, off: ((i // tiles_per_batch) * lat + jnp.maximum(i % tiles_per_batch - n_ctx_tiles, 0), 0)
    else:
        out_rows = t
        out_ix = lambda i, off: (i, 0)
    return pl.pallas_call(
        functools.partial(_combine_kernel, final=final, n_ctx_tiles=n_ctx_tiles, tiles_per_batch=tiles_per_batch),
        grid_spec=pltpu.PrefetchScalarGridSpec(
            num_scalar_prefetch=1,
            grid=(t // TM,),
            in_specs=[
                pl.BlockSpec((TM // LANES, SUBLANES, LANES), lambda i, off: (i, 0, 0), memory_space=pltpu.SMEM),
                pl.BlockSpec(memory_space=pl.ANY),
                pl.BlockSpec((TM, d), lambda i, off: (i, 0)),
                pl.BlockSpec((1, 1, d), mod_row),
                pl.BlockSpec((1, d), lambda i, off: (0, 0)),
            ],
            out_specs=pl.BlockSpec((TM, d), out_ix),
            scratch_shapes=[pltpu.VMEM((TM * SUBLANES, LANES), jnp.uint32), pltpu.SemaphoreType.DMA],
        ),
        out_shape=jax.ShapeDtypeStruct((out_rows, d), F32),
        compiler_params=_params(("arbitrary",)),
        name="moe_combine",
    )(off, meta, ys, x1, mod, gf)


def _moe_call(h2, meta, cnt, x1, mod, gf, wg, wu, wd, n_ctx_tiles, final):
    bsz, j, d = x1.shape
    t = bsz * j
    rows_max = -(-t // MOE_TG) * MOE_TG + N_BUCKETS * MOE_TG
    off, tea, teb, nu = _plan_call(cnt, rows_max // MOE_TG)
    xs = _dispatch_call(off, nu, h2, meta, cnt, rows_max)
    ys = _expert_call(tea, teb, nu, xs, wg, wu, wd)
    out = _combine_call(off, meta, ys, x1.reshape(t, d), mod, gf, j // TM, n_ctx_tiles, bsz, final)
    return out.reshape(bsz, -1, d)


def _rope_tables(ctx_len, seq, groups, shift):
    fidx = [0] * LANES
    cols = [False] * LANES
    role = [0] * LANES
    for start, use_cols in groups:
        for k in range(shift):
            for half in range(2):
                lane = start + half * shift + k
                fidx[lane], cols[lane], role[lane] = k, use_cols, half + 1
    fidx = jnp.asarray(fidx, F32)[None, :]
    cols = jnp.asarray(cols)[None, :]
    role = jnp.asarray(role, jnp.int32)[None, :]
    t = jnp.arange(ctx_len + seq, dtype=jnp.int32)[:, None] - ctx_len
    pos = jnp.where(cols, t % GRID_W, t // GRID_W).astype(F32)
    ang = pos * (ROPE_BASE ** (-fidx / shift))
    rot = (t >= 0) & (role > 0)
    c, s = jnp.cos(ang), jnp.sin(ang)
    return (jnp.where(rot, c, 1.0), jnp.where(rot & (role == 1), -s, 0.0), jnp.where(rot & (role == 2), s, 0.0))


def _block_diag(w):
    n, c, d = w.shape
    eye = jnp.eye(n, dtype=w.dtype)
    return (w[:, :, None, :] * eye[:, None, :, None]).reshape(n * c, n * d)


def _prep_layer(l, w_in, g_cq, w_uq, g_ckv, w_ukv, conv_w, conv_b, lru_wa, lru_ba, lru_wx, lru_bx,
                g_grp, w_out, w_g1, b_g1, w_g2, b_g2):
    d = w_in.shape[1]
    wi = w_in[l]
    zeros = lambda n: jnp.zeros((d, n), wi.dtype)
    win = jnp.concatenate([
        wi[:, 0:384], zeros(64), wi[:, 384:416], zeros(32), wi[:, 416:1952]], axis=1).astype(BF16)
    hq = MLA_NOPE + MLA_ROPE
    wuq = jnp.concatenate(
        [jnp.pad(w_uq[l][:, h * hq:(h + 1) * hq], ((0, 0), (0, LANES - hq))) for h in range(MLA_HEADS)],
        axis=1).astype(BF16)
    wkv = w_ukv[l].reshape(MLA_KV_RANK, MLA_HEADS, MLA_NOPE + MLA_V)
    wuk = jnp.pad(wkv[:, :, :MLA_NOPE], ((0, 0), (0, 0), (0, LANES - MLA_NOPE))).reshape(MLA_KV_RANK, -1)
    wuv = wkv[:, :, MLA_NOPE:].reshape(MLA_KV_RANK, -1)
    wukv = jnp.concatenate([wuk, wuv], axis=1).astype(BF16)
    lru = []
    for dr in range(2):
        lru.append(jnp.concatenate([_block_diag(lru_wa[l, dr]), _block_diag(lru_wx[l, dr])], axis=1).astype(BF16))
        lru.append(jnp.concatenate([lru_ba[l, dr], lru_bx[l, dr]])[None, :])
    wr = jnp.concatenate([jnp.moveaxis(w_g2[l], 0, 1).reshape(d, N_EXPERTS), w_g1[l]], axis=1)
    wr = jnp.pad(wr, ((0, 0), (0, LANES - wr.shape[1]))).astype(BF16)
    br = jnp.pad(jnp.concatenate([b_g2[l].reshape(-1), b_g1[l]]), (0, LANES - N_GROUPS - N_EXPERTS))[None, :]
    return dict(win=win, gcq=g_cq[l][None, :], wuq=wuq, gckv=g_ckv[l][None, :], wukv=wukv,
                cw=conv_w[l], cb=conv_b[l][None, :], wf=lru[0], bf=lru[1], wb=lru[2], bb=lru[3],
                gg=g_grp[l][None, :], wout=w_out[l].astype(BF16), wr=wr, br=br)


def kernel(x, c, ctx, c_ctx, w_ada, b_ada, g_norm1, g_norm2, w_in, g_cq, w_uq, g_ckv, w_ukv, swa_sink,
           conv_w, conv_b, lru_wa, lru_ba, lru_wx, lru_bx, lru_lam, g_grp, w_out, w_g1, b_g1, w_g2, b_g2,
           w_e_gate, w_e_up, w_e_down, g_final):
    bsz, seq, d = x.shape
    ctx_len = ctx.shape[1]
    depth = w_ada.shape[0]
    assert seq % TM == 0 and ctx_len % TM == 0 and seq % GRID_W == 0
    n_ctx_tiles = ctx_len // TM

    rows = -(-(bsz + 1) // SUBLANES) * SUBLANES
    cc = jnp.pad(jnp.concatenate([c, c_ctx[None, :]], axis=0), ((0, rows - bsz - 1), (0, 0)))
    mods = _ada_call(cc, w_ada, b_ada)

    q4 = MLA_ROPE // 4
    s4 = SWA_HEAD_DIM // 4
    mla_tabs = _rope_tables(ctx_len, seq, [(MLA_NOPE, False), (MLA_NOPE + 2 * q4, True)], q4)
    swa_tabs = _rope_tables(
        ctx_len, seq,
        [(hh * SWA_HEAD_DIM + ax * 2 * s4, bool(ax)) for hh in range(LANES // SWA_HEAD_DIM) for ax in range(2)], s4)
    tabs = mla_tabs + swa_tabs

    xs = jnp.concatenate([ctx, x], axis=1)
    for l in range(depth):
        p = _prep_layer(l, w_in, g_cq, w_uq, g_ckv, w_ukv, conv_w, conv_b, lru_wa, lru_ba, lru_wx, lru_bx,
                        g_grp, w_out, w_g1, b_g1, w_g2, b_g2)
        mod = mods[l].reshape(rows, 1, 6 * d)
        qm, km, vm, sq, sk, sv, lx, lg = _inproj_call(
            xs, mod, g_norm1[l][None, :], p["win"], p["gcq"], p["wuq"], p["gckv"], p["wukv"], tabs, n_ctx_tiles)
        om = _mla_call(qm, km, vm, n_ctx_tiles, ctx_len)
        osw = _swa_call(swa_sink[l], sq, sk, sv, ctx_len)
        ol = _lru_call(lx, lg, p["cw"], p["cb"], p["wf"], p["bf"], p["wb"], p["bb"], lru_lam[l], n_ctx_tiles)
        x1, h2, meta, cnt = _outproj_call(om, osw, ol, xs, mod, p["gg"], p["wout"], g_norm2[l][None, :],
                                          p["wr"], p["br"], n_ctx_tiles)
        xs = _moe_call(h2, meta, cnt, x1, mod, g_final[None, :], w_e_gate[l].astype(BF16),
                       w_e_up[l].astype(BF16), w_e_down[l].astype(BF16), n_ctx_tiles, l == depth - 1)
    return xs
```

```python
import functools

import jax
import jax.numpy as jnp
from jax import lax
from jax.experimental import pallas as pl
from jax.experimental.pallas import tpu as pltpu

GRID_W = 64
EPS = 1e-6
ROPE_BASE = 10000.0
NEG_INF = -1e30
MLA_HEADS = 4
MLA_NOPE = 64
MLA_ROPE = 32
MLA_V = 64
MLA_Q_RANK = 256
MLA_KV_RANK = 128
MLA_SCALE = (MLA_NOPE + MLA_ROPE) ** -0.5
SWA_HEADS = 4
SWA_KV_HEADS = 2
SWA_HEAD_DIM = 64
SWA_SCALE = SWA_HEAD_DIM ** -0.5
WINDOW = 128
LRU_WIDTH = 512
LRU_BLOCKS = 8
LRU_BW = LRU_WIDTH // LRU_BLOCKS
CONV_W = 4
LRU_C = 8.0
N_GROUPS = 4
EXPERTS_PER_GROUP = 8
N_EXPERTS = N_GROUPS * EXPERTS_PER_GROUP
D_EXPERT = 256
MLA_OUT = MLA_HEADS * MLA_V
SWA_OUT = SWA_HEADS * SWA_HEAD_DIM

LANES = 128
SUBLANES = 8
TM = 256
ZW = 2048
N_PAIRS = EXPERTS_PER_GROUP * (EXPERTS_PER_GROUP - 1) // 2
N_BUCKETS = N_GROUPS * N_PAIRS
MOE_TG = 256
MOE_TD = 512
LOG2E = 1.4426950408889634
VMEM_LIMIT = 56 * 1024 * 1024

F32 = jnp.float32
BF16 = jnp.bfloat16


def _params(sem):
    return pltpu.CompilerParams(dimension_semantics=sem, vmem_limit_bytes=VMEM_LIMIT)


def _full(shape):
    n = len(shape)
    return pl.BlockSpec(shape, lambda *_: (0,) * n)


def _rms(x, width):
    return x * lax.rsqrt(jnp.sum(x * x, axis=-1, keepdims=True) * (1.0 / width) + EPS)


def _dot(a, b):
    return jnp.dot(a, b, preferred_element_type=F32)


def _sigmoid(x):
    return 0.5 * jnp.tanh(0.5 * x) + 0.5


def _pack_pairs(x):
    w = x.shape[-1] // 2
    bits = pltpu.bitcast(x.astype(F32), jnp.uint32)
    return (bits[:, w:] & jnp.uint32(0xFFFF0000)) | (bits[:, :w] >> 16)


def _unpack_pairs(u):
    lo = pltpu.bitcast(u << 16, F32)
    hi = pltpu.bitcast(u & jnp.uint32(0xFFFF0000), F32)
    return jnp.concatenate([lo, hi], axis=-1).astype(BF16)


def _store_slabs(ref, x):
    n, w = x.shape
    for s in range(SUBLANES):
        piece = x[:, s * LANES:(s + 1) * LANES] if (s + 1) * LANES <= w else jnp.zeros((n, LANES), x.dtype)
        ref[pl.ds(s, n, stride=SUBLANES), :] = piece


def _load_slabs(ref, n, k):
    return jnp.concatenate([ref[pl.ds(s, n, stride=SUBLANES), :] for s in range(k)], axis=-1)


def _dot_nt(a, b):
    return lax.dot_general(a, b, (((1,), (1,)), ((), ())), preferred_element_type=F32)


def _ada_kernel(c_ref, w_ref, b_ref, o_ref):
    c = c_ref[...]
    s = c * jax.nn.sigmoid(c)
    o_ref[0] = jnp.dot(s, w_ref[0], preferred_element_type=F32,
                       precision=lax.Precision.HIGHEST) + b_ref[0]


def _ada_call(cc, w_ada, b_ada):
    depth, d, n = w_ada.shape
    r = cc.shape[0]
    tn = 1536
    return pl.pallas_call(
        _ada_kernel,
        grid=(depth, n // tn),
        in_specs=[
            pl.BlockSpec((r, d), lambda l, j: (0, 0)),
            pl.BlockSpec((1, d, tn), lambda l, j: (l, 0, j)),
            pl.BlockSpec((1, 1, tn), lambda l, j: (l, 0, j)),
        ],
        out_specs=pl.BlockSpec((1, r, tn), lambda l, j: (l, 0, j)),
        out_shape=jax.ShapeDtypeStruct((depth, r, n), F32),
        compiler_params=_params(("arbitrary", "arbitrary")),
        name="adaln",
    )(cc, w_ada, b_ada.reshape(depth, 1, n))


def _rope(x, cos, sina, sinb, shift):
    n = x.shape[-1]
    reps = n // LANES
    if reps > 1:
        cos = jnp.concatenate([cos] * reps, axis=-1)
        sina = jnp.concatenate([sina] * reps, axis=-1)
        sinb = jnp.concatenate([sinb] * reps, axis=-1)
    return x * cos + pltpu.roll(x, n - shift, 1) * sina + pltpu.roll(x, shift, 1) * sinb


def _inproj_kernel(*refs, n_src, n_ctx_tiles):
    (sh_ref, sc_ref, g1_ref, win_ref, gcq_ref, wuq_ref, gckv_ref, wukv_ref,
     mcos_ref, msa_ref, msb_ref, scos_ref, ssa_ref, ssb_ref,
     qm_ref, km_ref, vm_ref, sq_ref, sk_ref, sv_ref, lx_ref, lg_ref) = refs[n_src:]
    x = _read_stream(refs[:n_src], n_ctx_tiles)
    d = x.shape[-1]
    h = _rms(x, d) * g1_ref[...] * (1.0 + sc_ref[0]) + sh_ref[0]
    z = _dot(h.astype(BF16), win_ref[...])

    mcos, msa, msb = mcos_ref[...], msa_ref[...], msb_ref[...]
    scos, ssa, ssb = scos_ref[...], ssa_ref[...], ssb_ref[...]

    cq = _rms(z[:, 0:256], MLA_Q_RANK) * gcq_ref[...]
    q = _dot(cq.astype(BF16), wuq_ref[...])
    q = _rope(q, mcos, msa, msb, MLA_ROPE // 4)
    qm_ref[0] = (q * (MLA_SCALE * LOG2E)).astype(BF16)

    ckv = _rms(z[:, 256:384], MLA_KV_RANK) * gckv_ref[...]
    kv = _dot(ckv.astype(BF16), wukv_ref[...])
    kr = _rope(z[:, 384:512], mcos, msa, msb, MLA_ROPE // 4)
    km_ref[0] = (kv[:, 0:512] + jnp.concatenate([kr] * MLA_HEADS, axis=-1)).astype(BF16)
    ones = jnp.ones((kv.shape[0], LANES), F32)
    vm_ref[0] = jnp.concatenate([kv[:, 512:640], ones, kv[:, 640:768], ones], axis=-1).astype(BF16)

    sq_ref[0] = (_rope(z[:, 512:768], scos, ssa, ssb, SWA_HEAD_DIM // 4) * (SWA_SCALE * LOG2E)).astype(BF16)
    sk = _rope(z[:, 768:896], scos, ssa, ssb, SWA_HEAD_DIM // 4)
    sv = z[:, 896:1024]
    lane = lax.broadcasted_iota(jnp.int32, sk.shape, 1)
    low = lane < SWA_HEAD_DIM
    k0 = jnp.where(low, sk, 0.0)
    k1 = jnp.where(low, 0.0, sk)
    sk_ref[0] = jnp.concatenate([k0 + pltpu.roll(k0, SWA_HEAD_DIM, 1), k1 + pltpu.roll(k1, SWA_HEAD_DIM, 1)],
                                axis=-1).astype(BF16)
    v0 = jnp.where(low, sv, 0.0)
    v1 = jnp.where(low, 0.0, sv)
    one_hi = jnp.where(low, 0.0, 1.0)
    one_lo = jnp.where(low, 1.0, 0.0)
    sv_ref[0] = jnp.concatenate([v0 + one_hi, pltpu.roll(v0, SWA_HEAD_DIM, 1) + one_lo,
                                 pltpu.roll(v1, SWA_HEAD_DIM, 1) + one_hi, v1 + one_lo], axis=-1).astype(BF16)
    lx_ref[0] = z[:, 1024:1536].astype(BF16)
    lg_ref[0] = z[:, 1536:2048].astype(BF16)


def _mod_spec(j, b_rows, n_ctx_tiles, d, skip=0):
    return pl.BlockSpec((1, 1, d), lambda i, b: (jnp.where(i + skip < n_ctx_tiles, b_rows, b), 0, j))


def _stream_specs(src, n_ctx_tiles, d, skip=0):
    if not isinstance(src, tuple):
        return [pl.BlockSpec((1, TM, d), lambda i, b: (b, i + skip, 0))], [src]
    ctx, lat = src
    lat_spec = pl.BlockSpec(
        (1, TM, d), lambda i, b: (jnp.where(i + skip < n_ctx_tiles, 0, b), jnp.maximum(i + skip - n_ctx_tiles, 0), 0))
    if skip >= n_ctx_tiles:
        return [lat_spec], [lat]
    ctx_spec = pl.BlockSpec(
        (1, TM, d), lambda i, b: (jnp.where(i + skip < n_ctx_tiles, b, 0), jnp.minimum(i + skip, n_ctx_tiles - 1), 0))
    return [ctx_spec, lat_spec], [ctx, lat]


def _read_stream(refs, n_ctx_tiles, skip=0):
    if len(refs) == 1:
        return refs[0][0]
    return jnp.where(pl.program_id(0) + skip < n_ctx_tiles, refs[0][0], refs[1][0])


def _inproj_call(src, mod, g1, win, gcq, wuq, gckv, wukv, tabs, n_ctx_tiles):
    bsz, _, d = src[1].shape if isinstance(src, tuple) else src.shape
    j = tabs[0].shape[0]
    nt = j // TM
    x_specs, x_arrays = _stream_specs(src, n_ctx_tiles, d)
    tok = lambda w: pl.BlockSpec((1, TM, w), lambda i, b: (b, i, 0))
    tab = pl.BlockSpec((TM, LANES), lambda i, b: (i, 0))
    widths = (512, 512, 512, 256, 256, 512, 512, 512)
    return pl.pallas_call(
        functools.partial(_inproj_kernel, n_src=len(x_arrays), n_ctx_tiles=n_ctx_tiles),
        grid=(nt, bsz),
        in_specs=x_specs + [
            _mod_spec(0, bsz, n_ctx_tiles, d),
            _mod_spec(1, bsz, n_ctx_tiles, d),
            _full(g1.shape), _full(win.shape), _full(gcq.shape), _full(wuq.shape),
            _full(gckv.shape), _full(wukv.shape),
            tab, tab, tab, tab, tab, tab,
        ],
        out_specs=[tok(w) for w in widths],
        out_shape=[jax.ShapeDtypeStruct((bsz, j, w), BF16) for w in widths],
        compiler_params=_params(("arbitrary", "arbitrary")),
        name="inproj",
    )(*x_arrays, mod, mod, g1, win, gcq, wuq, gckv, wukv, *tabs)


def _mla_heads(q_ref, k_ref, v_ref, o_ref, klen):
    lane = lax.broadcasted_iota(jnp.int32, (q_ref.shape[1], LANES), 1)
    for hp in range(MLA_HEADS // 2):
        v = v_ref[0, 0:klen, 2 * hp * LANES:2 * (hp + 1) * LANES]
        outs = []
        for h in range(2 * hp, 2 * hp + 2):
            q = q_ref[0, :, h * LANES:(h + 1) * LANES]
            k = k_ref[0, 0:klen, h * LANES:(h + 1) * LANES]
            s = _dot_nt(q, k)
            m = jnp.max(s, axis=-1, keepdims=True)
            o = _dot(jnp.exp2(s - m).astype(BF16), v)
            outs.append(o[:, 0:LANES] / o[:, LANES:LANES + 1])
        o_ref[0, :, hp * LANES:(hp + 1) * LANES] = jnp.where(lane < MLA_V, outs[0], outs[1]).astype(o_ref.dtype)


def _mla_kernel(q_ref, k_ref, v_ref, o_ref, *, n_ctx_tiles, ctx_len):
    i = pl.program_id(1)

    @pl.when(i < n_ctx_tiles)
    def _():
        _mla_heads(q_ref, k_ref, v_ref, o_ref, ctx_len)

    @pl.when(i >= n_ctx_tiles)
    def _():
        _mla_heads(q_ref, k_ref, v_ref, o_ref, k_ref.shape[1])


def _mla_call(qm, km, vm, n_ctx_tiles, ctx_len):
    bsz, j, w = qm.shape
    nt = j // TM
    return pl.pallas_call(
        functools.partial(_mla_kernel, n_ctx_tiles=n_ctx_tiles, ctx_len=ctx_len),
        grid=(bsz, nt),
        in_specs=[
            pl.BlockSpec((1, TM, w), lambda b, i: (b, i, 0)),
            pl.BlockSpec((1, j, w), lambda b, i: (b, 0, 0)),
            pl.BlockSpec((1, j, w), lambda b, i: (b, 0, 0)),
        ],
        out_specs=pl.BlockSpec((1, TM, MLA_OUT), lambda b, i: (b, i, 0)),
        out_shape=jax.ShapeDtypeStruct((bsz, j, MLA_OUT), BF16),
        compiler_params=_params(("arbitrary", "arbitrary")),
        name="mla_attn",
    )(qm, km, vm)


def _swa_kernel(sink_ref, q_ref, k_ref, v_ref, o_ref, *, ctx_len):
    i = pl.program_id(1)
    j = k_ref.shape[1]
    span = TM + 2 * WINDOW
    row0 = pl.multiple_of(jnp.clip(i * TM - WINDOW, 0, j - span), WINDOW)
    qpos = i * TM - ctx_len + lax.broadcasted_iota(jnp.int32, (TM, span), 0)
    kpos = row0 - ctx_len + lax.broadcasted_iota(jnp.int32, (TM, span), 1)
    valid = (jnp.abs(qpos - kpos) <= WINDOW) & (kpos >= 0) & (qpos >= 0)
    low = lax.broadcasted_iota(jnp.int32, (TM, LANES), 1) < SWA_HEAD_DIM
    group = SWA_HEADS // SWA_KV_HEADS
    for kvh in range(SWA_KV_HEADS):
        kcols = slice(kvh * LANES, (kvh + 1) * LANES)
        qpair = q_ref[0, :, kcols]
        kl = k_ref[0, pl.ds(row0, span), kcols]
        kc = k_ref[0, 0:ctx_len, kcols]
        halves = []
        for g in range(group):
            vcols = slice((kvh * group + g) * LANES, (kvh * group + g + 1) * LANES)
            q = jnp.where(low if g == 0 else ~low, qpair, jnp.zeros_like(qpair))
            s_loc = jnp.where(valid, _dot_nt(q, kl), NEG_INF)
            s_ctx = _dot_nt(q, kc)
            sink = sink_ref[kvh * group + g] * LOG2E
            m = jnp.maximum(jnp.maximum(jnp.max(s_loc, axis=-1, keepdims=True),
                                        jnp.max(s_ctx, axis=-1, keepdims=True)), sink)
            o = (_dot(jnp.exp2(s_loc - m).astype(BF16), v_ref[0, pl.ds(row0, span), vcols])
                 + _dot(jnp.exp2(s_ctx - m).astype(BF16), v_ref[0, 0:ctx_len, vcols]))
            den = o[:, SWA_HEAD_DIM:SWA_HEAD_DIM + 1] if g == 0 else o[:, 0:1]
            halves.append(o / (den + jnp.exp2(sink - m)))
        o_ref[0, :, kcols] = jnp.where(low, halves[0], halves[1]).astype(o_ref.dtype)


def _swa_call(sink, sq, sk, sv, ctx_len):
    bsz, j, _ = sq.shape
    nt = j // TM
    return pl.pallas_call(
        functools.partial(_swa_kernel, ctx_len=ctx_len),
        grid=(bsz, nt),
        in_specs=[
            pl.BlockSpec(memory_space=pltpu.SMEM),
            pl.BlockSpec((1, TM, SWA_OUT), lambda b, i: (b, i, 0)),
            pl.BlockSpec((1, j, sk.shape[-1]), lambda b, i: (b, 0, 0)),
            pl.BlockSpec((1, j, sv.shape[-1]), lambda b, i: (b, 0, 0)),
        ],
        out_specs=pl.BlockSpec((1, TM, SWA_OUT), lambda b, i: (b, i, 0)),
        out_shape=jax.ShapeDtypeStruct((bsz, j, SWA_OUT), BF16),
        compiler_params=_params(("arbitrary", "arbitrary")),
        name="swa_attn",
    )(sink, sq, sk, sv)


def _lru_gates(prev_ref, cur_ref, next_ref, cw_ref, cb_ref, w_ref, b_ref, sp_ref, seg_start, seg_end):
    z = cur_ref[0].astype(F32)
    row = lax.broadcasted_iota(jnp.int32, z.shape, 0)
    keep_prev = jnp.where(seg_start, 0.0, 1.0)
    keep_next = jnp.where(seg_end, 0.0, 1.0)
    p2 = prev_ref[0, TM - 2:TM - 1, :].astype(F32) * keep_prev
    p1 = prev_ref[0, TM - 1:TM, :].astype(F32) * keep_prev
    n0 = next_ref[0, 0:1, :].astype(F32) * keep_next
    z_m1 = jnp.where(row == 0, p1, pltpu.roll(z, 1, 0))
    z_m2 = jnp.where(row == 0, p2, jnp.where(row == 1, p1, pltpu.roll(z, 2, 0)))
    z_p1 = jnp.where(row == TM - 1, n0, pltpu.roll(z, TM - 1, 0))
    u = cb_ref[...] + z_m2 * cw_ref[0:1, :]
    u = u + z_m1 * cw_ref[1:2, :]
    u = u + z * cw_ref[2:3, :]
    u = u + z_p1 * cw_ref[3:4, :]
    g = _dot(u.astype(BF16), w_ref[...]) + b_ref[...]
    r = _sigmoid(g[:, 0:LRU_WIDTH])
    ig = _sigmoid(g[:, LRU_WIDTH:2 * LRU_WIDTH])
    log_a = (-LRU_C) * r * sp_ref[...]
    a = jnp.exp(log_a)
    om = 1.0 - a * a
    bt = jnp.where(om > 0.0, om * lax.rsqrt(om), 0.0) * (ig * u)
    return a, bt


def _lru_scan(a, b, h0, reverse):
    t, w = a.shape
    g = t // SUBLANES
    a = a.reshape(g, SUBLANES, w)
    b = b.reshape(g, SUBLANES, w)
    sub = lax.broadcasted_iota(jnp.int32, (g, SUBLANES, w), 1)
    d = 1
    while d < SUBLANES:
        if reverse:
            shift, ok = SUBLANES - d, sub < SUBLANES - d
        else:
            shift, ok = d, sub >= d
        a_sh = pltpu.roll(a, shift, 1)
        b_sh = pltpu.roll(b, shift, 1)
        b = jnp.where(ok, a * b_sh + b, b)
        a = jnp.where(ok, a * a_sh, a)
        d *= 2
    hs = [None] * g
    h = h0
    order = range(g - 1, -1, -1) if reverse else range(g)
    for gi in order:
        hg = a[gi] * h + b[gi]
        hs[gi] = hg
        h = hg[0:1, :] if reverse else hg[SUBLANES - 1:SUBLANES, :]
    return jnp.concatenate(hs, axis=0), h


def _softplus_neg(lam):
    x = -lam
    return jnp.maximum(x, 0.0) + jnp.log1p(jnp.exp(-jnp.abs(x)))


def _lru_fwd_kernel(prev_ref, cur_ref, next_ref, cw_ref, cb_ref, w_ref, b_ref, lam_ref,
                    hf_ref, carry_ref, sp_ref, *, n_ctx_tiles):
    c = pl.program_id(1)
    nt = pl.num_programs(1)

    @pl.when(c == 0)
    def _():
        carry_ref[...] = jnp.zeros_like(carry_ref)
        sp_ref[...] = _softplus_neg(lam_ref[...])

    seg_start = (c == 0) | (c == n_ctx_tiles)
    seg_end = (c == n_ctx_tiles - 1) | (c == nt - 1)
    a, bt = _lru_gates(prev_ref, cur_ref, next_ref, cw_ref, cb_ref, w_ref, b_ref, sp_ref,
                       seg_start, seg_end)
    hs, h = _lru_scan(a, bt, carry_ref[...], False)
    carry_ref[...] = h
    hf_ref[0] = hs.astype(hf_ref.dtype)


def _bwd_chunk(s, nt, n_ctx_tiles):
    return jnp.where(s < n_ctx_tiles, n_ctx_tiles - 1 - s, nt - 1 - (s - n_ctx_tiles))


def _lru_bwd_kernel(prev_ref, cur_ref, next_ref, cw_ref, cb_ref, w_ref, b_ref, lam_ref,
                    hf_ref, lg_ref, o_ref, carry_ref, sp_ref, *, n_ctx_tiles):
    s = pl.program_id(1)
    nt = pl.num_programs(1)
    c = _bwd_chunk(s, nt, n_ctx_tiles)

    @pl.when(s == 0)
    def _():
        carry_ref[...] = jnp.zeros_like(carry_ref)
        sp_ref[...] = _softplus_neg(lam_ref[...])

    seg_start = (c == 0) | (c == n_ctx_tiles)
    seg_end = (c == n_ctx_tiles - 1) | (c == nt - 1)
    a, bt = _lru_gates(prev_ref, cur_ref, next_ref, cw_ref, cb_ref, w_ref, b_ref, sp_ref,
                       seg_start, seg_end)
    hs, h = _lru_scan(a, bt, carry_ref[...], True)
    carry_ref[...] = h
    gate = jax.nn.gelu(lg_ref[0].astype(F32), approximate=True)
    o_ref[0] = ((hf_ref[0].astype(F32) + hs) * gate).astype(o_ref.dtype)


def _lru_call(lx, lg, cw, cb, wf, bf, wb, bb, lam, n_ctx_tiles):
    bsz, j, w = lx.shape
    nt = j // TM
    consts = [_full(cw.shape), _full((1, w)), _full(wf.shape), _full(bf.shape), _full((1, w))]
    scratch = [pltpu.VMEM((1, w), F32), pltpu.VMEM((1, w), F32)]
    blk = lambda f: pl.BlockSpec((1, TM, w), f)
    hf = pl.pallas_call(
        functools.partial(_lru_fwd_kernel, n_ctx_tiles=n_ctx_tiles),
        grid=(bsz, nt),
        in_specs=[
            blk(lambda b, c: (b, jnp.maximum(c - 1, 0), 0)),
            blk(lambda b, c: (b, c, 0)),
            blk(lambda b, c: (b, jnp.minimum(c + 1, nt - 1), 0)),
        ] + consts,
        out_specs=blk(lambda b, c: (b, c, 0)),
        out_shape=jax.ShapeDtypeStruct((bsz, j, w), BF16),
        scratch_shapes=scratch,
        compiler_params=_params(("arbitrary", "arbitrary")),
        name="lru_fwd",
    )(lx, lx, lx, cw, cb, wf, bf, lam[0:1])
    cix = lambda s: _bwd_chunk(s, nt, n_ctx_tiles)
    return pl.pallas_call(
        functools.partial(_lru_bwd_kernel, n_ctx_tiles=n_ctx_tiles),
        grid=(bsz, nt),
        in_specs=[
            blk(lambda b, s: (b, jnp.maximum(cix(s) - 1, 0), 0)),
            blk(lambda b, s: (b, cix(s), 0)),
            blk(lambda b, s: (b, jnp.minimum(cix(s) + 1, nt - 1), 0)),
        ] + consts + [
            blk(lambda b, s: (b, cix(s), 0)),
            blk(lambda b, s: (b, cix(s), 0)),
        ],
        out_specs=blk(lambda b, s: (b, cix(s), 0)),
        out_shape=jax.ShapeDtypeStruct((bsz, j, w), BF16),
        scratch_shapes=scratch,
        compiler_params=_params(("arbitrary", "arbitrary")),
        name="lru_bwd",
    )(lx, lx, lx, cw, cb, wb, bb, lam[1:2], hf, lg)


def _route(logits, carry):
    lt = logits.T
    tm = lt.shape[1]
    big = jnp.int32(LANES)
    lg = lt[N_EXPERTS:N_EXPERTS + SUBLANES]
    rg = lax.broadcasted_iota(jnp.int32, lg.shape, 0)
    is_g = rg < N_GROUPS
    mg = jnp.max(jnp.where(is_g, lg, -jnp.inf), axis=0, keepdims=True)
    eg = jnp.where(is_g, jnp.exp(lg - mg), 0.0)
    pg = eg / jnp.sum(eg, axis=0, keepdims=True)
    pg_top = jnp.max(pg, axis=0, keepdims=True)
    g_idx = jnp.min(jnp.where(is_g & (pg == pg_top), rg, big), axis=0, keepdims=True)
    le = lt[0:N_EXPERTS]
    re = lax.broadcasted_iota(jnp.int32, le.shape, 0)
    lo = g_idx * EXPERTS_PER_GROUP
    sel = (re >= lo) & (re < lo + EXPERTS_PER_GROUP)
    me = jnp.max(jnp.where(sel, le, -jnp.inf), axis=0, keepdims=True)
    ee = jnp.where(sel, jnp.exp(le - me), 0.0)
    pe = ee / jnp.sum(ee, axis=0, keepdims=True)
    p1 = jnp.max(jnp.where(sel, pe, -1.0), axis=0, keepdims=True)
    i1 = jnp.min(jnp.where(sel & (pe == p1), re, big), axis=0, keepdims=True)
    sel2 = sel & (re != i1)
    p2 = jnp.max(jnp.where(sel2, pe, -1.0), axis=0, keepdims=True)
    i2 = jnp.min(jnp.where(sel2 & (pe == p2), re, big), axis=0, keepdims=True)
    den = p1 + p2
    w1 = pg_top * p1 / den
    w2 = pg_top * p2 / den
    first_lo = i1 < i2
    ia = (jnp.where(first_lo, i1, i2) - lo).astype(F32)
    ib = (jnp.where(first_lo, i2, i1) - lo).astype(F32)
    wa = jnp.where(first_lo, w1, w2)
    wb = jnp.where(first_lo, w2, w1)
    pair = ia * (2 * EXPERTS_PER_GROUP - 1 - ia) * 0.5 + (ib - ia - 1.0)
    bucket = (g_idx.astype(F32) * N_PAIRS + pair).astype(jnp.int32)
    rb = lax.broadcasted_iota(jnp.int32, (LANES, tm), 0)
    mine = rb == bucket
    picks = jnp.where(mine, 1.0, 0.0)
    tri = jnp.where(lax.broadcasted_iota(jnp.int32, (tm, tm), 0) < lax.broadcasted_iota(jnp.int32, (tm, tm), 1),
                    1.0, 0.0).astype(BF16)
    before = _dot(picks.astype(BF16), tri) + carry
    rank = jnp.sum(jnp.where(mine, before, 0.0), axis=0, keepdims=True)
    zeros = jnp.zeros((SUBLANES - 2, tm), F32)
    meta = jnp.concatenate([bucket.astype(F32), rank, zeros], axis=0).astype(jnp.int32)
    wrows = jnp.concatenate([wa, wb, jnp.zeros((LANES - 2, tm), F32)], axis=0)
    return meta, wrows.T, carry + jnp.sum(picks, axis=1, keepdims=True)


def _outproj_kernel(*refs, n_src, n_ctx_tiles, skip):
    (om_ref, os_ref, ol_ref, gt1_ref, sh2_ref, sc2_ref, gg_ref, wout_ref,
     g2_ref, wr_ref, br_ref, x1_ref, h2_ref, meta_ref, cnt_ref, carry_ref) = refs[n_src:]
    x = _read_stream(refs[:n_src], n_ctx_tiles, skip)
    d = x.shape[-1]

    @pl.when((pl.program_id(0) == 0) & (pl.program_id(1) == 0))
    def _():
        carry_ref[...] = jnp.zeros_like(carry_ref)

    gg = gg_ref[...]
    nm = _rms(om_ref[0].astype(F32), MLA_OUT) * gg[:, 0:MLA_OUT]
    ns = _rms(os_ref[0].astype(F32), SWA_OUT) * gg[:, MLA_OUT:MLA_OUT + SWA_OUT]
    nl = _rms(ol_ref[0].astype(F32), LRU_WIDTH) * gg[:, MLA_OUT + SWA_OUT:]
    merged = jnp.concatenate([nm, ns, nl], axis=-1).astype(BF16)
    mix = _dot(merged, wout_ref[...])
    x1 = x + gt1_ref[0] * mix
    x1_ref[0] = x1
    h2 = (_rms(x1, d) * g2_ref[...] * (1.0 + sc2_ref[0]) + sh2_ref[0]).astype(BF16)
    meta, wcols, carry = _route(_dot(h2, wr_ref[...]) + br_ref[...], carry_ref[...])
    _store_slabs(h2_ref, jnp.concatenate([_pack_pairs(h2), pltpu.bitcast(wcols, jnp.uint32)], axis=-1))
    carry_ref[...] = carry
    cnt_ref[...] = jnp.broadcast_to(carry, (LANES, LANES)).T[0:1, :].astype(jnp.int32)
    for m in range(TM // LANES):
        meta_ref[m] = meta[:, m * LANES:(m + 1) * LANES]


def _outproj_call(om, osw, ol, src, mod, gg, wout, g2, wr, br, n_ctx_tiles, latent_only):
    bsz, j_all, _ = om.shape
    d = wout.shape[1]
    skip = n_ctx_tiles if latent_only else 0
    nt = j_all // TM - skip
    j = nt * TM
    tok_in = lambda w: pl.BlockSpec((1, TM, w), lambda i, b: (b, i + skip, 0))
    tok = lambda w: pl.BlockSpec((1, TM, w), lambda i, b: (b, i, 0))
    x_specs, x_arrays = _stream_specs(src, n_ctx_tiles, d, skip)
    return pl.pallas_call(
        functools.partial(_outproj_kernel, n_src=len(x_arrays), n_ctx_tiles=n_ctx_tiles, skip=skip),
        grid=(nt, bsz),
        in_specs=x_specs + [
            tok_in(MLA_OUT), tok_in(SWA_OUT), tok_in(LRU_WIDTH),
            _mod_spec(2, bsz, n_ctx_tiles, d, skip),
            _mod_spec(3, bsz, n_ctx_tiles, d, skip),
            _mod_spec(4, bsz, n_ctx_tiles, d, skip),
            _full(gg.shape), _full(wout.shape), _full(g2.shape), _full(wr.shape), _full(br.shape),
        ],
        out_specs=[
            tok(d), pl.BlockSpec((TM * SUBLANES, LANES), lambda i, b: (b * nt + i, 0)),
            pl.BlockSpec((TM // LANES, SUBLANES, LANES), lambda i, b: (b * nt + i, 0, 0)),
            pl.BlockSpec((1, LANES), lambda i, b: (0, 0)),
        ],
        out_shape=[
            jax.ShapeDtypeStruct((bsz, j, d), F32),
            jax.ShapeDtypeStruct((bsz * j * SUBLANES, LANES), jnp.uint32),
            jax.ShapeDtypeStruct((bsz * j // LANES, SUBLANES, LANES), jnp.int32),
            jax.ShapeDtypeStruct((1, LANES), jnp.int32),
        ],
        scratch_shapes=[pltpu.VMEM((LANES, 1), F32)],
        compiler_params=_params(("arbitrary", "arbitrary")),
        name="outproj_route",
    )(*x_arrays, om, osw, ol, mod, mod, mod, gg, wout, g2, wr, br)


def _padded_tiles(count):
    return lax.shift_right_logical(count + (MOE_TG - 1), MOE_TG.bit_length() - 1)


def _plan_kernel(cnt_ref, off_ref, tea_ref, teb_ref, nu_ref, *, n_tiles_max):
    off = jnp.int32(0)
    ti = jnp.int32(0)
    bucket = 0
    for g in range(N_GROUPS):
        for a in range(EXPERTS_PER_GROUP):
            for b in range(a + 1, EXPERTS_PER_GROUP):
                n_q = _padded_tiles(cnt_ref[0, bucket])
                off_ref[bucket] = off

                def fill(k, _, base=ti, ea=g * EXPERTS_PER_GROUP + a, eb=g * EXPERTS_PER_GROUP + b):
                    tea_ref[base + k] = ea
                    teb_ref[base + k] = eb
                    return 0

                lax.fori_loop(0, n_q, fill, 0)
                off = off + n_q * MOE_TG
                ti = ti + n_q
                bucket += 1
    nu_ref[0] = ti

    def rest(k, _):
        tea_ref[k] = N_EXPERTS - 2
        teb_ref[k] = N_EXPERTS - 1
        return 0

    lax.fori_loop(ti, n_tiles_max, rest, 0)


def _plan_call(cnt, n_tiles_max):
    smem = pl.BlockSpec(memory_space=pltpu.SMEM)
    return pl.pallas_call(
        functools.partial(_plan_kernel, n_tiles_max=n_tiles_max),
        in_specs=[smem],
        out_specs=[smem, smem, smem, smem],
        out_shape=[
            jax.ShapeDtypeStruct((N_BUCKETS,), jnp.int32),
            jax.ShapeDtypeStruct((n_tiles_max,), jnp.int32),
            jax.ShapeDtypeStruct((n_tiles_max,), jnp.int32),
            jax.ShapeDtypeStruct((1,), jnp.int32),
        ],
        name="moe_plan",
    )(cnt)


def _row_copy(src_ref, src_row, dst_ref, dst_row, sem, used=SUBLANES):
    tile_of = lambda row: pl.ds(pl.multiple_of(row * SUBLANES, SUBLANES), used)
    return pltpu.make_async_copy(src_ref.at[tile_of(src_row)], dst_ref.at[tile_of(dst_row)], sem)


def _dispatch_kernel(off_ref, nu_ref, h_ref, meta_ref, cnt_ref, xs_ref, zero_ref, sem, tile_sem, *, tile, used):
    tile_rows = MOE_TG * SUBLANES
    n_tiles_max = xs_ref.shape[0] // tile_rows

    @pl.when(pl.program_id(0) == 0)
    def _():
        zero_ref[...] = jnp.zeros_like(zero_ref)

        def per_bucket(q, n):
            c = cnt_ref[0, q]
            padded = _padded_tiles(c) * MOE_TG
            base = off_ref[q]

            def fill(r, _):
                _row_copy(zero_ref, 0, xs_ref, base + r, sem, used).start()
                return 0

            lax.fori_loop(c, padded, fill, 0)
            return n + (padded - c)

        n_pad = lax.fori_loop(0, N_BUCKETS, per_bucket, jnp.int32(0))

        def drain(r, _):
            _row_copy(zero_ref, 0, xs_ref, 0, sem, used).wait()
            return 0

        lax.fori_loop(0, n_pad, drain, 0)

        def unused_tile(ti, _):
            cp = pltpu.make_async_copy(
                zero_ref, xs_ref.at[pl.ds(pl.multiple_of(ti * tile_rows, tile_rows), tile_rows)], tile_sem)
            cp.start()
            cp.wait()
            return 0

        lax.fori_loop(nu_ref[0], n_tiles_max, unused_tile, 0)

    for m in range(tile // LANES):
        for r in range(LANES):
            dst = off_ref[meta_ref[m, 0, r]] + meta_ref[m, 1, r]
            _row_copy(h_ref, m * LANES + r, xs_ref, dst, sem, used).start(priority=r % 2)

    pltpu.make_async_copy(h_ref.at[pl.ds(0, tile * used)], xs_ref.at[pl.ds(0, tile * used)], sem).wait()


def _dispatch_call(off, nu, h2, meta, cnt, rows_max, used):
    t = h2.shape[0] // SUBLANES
    tile = MOE_TD if t % MOE_TD == 0 else TM
    return pl.pallas_call(
        functools.partial(_dispatch_kernel, tile=tile, used=used),
        grid_spec=pltpu.PrefetchScalarGridSpec(
            num_scalar_prefetch=2,
            grid=(t // tile,),
            in_specs=[
                pl.BlockSpec((tile * SUBLANES, LANES), lambda i, off, nu: (i, 0)),
                pl.BlockSpec((tile // LANES, SUBLANES, LANES), lambda i, off, nu: (i, 0, 0),
                             memory_space=pltpu.SMEM),
                pl.BlockSpec(memory_space=pltpu.SMEM),
            ],
            out_specs=pl.BlockSpec(memory_space=pl.ANY),
            scratch_shapes=[pltpu.VMEM((MOE_TG * SUBLANES, LANES), h2.dtype), pltpu.SemaphoreType.DMA,
                            pltpu.SemaphoreType.DMA],
        ),
        out_shape=jax.ShapeDtypeStruct((rows_max * SUBLANES, LANES), h2.dtype),
        compiler_params=_params(("arbitrary",)),
        name="moe_dispatch",
    )(off, nu, h2, meta, cnt)


def _expert_mlp(x, wg_ref, wu_ref, wd_ref):
    gate = _dot(x, wg_ref[0])
    act = gate * _sigmoid(gate) * _dot(x, wu_ref[0])
    return _dot(act.astype(BF16), wd_ref[0])


def _expert_kernel(tea_ref, teb_ref, nu_ref, x_ref, wga_ref, wua_ref, wda_ref, wgb_ref, wub_ref, wdb_ref, y_ref):
    groups = wga_ref.shape[1] // (2 * LANES)

    @pl.when(pl.program_id(0) < nu_ref[0])
    def _():
        row = _load_slabs(x_ref, MOE_TG, groups + 1)
        x = _unpack_pairs(row[:, 0:groups * LANES])
        wts = pltpu.bitcast(row[:, groups * LANES:], F32)
        y = (wts[:, 0:1] * _expert_mlp(x, wga_ref, wua_ref, wda_ref)
             + wts[:, 1:2] * _expert_mlp(x, wgb_ref, wub_ref, wdb_ref))
        _store_slabs(y_ref, _pack_pairs(y.astype(BF16)))

    @pl.when(pl.program_id(0) >= nu_ref[0])
    def _():
        y_ref[...] = jnp.zeros_like(y_ref)


def _expert_call(tea, teb, nu, xs, wg, wu, wd):
    d = wg.shape[1]
    n_tiles = xs.shape[0] // (MOE_TG * SUBLANES)
    used = lambda i, nu: jnp.minimum(i, nu[0] - 1)
    wspec = lambda shape, te_pos: pl.BlockSpec(
        shape, lambda i, tea, teb, nu: ((tea, teb)[te_pos][used(i, nu)], 0, 0))
    return pl.pallas_call(
        _expert_kernel,
        grid_spec=pltpu.PrefetchScalarGridSpec(
            num_scalar_prefetch=3,
            grid=(n_tiles,),
            in_specs=[
                pl.BlockSpec((MOE_TG * SUBLANES, LANES), lambda i, tea, teb, nu: (used(i, nu), 0)),
                wspec((1, d, D_EXPERT), 0), wspec((1, d, D_EXPERT), 0), wspec((1, D_EXPERT, d), 0),
                wspec((1, d, D_EXPERT), 1), wspec((1, d, D_EXPERT), 1), wspec((1, D_EXPERT, d), 1),
            ],
            out_specs=pl.BlockSpec((MOE_TG * SUBLANES, LANES), lambda i, tea, teb, nu: (i, 0)),
        ),
        out_shape=jax.ShapeDtypeStruct(xs.shape, jnp.uint32),
        compiler_params=_params(("arbitrary",)),
        name="moe_experts",
    )(tea, teb, nu, xs, wg, wu, wd, wg, wu, wd)


def _combine_kernel(off_ref, meta_ref, ys_ref, x_ref, gt2_ref, gf_ref, o_ref, y_ref, sem, *, final, n_ctx_tiles,
                    tiles_per_batch):
    def body():
        groups = x_ref.shape[-1] // (2 * LANES)
        for m in range(TM // LANES):
            for r in range(LANES):
                src = off_ref[meta_ref[m, 0, r]] + meta_ref[m, 1, r]
                _row_copy(ys_ref, src, y_ref, m * LANES + r, sem, groups).start(priority=r % 2)

        pltpu.make_async_copy(ys_ref.at[pl.ds(0, TM * groups)], y_ref.at[pl.ds(0, TM * groups)], sem).wait()
        x2 = x_ref[...] + gt2_ref[0] * _unpack_pairs(_load_slabs(y_ref, TM, groups)).astype(F32)
        if final:
            x2 = _rms(x2, x2.shape[-1]) * gf_ref[...]
        o_ref[...] = x2

    if final:
        pl.when(pl.program_id(0) % tiles_per_batch >= n_ctx_tiles)(body)
    else:
        body()


def _combine_call(off, meta, ys, x1, mod, gf, tiles_per_batch, n_ctx_tiles, bsz, final):
    t, d = x1.shape
    lat = tiles_per_batch - n_ctx_tiles
    mod_row = lambda i, off: (jnp.where(i % tiles_per_batch < n_ctx_tiles, bsz, i // tiles_per_batch), 0, 5)
    if final:
        out_rows = bsz * lat * TM
        out_ix = lambda i, off: ((i // tiles_per_batch) * lat + jnp.maximum(i % tiles_per_batch - n_ctx_tiles, 0), 0)
    else:
        out_rows = t
        out_ix = lambda i, off: (i, 0)
    return pl.pallas_call(
        functools.partial(_combine_kernel, final=final, n_ctx_tiles=n_ctx_tiles, tiles_per_batch=tiles_per_batch),
        grid_spec=pltpu.PrefetchScalarGridSpec(
            num_scalar_prefetch=1,
            grid=(t // TM,),
            in_specs=[
                pl.BlockSpec((TM // LANES, SUBLANES, LANES), lambda i, off: (i, 0, 0), memory_space=pltpu.SMEM),
                pl.BlockSpec(memory_space=pl.ANY),
                pl.BlockSpec((TM, d), lambda i, off: (i, 0)),
                pl.BlockSpec((1, 1, d), mod_row),
                pl.BlockSpec((1, d), lambda i, off: (0, 0)),
            ],
            out_specs=pl.BlockSpec((TM, d), out_ix),
            scratch_shapes=[pltpu.VMEM((TM * SUBLANES, LANES), jnp.uint32), pltpu.SemaphoreType.DMA],
        ),
        out_shape=jax.ShapeDtypeStruct((out_rows, d), F32),
        compiler_params=_params(("arbitrary",)),
        name="moe_combine",
    )(off, meta, ys, x1, mod, gf)


def _moe_call(h2, meta, cnt, x1, mod, gf, wg, wu, wd, n_ctx_tiles, final):
    bsz, j, d = x1.shape
    t = bsz * j
    rows_max = -(-t // MOE_TG) * MOE_TG + N_BUCKETS * MOE_TG
    off, tea, teb, nu = _plan_call(cnt, rows_max // MOE_TG)
    xs = _dispatch_call(off, nu, h2, meta, cnt, rows_max, d // (2 * LANES) + 1)
    ys = _expert_call(tea, teb, nu, xs, wg, wu, wd)
    out = _combine_call(off, meta, ys, x1.reshape(t, d), mod, gf, j // TM, n_ctx_tiles, bsz, final)
    return out.reshape(bsz, -1, d)


def _rope_tables(ctx_len, seq, groups, shift):
    fidx = [0] * LANES
    cols = [False] * LANES
    role = [0] * LANES
    for start, use_cols in groups:
        for k in range(shift):
            for half in range(2):
                lane = start + half * shift + k
                fidx[lane], cols[lane], role[lane] = k, use_cols, half + 1
    fidx = jnp.asarray(fidx, F32)[None, :]
    cols = jnp.asarray(cols)[None, :]
    role = jnp.asarray(role, jnp.int32)[None, :]
    t = jnp.arange(ctx_len + seq, dtype=jnp.int32)[:, None] - ctx_len
    pos = jnp.where(cols, t % GRID_W, t // GRID_W).astype(F32)
    ang = pos * (ROPE_BASE ** (-fidx / shift))
    rot = (t >= 0) & (role > 0)
    c, s = jnp.cos(ang), jnp.sin(ang)
    return (jnp.where(rot, c, 1.0), jnp.where(rot & (role == 1), -s, 0.0), jnp.where(rot & (role == 2), s, 0.0))


def _block_diag(w):
    n, c, d = w.shape
    eye = jnp.eye(n, dtype=w.dtype)
    return (w[:, :, None, :] * eye[:, None, :, None]).reshape(n * c, n * d)


def _prep_layer(l, w_in, g_cq, w_uq, g_ckv, w_ukv, conv_w, conv_b, lru_wa, lru_ba, lru_wx, lru_bx,
                g_grp, w_out, w_g1, b_g1, w_g2, b_g2):
    d = w_in.shape[1]
    wi = w_in[l]
    zeros = lambda n: jnp.zeros((d, n), wi.dtype)
    win = jnp.concatenate([
        wi[:, 0:384], zeros(64), wi[:, 384:416], zeros(32), wi[:, 416:1952]], axis=1).astype(BF16)
    hq = MLA_NOPE + MLA_ROPE
    wuq = jnp.concatenate(
        [jnp.pad(w_uq[l][:, h * hq:(h + 1) * hq], ((0, 0), (0, LANES - hq))) for h in range(MLA_HEADS)],
        axis=1).astype(BF16)
    wkv = w_ukv[l].reshape(MLA_KV_RANK, MLA_HEADS, MLA_NOPE + MLA_V)
    wuk = jnp.pad(wkv[:, :, :MLA_NOPE], ((0, 0), (0, 0), (0, LANES - MLA_NOPE))).reshape(MLA_KV_RANK, -1)
    wuv = wkv[:, :, MLA_NOPE:].reshape(MLA_KV_RANK, -1)
    wukv = jnp.concatenate([wuk, wuv], axis=1).astype(BF16)
    lru = []
    for dr in range(2):
        lru.append(jnp.concatenate([_block_diag(lru_wa[l, dr]), _block_diag(lru_wx[l, dr])], axis=1).astype(BF16))
        lru.append(jnp.concatenate([lru_ba[l, dr], lru_bx[l, dr]])[None, :])
    wr = jnp.concatenate([jnp.moveaxis(w_g2[l], 0, 1).reshape(d, N_EXPERTS), w_g1[l]], axis=1)
    wr = jnp.pad(wr, ((0, 0), (0, LANES - wr.shape[1]))).astype(BF16)
    br = jnp.pad(jnp.concatenate([b_g2[l].reshape(-1), b_g1[l]]), (0, LANES - N_GROUPS - N_EXPERTS))[None, :]
    return dict(win=win, gcq=g_cq[l][None, :], wuq=wuq, gckv=g_ckv[l][None, :], wukv=wukv,
                cw=conv_w[l], cb=conv_b[l][None, :], wf=lru[0], bf=lru[1], wb=lru[2], bb=lru[3],
                gg=g_grp[l][None, :], wout=w_out[l].astype(BF16), wr=wr, br=br)


def kernel(x, c, ctx, c_ctx, w_ada, b_ada, g_norm1, g_norm2, w_in, g_cq, w_uq, g_ckv, w_ukv, swa_sink,
           conv_w, conv_b, lru_wa, lru_ba, lru_wx, lru_bx, lru_lam, g_grp, w_out, w_g1, b_g1, w_g2, b_g2,
           w_e_gate, w_e_up, w_e_down, g_final):
    bsz, seq, d = x.shape
    ctx_len = ctx.shape[1]
    depth = w_ada.shape[0]
    assert seq % TM == 0 and ctx_len % TM == 0 and seq % GRID_W == 0
    n_ctx_tiles = ctx_len // TM

    rows = -(-(bsz + 1) // SUBLANES) * SUBLANES
    cc = jnp.pad(jnp.concatenate([c, c_ctx[None, :]], axis=0), ((0, rows - bsz - 1), (0, 0)))
    mods = _ada_call(cc, w_ada, b_ada)

    q4 = MLA_ROPE // 4
    s4 = SWA_HEAD_DIM // 4
    mla_tabs = _rope_tables(ctx_len, seq, [(MLA_NOPE, False), (MLA_NOPE + 2 * q4, True)], q4)
    swa_tabs = _rope_tables(
        ctx_len, seq,
        [(hh * SWA_HEAD_DIM + ax * 2 * s4, bool(ax)) for hh in range(LANES // SWA_HEAD_DIM) for ax in range(2)], s4)
    tabs = mla_tabs + swa_tabs

    src = (ctx, x)
    for l in range(depth):
        last = l == depth - 1
        p = _prep_layer(l, w_in, g_cq, w_uq, g_ckv, w_ukv, conv_w, conv_b, lru_wa, lru_ba, lru_wx, lru_bx,
                        g_grp, w_out, w_g1, b_g1, w_g2, b_g2)
        mod = mods[l].reshape(rows, 1, 6 * d)
        qm, km, vm, sq, sk, sv, lx, lg = _inproj_call(
            src, mod, g_norm1[l][None, :], p["win"], p["gcq"], p["wuq"], p["gckv"], p["wukv"], tabs, n_ctx_tiles)
        om = _mla_call(qm, km, vm, n_ctx_tiles, ctx_len)
        osw = _swa_call(swa_sink[l], sq, sk, sv, ctx_len)
        ol = _lru_call(lx, lg, p["cw"], p["cb"], p["wf"], p["bf"], p["wb"], p["bb"], lru_lam[l], n_ctx_tiles)
        x1, h2, meta, cnt = _outproj_call(om, osw, ol, src, mod, p["gg"], p["wout"], g_norm2[l][None, :],
                                          p["wr"], p["br"], n_ctx_tiles, last)
        src = _moe_call(h2, meta, cnt, x1, mod, g_final[None, :], w_e_gate[l].astype(BF16),
                        w_e_up[l].astype(BF16), w_e_down[l].astype(BF16), 0 if last else n_ctx_tiles, last)
    return src
```

```python
import functools

import jax
import jax.numpy as jnp
from jax import lax
from jax.experimental import pallas as pl
from jax.experimental.pallas import tpu as pltpu

GRID_W = 64
EPS = 1e-6
ROPE_BASE = 10000.0
NEG_INF = -1e30
MLA_HEADS = 4
MLA_NOPE = 64
MLA_ROPE = 32
MLA_V = 64
MLA_Q_RANK = 256
MLA_KV_RANK = 128
MLA_SCALE = (MLA_NOPE + MLA_ROPE) ** -0.5
SWA_HEADS = 4
SWA_KV_HEADS = 2
SWA_HEAD_DIM = 64
SWA_SCALE = SWA_HEAD_DIM ** -0.5
WINDOW = 128
LRU_WIDTH = 512
LRU_BLOCKS = 8
LRU_BW = LRU_WIDTH // LRU_BLOCKS
CONV_W = 4
LRU_C = 8.0
N_GROUPS = 4
EXPERTS_PER_GROUP = 8
N_EXPERTS = N_GROUPS * EXPERTS_PER_GROUP
D_EXPERT = 256
MLA_OUT = MLA_HEADS * MLA_V
SWA_OUT = SWA_HEADS * SWA_HEAD_DIM

LANES = 128
SUBLANES = 8
TM = 256
ZW = 2048
N_PAIRS = EXPERTS_PER_GROUP * (EXPERTS_PER_GROUP - 1) // 2
N_BUCKETS = N_GROUPS * N_PAIRS
MOE_TG = 256
MOE_TD = 512
LOG2E = 1.4426950408889634
VMEM_LIMIT = 56 * 1024 * 1024

F32 = jnp.float32
BF16 = jnp.bfloat16


def _params(sem):
    return pltpu.CompilerParams(dimension_semantics=sem, vmem_limit_bytes=VMEM_LIMIT)


def _full(shape):
    n = len(shape)
    return pl.BlockSpec(shape, lambda *_: (0,) * n)


def _rms(x, width):
    return x * lax.rsqrt(jnp.sum(x * x, axis=-1, keepdims=True) * (1.0 / width) + EPS)


def _dot(a, b):
    return jnp.dot(a, b, preferred_element_type=F32)


def _sigmoid(x):
    return 0.5 * jnp.tanh(0.5 * x) + 0.5


def _pack_pairs(x):
    w = x.shape[-1] // 2
    bits = pltpu.bitcast(x.astype(F32), jnp.uint32)
    return (bits[:, w:] & jnp.uint32(0xFFFF0000)) | (bits[:, :w] >> 16)


def _unpack_pairs(u):
    lo = pltpu.bitcast(u << 16, F32)
    hi = pltpu.bitcast(u & jnp.uint32(0xFFFF0000), F32)
    return jnp.concatenate([lo, hi], axis=-1).astype(BF16)


def _store_slabs(ref, x, pitch=SUBLANES):
    n, w = x.shape
    for s in range(pitch):
        piece = x[:, s * LANES:(s + 1) * LANES] if (s + 1) * LANES <= w else jnp.zeros((n, LANES), x.dtype)
        ref[pl.ds(s, n, stride=pitch), :] = piece


def _load_slabs(ref, n, k, pitch=SUBLANES):
    return jnp.concatenate([ref[pl.ds(s, n, stride=pitch), :] for s in range(k)], axis=-1)


def _dot_nt(a, b):
    return lax.dot_general(a, b, (((1,), (1,)), ((), ())), preferred_element_type=F32)


def _ada_kernel(c_ref, w_ref, b_ref, o_ref):
    c = c_ref[...]
    s = c * jax.nn.sigmoid(c)
    o_ref[0] = jnp.dot(s, w_ref[0], preferred_element_type=F32,
                       precision=lax.Precision.HIGHEST) + b_ref[0]


def _ada_call(cc, w_ada, b_ada):
    depth, d, n = w_ada.shape
    r = cc.shape[0]
    tn = 1536
    return pl.pallas_call(
        _ada_kernel,
        grid=(depth, n // tn),
        in_specs=[
            pl.BlockSpec((r, d), lambda l, j: (0, 0)),
            pl.BlockSpec((1, d, tn), lambda l, j: (l, 0, j)),
            pl.BlockSpec((1, 1, tn), lambda l, j: (l, 0, j)),
        ],
        out_specs=pl.BlockSpec((1, r, tn), lambda l, j: (l, 0, j)),
        out_shape=jax.ShapeDtypeStruct((depth, r, n), F32),
        compiler_params=_params(("arbitrary", "arbitrary")),
        name="adaln",
    )(cc, w_ada, b_ada.reshape(depth, 1, n))


def _rope(x, cos, sina, sinb, shift):
    n = x.shape[-1]
    reps = n // LANES
    if reps > 1:
        cos = jnp.concatenate([cos] * reps, axis=-1)
        sina = jnp.concatenate([sina] * reps, axis=-1)
        sinb = jnp.concatenate([sinb] * reps, axis=-1)
    return x * cos + pltpu.roll(x, n - shift, 1) * sina + pltpu.roll(x, shift, 1) * sinb


def _inproj_kernel(*refs, n_src, n_ctx_tiles):
    (sh_ref, sc_ref, g1_ref, win_ref, gcq_ref, wuq_ref, gckv_ref, wukv_ref,
     mcos_ref, msa_ref, msb_ref, scos_ref, ssa_ref, ssb_ref,
     qm_ref, km_ref, vm_ref, sq_ref, sk_ref, sv_ref, lx_ref, lg_ref) = refs[n_src:]
    x = _read_stream(refs[:n_src], n_ctx_tiles)
    d = x.shape[-1]
    h = _rms(x, d) * g1_ref[...] * (1.0 + sc_ref[0]) + sh_ref[0]
    z = _dot(h.astype(BF16), win_ref[...])

    mcos, msa, msb = mcos_ref[...], msa_ref[...], msb_ref[...]
    scos, ssa, ssb = scos_ref[...], ssa_ref[...], ssb_ref[...]

    cq = _rms(z[:, 0:256], MLA_Q_RANK) * gcq_ref[...]
    q = _dot(cq.astype(BF16), wuq_ref[...])
    q = _rope(q, mcos, msa, msb, MLA_ROPE // 4)
    qm_ref[0] = (q * (MLA_SCALE * LOG2E)).astype(BF16)

    ckv = _rms(z[:, 256:384], MLA_KV_RANK) * gckv_ref[...]
    kv = _dot(ckv.astype(BF16), wukv_ref[...])
    kr = _rope(z[:, 384:512], mcos, msa, msb, MLA_ROPE // 4)
    km_ref[0] = (kv[:, 0:512] + jnp.concatenate([kr] * MLA_HEADS, axis=-1)).astype(BF16)
    ones = jnp.ones((kv.shape[0], LANES), F32)
    vm_ref[0] = jnp.concatenate([kv[:, 512:640], ones, kv[:, 640:768], ones], axis=-1).astype(BF16)

    sq_ref[0] = (_rope(z[:, 512:768], scos, ssa, ssb, SWA_HEAD_DIM // 4) * (SWA_SCALE * LOG2E)).astype(BF16)
    sk = _rope(z[:, 768:896], scos, ssa, ssb, SWA_HEAD_DIM // 4)
    sv = z[:, 896:1024]
    lane = lax.broadcasted_iota(jnp.int32, sk.shape, 1)
    low = lane < SWA_HEAD_DIM
    k0 = jnp.where(low, sk, 0.0)
    k1 = jnp.where(low, 0.0, sk)
    sk_ref[0] = jnp.concatenate([k0 + pltpu.roll(k0, SWA_HEAD_DIM, 1), k1 + pltpu.roll(k1, SWA_HEAD_DIM, 1)],
                                axis=-1).astype(BF16)
    v0 = jnp.where(low, sv, 0.0)
    v1 = jnp.where(low, 0.0, sv)
    one_hi = jnp.where(low, 0.0, 1.0)
    sv_ref[0] = jnp.concatenate([v0 + one_hi, pltpu.roll(v1, SWA_HEAD_DIM, 1) + one_hi], axis=-1).astype(BF16)
    lx_ref[0] = z[:, 1024:1536].astype(BF16)
    lg_ref[0] = z[:, 1536:2048].astype(BF16)


def _mod_spec(j, b_rows, n_ctx_tiles, d, skip=0):
    return pl.BlockSpec((1, 1, d), lambda i, b: (jnp.where(i + skip < n_ctx_tiles, b_rows, b), 0, j))


def _stream_specs(src, n_ctx_tiles, d, skip=0):
    if not isinstance(src, tuple):
        return [pl.BlockSpec((1, TM, d), lambda i, b: (b, i + skip, 0))], [src]
    ctx, lat = src
    lat_spec = pl.BlockSpec(
        (1, TM, d), lambda i, b: (jnp.where(i + skip < n_ctx_tiles, 0, b), jnp.maximum(i + skip - n_ctx_tiles, 0), 0))
    if skip >= n_ctx_tiles:
        return [lat_spec], [lat]
    ctx_spec = pl.BlockSpec(
        (1, TM, d), lambda i, b: (jnp.where(i + skip < n_ctx_tiles, b, 0), jnp.minimum(i + skip, n_ctx_tiles - 1), 0))
    return [ctx_spec, lat_spec], [ctx, lat]


def _read_stream(refs, n_ctx_tiles, skip=0):
    if len(refs) == 1:
        return refs[0][0]
    return jnp.where(pl.program_id(0) + skip < n_ctx_tiles, refs[0][0], refs[1][0])


def _inproj_call(src, mod, g1, win, gcq, wuq, gckv, wukv, tabs, n_ctx_tiles):
    bsz, _, d = src[1].shape if isinstance(src, tuple) else src.shape
    j = tabs[0].shape[0]
    nt = j // TM
    x_specs, x_arrays = _stream_specs(src, n_ctx_tiles, d)
    tok = lambda w: pl.BlockSpec((1, TM, w), lambda i, b: (b, i, 0))
    tab = pl.BlockSpec((TM, LANES), lambda i, b: (i, 0))
    widths = (512, 512, 512, 256, 256, 256, 512, 512)
    return pl.pallas_call(
        functools.partial(_inproj_kernel, n_src=len(x_arrays), n_ctx_tiles=n_ctx_tiles),
        grid=(nt, bsz),
        in_specs=x_specs + [
            _mod_spec(0, bsz, n_ctx_tiles, d),
            _mod_spec(1, bsz, n_ctx_tiles, d),
            _full(g1.shape), _full(win.shape), _full(gcq.shape), _full(wuq.shape),
            _full(gckv.shape), _full(wukv.shape),
            tab, tab, tab, tab, tab, tab,
        ],
        out_specs=[tok(w) for w in widths],
        out_shape=[jax.ShapeDtypeStruct((bsz, j, w), BF16) for w in widths],
        compiler_params=_params(("arbitrary", "arbitrary")),
        name="inproj",
    )(*x_arrays, mod, mod, g1, win, gcq, wuq, gckv, wukv, *tabs)


def _mla_heads(q_ref, k_ref, v_ref, o_ref, klen):
    lane = lax.broadcasted_iota(jnp.int32, (q_ref.shape[1], LANES), 1)
    for hp in range(MLA_HEADS // 2):
        v = v_ref[0, 0:klen, 2 * hp * LANES:2 * (hp + 1) * LANES]
        outs = []
        for h in range(2 * hp, 2 * hp + 2):
            q = q_ref[0, :, h * LANES:(h + 1) * LANES]
            k = k_ref[0, 0:klen, h * LANES:(h + 1) * LANES]
            s = _dot_nt(q, k)
            m = jnp.max(s, axis=-1, keepdims=True)
            o = _dot(jnp.exp2(s - m).astype(BF16), v)
            outs.append(o[:, 0:LANES] / o[:, LANES:LANES + 1])
        o_ref[0, :, hp * LANES:(hp + 1) * LANES] = jnp.where(lane < MLA_V, outs[0], outs[1]).astype(o_ref.dtype)


def _mla_kernel(q_ref, k_ref, v_ref, o_ref, *, n_ctx_tiles, ctx_len):
    i = pl.program_id(1)

    @pl.when(i < n_ctx_tiles)
    def _():
        _mla_heads(q_ref, k_ref, v_ref, o_ref, ctx_len)

    @pl.when(i >= n_ctx_tiles)
    def _():
        _mla_heads(q_ref, k_ref, v_ref, o_ref, k_ref.shape[1])


def _mla_call(qm, km, vm, n_ctx_tiles, ctx_len):
    bsz, j, w = qm.shape
    nt = j // TM
    return pl.pallas_call(
        functools.partial(_mla_kernel, n_ctx_tiles=n_ctx_tiles, ctx_len=ctx_len),
        grid=(bsz, nt),
        in_specs=[
            pl.BlockSpec((1, TM, w), lambda b, i: (b, i, 0)),
            pl.BlockSpec((1, j, w), lambda b, i: (b, 0, 0)),
            pl.BlockSpec((1, j, w), lambda b, i: (b, 0, 0)),
        ],
        out_specs=pl.BlockSpec((1, TM, MLA_OUT), lambda b, i: (b, i, 0)),
        out_shape=jax.ShapeDtypeStruct((bsz, j, MLA_OUT), BF16),
        compiler_params=_params(("arbitrary", "arbitrary")),
        name="mla_attn",
    )(qm, km, vm)


def _swa_kernel(sink_ref, q_ref, k_ref, v_ref, o_ref, *, ctx_len):
    i = pl.program_id(1)
    j = k_ref.shape[1]
    span = TM + 2 * WINDOW
    row0 = pl.multiple_of(jnp.clip(i * TM - WINDOW, 0, j - span), WINDOW)
    qpos = i * TM - ctx_len + lax.broadcasted_iota(jnp.int32, (TM, span), 0)
    kpos = row0 - ctx_len + lax.broadcasted_iota(jnp.int32, (TM, span), 1)
    valid = (jnp.abs(qpos - kpos) <= WINDOW) & (kpos >= 0) & (qpos >= 0)
    valid2 = jnp.concatenate([valid, valid], axis=0)
    low = lax.broadcasted_iota(jnp.int32, (TM, LANES), 1) < SWA_HEAD_DIM
    top = lax.broadcasted_iota(jnp.int32, (2 * TM, 1), 0) < TM
    for kvh in range(SWA_KV_HEADS):
        kcols = slice(kvh * LANES, (kvh + 1) * LANES)
        vcols = kcols
        qpair = q_ref[0, :, kcols]
        zero = jnp.zeros_like(qpair)
        q2 = jnp.concatenate([jnp.where(low, qpair, zero), jnp.where(low, zero, qpair)], axis=0)
        s_loc = jnp.where(valid2, _dot_nt(q2, k_ref[0, pl.ds(row0, span), kcols]), NEG_INF)
        s_ctx = _dot_nt(q2, k_ref[0, 0:ctx_len, kcols])
        sink = jnp.where(top, sink_ref[2 * kvh], sink_ref[2 * kvh + 1]) * LOG2E
        m = jnp.maximum(jnp.maximum(jnp.max(s_loc, axis=-1, keepdims=True),
                                    jnp.max(s_ctx, axis=-1, keepdims=True)), sink)
        o = (_dot(jnp.exp2(s_loc - m).astype(BF16), v_ref[0, pl.ds(row0, span), vcols])
             + _dot(jnp.exp2(s_ctx - m).astype(BF16), v_ref[0, 0:ctx_len, vcols]))
        res = o / (o[:, SWA_HEAD_DIM:SWA_HEAD_DIM + 1] + jnp.exp2(sink - m))
        o_ref[0, :, kcols] = jnp.where(low, res[0:TM], pltpu.roll(res[TM:2 * TM], SWA_HEAD_DIM, 1)
                                       ).astype(o_ref.dtype)


def _swa_call(sink, sq, sk, sv, ctx_len):
    bsz, j, _ = sq.shape
    nt = j // TM
    return pl.pallas_call(
        functools.partial(_swa_kernel, ctx_len=ctx_len),
        grid=(bsz, nt),
        in_specs=[
            pl.BlockSpec(memory_space=pltpu.SMEM),
            pl.BlockSpec((1, TM, SWA_OUT), lambda b, i: (b, i, 0)),
            pl.BlockSpec((1, j, sk.shape[-1]), lambda b, i: (b, 0, 0)),
            pl.BlockSpec((1, j, sv.shape[-1]), lambda b, i: (b, 0, 0)),
        ],
        out_specs=pl.BlockSpec((1, TM, SWA_OUT), lambda b, i: (b, i, 0)),
        out_shape=jax.ShapeDtypeStruct((bsz, j, SWA_OUT), BF16),
        compiler_params=_params(("arbitrary", "arbitrary")),
        name="swa_attn",
    )(sink, sq, sk, sv)


def _lru_gates(prev_ref, cur_ref, next_ref, cw_ref, cb_ref, w_ref, b_ref, sp_ref, seg_start, seg_end):
    z = cur_ref[0].astype(F32)
    row = lax.broadcasted_iota(jnp.int32, z.shape, 0)
    keep_prev = jnp.where(seg_start, 0.0, 1.0)
    keep_next = jnp.where(seg_end, 0.0, 1.0)
    p2 = prev_ref[0, TM - 2:TM - 1, :].astype(F32) * keep_prev
    p1 = prev_ref[0, TM - 1:TM, :].astype(F32) * keep_prev
    n0 = next_ref[0, 0:1, :].astype(F32) * keep_next
    z_m1 = jnp.where(row == 0, p1, pltpu.roll(z, 1, 0))
    z_m2 = jnp.where(row == 0, p2, jnp.where(row == 1, p1, pltpu.roll(z, 2, 0)))
    z_p1 = jnp.where(row == TM - 1, n0, pltpu.roll(z, TM - 1, 0))
    u = cb_ref[...] + z_m2 * cw_ref[0:1, :]
    u = u + z_m1 * cw_ref[1:2, :]
    u = u + z * cw_ref[2:3, :]
    u = u + z_p1 * cw_ref[3:4, :]
    g = _dot(u.astype(BF16), w_ref[...]) + b_ref[...]
    r = _sigmoid(g[:, 0:LRU_WIDTH])
    ig = _sigmoid(g[:, LRU_WIDTH:2 * LRU_WIDTH])
    log_a = (-LRU_C) * r * sp_ref[...]
    a = jnp.exp(log_a)
    om = 1.0 - a * a
    bt = jnp.where(om > 0.0, om * lax.rsqrt(om), 0.0) * (ig * u)
    return a, bt


def _lru_scan(a, b, h0, reverse):
    t, w = a.shape
    g = t // SUBLANES
    a = a.reshape(g, SUBLANES, w)
    b = b.reshape(g, SUBLANES, w)
    sub = lax.broadcasted_iota(jnp.int32, (g, SUBLANES, w), 1)
    d = 1
    while d < SUBLANES:
        if reverse:
            shift, ok = SUBLANES - d, sub < SUBLANES - d
        else:
            shift, ok = d, sub >= d
        a_sh = pltpu.roll(a, shift, 1)
        b_sh = pltpu.roll(b, shift, 1)
        b = jnp.where(ok, a * b_sh + b, b)
        a = jnp.where(ok, a * a_sh, a)
        d *= 2
    hs = [None] * g
    h = h0
    order = range(g - 1, -1, -1) if reverse else range(g)
    for gi in order:
        hg = a[gi] * h + b[gi]
        hs[gi] = hg
        h = hg[0:1, :] if reverse else hg[SUBLANES - 1:SUBLANES, :]
    return jnp.concatenate(hs, axis=0), h


def _softplus_neg(lam):
    x = -lam
    return jnp.maximum(x, 0.0) + jnp.log1p(jnp.exp(-jnp.abs(x)))


def _lru_fwd_kernel(prev_ref, cur_ref, next_ref, cw_ref, cb_ref, w_ref, b_ref, lam_ref,
                    hf_ref, carry_ref, sp_ref, *, n_ctx_tiles):
    c = pl.program_id(1)
    nt = pl.num_programs(1)

    @pl.when(c == 0)
    def _():
        carry_ref[...] = jnp.zeros_like(carry_ref)
        sp_ref[...] = _softplus_neg(lam_ref[...])

    seg_start = (c == 0) | (c == n_ctx_tiles)
    seg_end = (c == n_ctx_tiles - 1) | (c == nt - 1)
    a, bt = _lru_gates(prev_ref, cur_ref, next_ref, cw_ref, cb_ref, w_ref, b_ref, sp_ref,
                       seg_start, seg_end)
    hs, h = _lru_scan(a, bt, carry_ref[...], False)
    carry_ref[...] = h
    hf_ref[0] = hs.astype(hf_ref.dtype)


def _bwd_chunk(s, nt, n_ctx_tiles):
    return jnp.where(s < n_ctx_tiles, n_ctx_tiles - 1 - s, nt - 1 - (s - n_ctx_tiles))


def _lru_bwd_kernel(prev_ref, cur_ref, next_ref, cw_ref, cb_ref, w_ref, b_ref, lam_ref,
                    hf_ref, lg_ref, o_ref, carry_ref, sp_ref, *, n_ctx_tiles):
    s = pl.program_id(1)
    nt = pl.num_programs(1)
    c = _bwd_chunk(s, nt, n_ctx_tiles)

    @pl.when(s == 0)
    def _():
        carry_ref[...] = jnp.zeros_like(carry_ref)
        sp_ref[...] = _softplus_neg(lam_ref[...])

    seg_start = (c == 0) | (c == n_ctx_tiles)
    seg_end = (c == n_ctx_tiles - 1) | (c == nt - 1)
    a, bt = _lru_gates(prev_ref, cur_ref, next_ref, cw_ref, cb_ref, w_ref, b_ref, sp_ref,
                       seg_start, seg_end)
    hs, h = _lru_scan(a, bt, carry_ref[...], True)
    carry_ref[...] = h
    gate = jax.nn.gelu(lg_ref[0].astype(F32), approximate=True)
    o_ref[0] = ((hf_ref[0].astype(F32) + hs) * gate).astype(o_ref.dtype)


def _lru_call(lx, lg, cw, cb, wf, bf, wb, bb, lam, n_ctx_tiles):
    bsz, j, w = lx.shape
    nt = j // TM
    consts = [_full(cw.shape), _full((1, w)), _full(wf.shape), _full(bf.shape), _full((1, w))]
    scratch = [pltpu.VMEM((1, w), F32), pltpu.VMEM((1, w), F32)]
    blk = lambda f: pl.BlockSpec((1, TM, w), f)
    hf = pl.pallas_call(
        functools.partial(_lru_fwd_kernel, n_ctx_tiles=n_ctx_tiles),
        grid=(bsz, nt),
        in_specs=[
            blk(lambda b, c: (b, jnp.maximum(c - 1, 0), 0)),
            blk(lambda b, c: (b, c, 0)),
            blk(lambda b, c: (b, jnp.minimum(c + 1, nt - 1), 0)),
        ] + consts,
        out_specs=blk(lambda b, c: (b, c, 0)),
        out_shape=jax.ShapeDtypeStruct((bsz, j, w), BF16),
        scratch_shapes=scratch,
        compiler_params=_params(("arbitrary", "arbitrary")),
        name="lru_fwd",
    )(lx, lx, lx, cw, cb, wf, bf, lam[0:1])
    cix = lambda s: _bwd_chunk(s, nt, n_ctx_tiles)
    return pl.pallas_call(
        functools.partial(_lru_bwd_kernel, n_ctx_tiles=n_ctx_tiles),
        grid=(bsz, nt),
        in_specs=[
            blk(lambda b, s: (b, jnp.maximum(cix(s) - 1, 0), 0)),
            blk(lambda b, s: (b, cix(s), 0)),
            blk(lambda b, s: (b, jnp.minimum(cix(s) + 1, nt - 1), 0)),
        ] + consts + [
            blk(lambda b, s: (b, cix(s), 0)),
            blk(lambda b, s: (b, cix(s), 0)),
        ],
        out_specs=blk(lambda b, s: (b, cix(s), 0)),
        out_shape=jax.ShapeDtypeStruct((bsz, j, w), BF16),
        scratch_shapes=scratch,
        compiler_params=_params(("arbitrary", "arbitrary")),
        name="lru_bwd",
    )(lx, lx, lx, cw, cb, wb, bb, lam[1:2], hf, lg)


def _route(logits, carry):
    lt = logits.T
    tm = lt.shape[1]
    big = jnp.int32(LANES)
    lg = lt[N_EXPERTS:N_EXPERTS + SUBLANES]
    rg = lax.broadcasted_iota(jnp.int32, lg.shape, 0)
    is_g = rg < N_GROUPS
    mg = jnp.max(jnp.where(is_g, lg, -jnp.inf), axis=0, keepdims=True)
    eg = jnp.where(is_g, jnp.exp(lg - mg), 0.0)
    pg = eg / jnp.sum(eg, axis=0, keepdims=True)
    pg_top = jnp.max(pg, axis=0, keepdims=True)
    g_idx = jnp.min(jnp.where(is_g & (pg == pg_top), rg, big), axis=0, keepdims=True)
    le = lt[0:N_EXPERTS]
    re = lax.broadcasted_iota(jnp.int32, le.shape, 0)
    lo = g_idx * EXPERTS_PER_GROUP
    sel = (re >= lo) & (re < lo + EXPERTS_PER_GROUP)
    me = jnp.max(jnp.where(sel, le, -jnp.inf), axis=0, keepdims=True)
    ee = jnp.where(sel, jnp.exp(le - me), 0.0)
    pe = ee / jnp.sum(ee, axis=0, keepdims=True)
    p1 = jnp.max(jnp.where(sel, pe, -1.0), axis=0, keepdims=True)
    i1 = jnp.min(jnp.where(sel & (pe == p1), re, big), axis=0, keepdims=True)
    sel2 = sel & (re != i1)
    p2 = jnp.max(jnp.where(sel2, pe, -1.0), axis=0, keepdims=True)
    i2 = jnp.min(jnp.where(sel2 & (pe == p2), re, big), axis=0, keepdims=True)
    den = p1 + p2
    w1 = pg_top * p1 / den
    w2 = pg_top * p2 / den
    first_lo = i1 < i2
    ia = (jnp.where(first_lo, i1, i2) - lo).astype(F32)
    ib = (jnp.where(first_lo, i2, i1) - lo).astype(F32)
    wa = jnp.where(first_lo, w1, w2)
    wb = jnp.where(first_lo, w2, w1)
    pair = ia * (2 * EXPERTS_PER_GROUP - 1 - ia) * 0.5 + (ib - ia - 1.0)
    bucket = (g_idx.astype(F32) * N_PAIRS + pair).astype(jnp.int32)
    rb = lax.broadcasted_iota(jnp.int32, (LANES, tm), 0)
    mine = rb == bucket
    picks = jnp.where(mine, 1.0, 0.0)
    tri = jnp.where(lax.broadcasted_iota(jnp.int32, (tm, tm), 0) < lax.broadcasted_iota(jnp.int32, (tm, tm), 1),
                    1.0, 0.0).astype(BF16)
    before = _dot(picks.astype(BF16), tri) + carry
    rank = jnp.sum(jnp.where(mine, before, 0.0), axis=0, keepdims=True)
    zeros = jnp.zeros((SUBLANES - 2, tm), F32)
    meta = jnp.concatenate([bucket.astype(F32), rank, zeros], axis=0).astype(jnp.int32)
    wrows = jnp.concatenate([wa, wb, jnp.zeros((LANES - 2, tm), F32)], axis=0)
    return meta, wrows.T, carry + jnp.sum(picks, axis=1, keepdims=True)


def _outproj_kernel(*refs, n_src, n_ctx_tiles, skip):
    (om_ref, os_ref, ol_ref, gt1_ref, sh2_ref, sc2_ref, gg_ref, wout_ref,
     g2_ref, wr_ref, br_ref, x1_ref, h2_ref, meta_ref, cnt_ref, carry_ref) = refs[n_src:]
    x = _read_stream(refs[:n_src], n_ctx_tiles, skip)
    d = x.shape[-1]

    @pl.when((pl.program_id(0) == 0) & (pl.program_id(1) == 0))
    def _():
        carry_ref[...] = jnp.zeros_like(carry_ref)

    gg = gg_ref[...]
    nm = _rms(om_ref[0].astype(F32), MLA_OUT) * gg[:, 0:MLA_OUT]
    ns = _rms(os_ref[0].astype(F32), SWA_OUT) * gg[:, MLA_OUT:MLA_OUT + SWA_OUT]
    nl = _rms(ol_ref[0].astype(F32), LRU_WIDTH) * gg[:, MLA_OUT + SWA_OUT:]
    merged = jnp.concatenate([nm, ns, nl], axis=-1).astype(BF16)
    mix = _dot(merged, wout_ref[...])
    x1 = x + gt1_ref[0] * mix
    x1_ref[0] = x1
    h2 = (_rms(x1, d) * g2_ref[...] * (1.0 + sc2_ref[0]) + sh2_ref[0]).astype(BF16)
    meta, wcols, carry = _route(_dot(h2, wr_ref[...]) + br_ref[...], carry_ref[...])
    _store_slabs(h2_ref, jnp.concatenate([_pack_pairs(h2), pltpu.bitcast(wcols, jnp.uint32)], axis=-1))
    carry_ref[...] = carry
    cnt_ref[...] = jnp.broadcast_to(carry, (LANES, LANES)).T[0:1, :].astype(jnp.int32)
    for m in range(TM // LANES):
        meta_ref[m] = meta[:, m * LANES:(m + 1) * LANES]


def _outproj_call(om, osw, ol, src, mod, gg, wout, g2, wr, br, n_ctx_tiles, latent_only):
    bsz, j_all, _ = om.shape
    d = wout.shape[1]
    skip = n_ctx_tiles if latent_only else 0
    nt = j_all // TM - skip
    j = nt * TM
    tok_in = lambda w: pl.BlockSpec((1, TM, w), lambda i, b: (b, i + skip, 0))
    tok = lambda w: pl.BlockSpec((1, TM, w), lambda i, b: (b, i, 0))
    x_specs, x_arrays = _stream_specs(src, n_ctx_tiles, d, skip)
    return pl.pallas_call(
        functools.partial(_outproj_kernel, n_src=len(x_arrays), n_ctx_tiles=n_ctx_tiles, skip=skip),
        grid=(nt, bsz),
        in_specs=x_specs + [
            tok_in(MLA_OUT), tok_in(SWA_OUT), tok_in(LRU_WIDTH),
            _mod_spec(2, bsz, n_ctx_tiles, d, skip),
            _mod_spec(3, bsz, n_ctx_tiles, d, skip),
            _mod_spec(4, bsz, n_ctx_tiles, d, skip),
            _full(gg.shape), _full(wout.shape), _full(g2.shape), _full(wr.shape), _full(br.shape),
        ],
        out_specs=[
            tok(d), pl.BlockSpec((TM * SUBLANES, LANES), lambda i, b: (b * nt + i, 0)),
            pl.BlockSpec((TM // LANES, SUBLANES, LANES), lambda i, b: (b * nt + i, 0, 0)),
            pl.BlockSpec((1, LANES), lambda i, b: (0, 0)),
        ],
        out_shape=[
            jax.ShapeDtypeStruct((bsz, j, d), F32),
            jax.ShapeDtypeStruct((bsz * j * SUBLANES, LANES), jnp.uint32),
            jax.ShapeDtypeStruct((bsz * j // LANES, SUBLANES, LANES), jnp.int32),
            jax.ShapeDtypeStruct((1, LANES), jnp.int32),
        ],
        scratch_shapes=[pltpu.VMEM((LANES, 1), F32)],
        compiler_params=_params(("arbitrary", "arbitrary")),
        name="outproj_route",
    )(*x_arrays, om, osw, ol, mod, mod, mod, gg, wout, g2, wr, br)


def _padded_tiles(count):
    return lax.shift_right_logical(count + (MOE_TG - 1), MOE_TG.bit_length() - 1)


def _plan_kernel(cnt_ref, off_ref, tea_ref, teb_ref, nu_ref, *, n_tiles_max):
    off = jnp.int32(0)
    ti = jnp.int32(0)
    bucket = 0
    for g in range(N_GROUPS):
        for a in range(EXPERTS_PER_GROUP):
            for b in range(a + 1, EXPERTS_PER_GROUP):
                n_q = _padded_tiles(cnt_ref[0, bucket])
                off_ref[bucket] = off

                def fill(k, _, base=ti, ea=g * EXPERTS_PER_GROUP + a, eb=g * EXPERTS_PER_GROUP + b):
                    tea_ref[base + k] = ea
                    teb_ref[base + k] = eb
                    return 0

                lax.fori_loop(0, n_q, fill, 0)
                off = off + n_q * MOE_TG
                ti = ti + n_q
                bucket += 1
    nu_ref[0] = ti

    def rest(k, _):
        tea_ref[k] = N_EXPERTS - 2
        teb_ref[k] = N_EXPERTS - 1
        return 0

    lax.fori_loop(ti, n_tiles_max, rest, 0)


def _plan_call(cnt, n_tiles_max):
    smem = pl.BlockSpec(memory_space=pltpu.SMEM)
    return pl.pallas_call(
        functools.partial(_plan_kernel, n_tiles_max=n_tiles_max),
        in_specs=[smem],
        out_specs=[smem, smem, smem, smem],
        out_shape=[
            jax.ShapeDtypeStruct((N_BUCKETS,), jnp.int32),
            jax.ShapeDtypeStruct((n_tiles_max,), jnp.int32),
            jax.ShapeDtypeStruct((n_tiles_max,), jnp.int32),
            jax.ShapeDtypeStruct((1,), jnp.int32),
        ],
        name="moe_plan",
    )(cnt)


def _row_copy(src_ref, src_row, dst_ref, dst_row, sem, used=SUBLANES, pitch=SUBLANES):
    slot = lambda row: pl.ds(pl.multiple_of(row * pitch, pitch), used)
    return pltpu.make_async_copy(src_ref.at[slot(src_row)], dst_ref.at[slot(dst_row)], sem)


def _dispatch_kernel(off_ref, nu_ref, h_ref, meta_ref, cnt_ref, xs_ref, zero_ref, sem, tile_sem, *, tile, used):
    tile_rows = MOE_TG * SUBLANES
    n_tiles_max = xs_ref.shape[0] // tile_rows

    @pl.when(pl.program_id(0) == 0)
    def _():
        zero_ref[...] = jnp.zeros_like(zero_ref)

        def per_bucket(q, n):
            c = cnt_ref[0, q]
            padded = _padded_tiles(c) * MOE_TG
            base = off_ref[q]

            def fill(r, _):
                _row_copy(zero_ref, 0, xs_ref, base + r, sem, used).start()
                return 0

            lax.fori_loop(c, padded, fill, 0)
            return n + (padded - c)

        n_pad = lax.fori_loop(0, N_BUCKETS, per_bucket, jnp.int32(0))

        def drain(r, _):
            _row_copy(zero_ref, 0, xs_ref, 0, sem, used).wait()
            return 0

        lax.fori_loop(0, n_pad, drain, 0)

        def unused_tile(ti, _):
            cp = pltpu.make_async_copy(
                zero_ref, xs_ref.at[pl.ds(pl.multiple_of(ti * tile_rows, tile_rows), tile_rows)], tile_sem)
            cp.start()
            cp.wait()
            return 0

        lax.fori_loop(nu_ref[0], n_tiles_max, unused_tile, 0)

    for m in range(tile // LANES):
        for r in range(LANES):
            dst = off_ref[meta_ref[m, 0, r]] + meta_ref[m, 1, r]
            _row_copy(h_ref, m * LANES + r, xs_ref, dst, sem, used).start(priority=r % 2)

    pltpu.make_async_copy(h_ref.at[pl.ds(0, tile * used)], xs_ref.at[pl.ds(0, tile * used)], sem).wait()


def _dispatch_call(off, nu, h2, meta, cnt, rows_max, used):
    t = h2.shape[0] // SUBLANES
    tile = MOE_TD if t % MOE_TD == 0 else TM
    return pl.pallas_call(
        functools.partial(_dispatch_kernel, tile=tile, used=used),
        grid_spec=pltpu.PrefetchScalarGridSpec(
            num_scalar_prefetch=2,
            grid=(t // tile,),
            in_specs=[
                pl.BlockSpec((tile * SUBLANES, LANES), lambda i, off, nu: (i, 0)),
                pl.BlockSpec((tile // LANES, SUBLANES, LANES), lambda i, off, nu: (i, 0, 0),
                             memory_space=pltpu.SMEM),
                pl.BlockSpec(memory_space=pltpu.SMEM),
            ],
            out_specs=pl.BlockSpec(memory_space=pl.ANY),
            scratch_shapes=[pltpu.VMEM((MOE_TG * SUBLANES, LANES), h2.dtype), pltpu.SemaphoreType.DMA,
                            pltpu.SemaphoreType.DMA],
        ),
        out_shape=jax.ShapeDtypeStruct((rows_max * SUBLANES, LANES), h2.dtype),
        compiler_params=_params(("arbitrary",)),
        name="moe_dispatch",
    )(off, nu, h2, meta, cnt)


def _expert_mlp(x, e, wg_ref, wu_ref, wd_ref):
    gate = _dot(x, wg_ref[e])
    act = gate * _sigmoid(gate) * _dot(x, wu_ref[e])
    return _dot(act.astype(BF16), wd_ref[e])


def _expert_kernel(tea_ref, teb_ref, nu_ref, x_ref, wg_ref, wu_ref, wd_ref, y_ref):
    groups = wg_ref.shape[1] // (2 * LANES)
    i = pl.program_id(0)

    @pl.when(i < nu_ref[0])
    def _():
        row = _load_slabs(x_ref, MOE_TG, groups + 1)
        x = _unpack_pairs(row[:, 0:groups * LANES])
        wts = pltpu.bitcast(row[:, groups * LANES:], F32)
        ea = tea_ref[i] & (EXPERTS_PER_GROUP - 1)
        eb = teb_ref[i] & (EXPERTS_PER_GROUP - 1)
        y = (wts[:, 0:1] * _expert_mlp(x, ea, wg_ref, wu_ref, wd_ref)
             + wts[:, 1:2] * _expert_mlp(x, eb, wg_ref, wu_ref, wd_ref))
        _store_slabs(y_ref, _pack_pairs(y.astype(BF16)), groups)

    @pl.when(pl.program_id(0) >= nu_ref[0])
    def _():
        y_ref[...] = jnp.zeros_like(y_ref)


def _expert_call(tea, teb, nu, xs, wg, wu, wd):
    d = wg.shape[1]
    ypitch = d // (2 * LANES)
    n_tiles = xs.shape[0] // (MOE_TG * SUBLANES)
    used = lambda i, nu: jnp.minimum(i, nu[0] - 1)
    shift = EXPERTS_PER_GROUP.bit_length() - 1
    wspec = lambda rows, cols: pl.BlockSpec(
        (EXPERTS_PER_GROUP, rows, cols), lambda i, tea, teb, nu: (tea[used(i, nu)] >> shift, 0, 0))
    return pl.pallas_call(
        _expert_kernel,
        grid_spec=pltpu.PrefetchScalarGridSpec(
            num_scalar_prefetch=3,
            grid=(n_tiles,),
            in_specs=[
                pl.BlockSpec((MOE_TG * SUBLANES, LANES), lambda i, tea, teb, nu: (used(i, nu), 0)),
                wspec(d, D_EXPERT), wspec(d, D_EXPERT), wspec(D_EXPERT, d),
            ],
            out_specs=pl.BlockSpec((MOE_TG * ypitch, LANES), lambda i, tea, teb, nu: (i, 0)),
        ),
        out_shape=jax.ShapeDtypeStruct((n_tiles * MOE_TG * ypitch, LANES), jnp.uint32),
        compiler_params=_params(("arbitrary",)),
        name="moe_experts",
    )(tea, teb, nu, xs, wg, wu, wd)


def _combine_kernel(off_ref, meta_ref, ys_ref, x_ref, gt2_ref, gf_ref, o_ref, y_ref, sem, *, final, n_ctx_tiles,
                    tiles_per_batch):
    def body():
        groups = x_ref.shape[-1] // (2 * LANES)
        for m in range(TM // LANES):
            for r in range(LANES):
                src = off_ref[meta_ref[m, 0, r]] + meta_ref[m, 1, r]
                _row_copy(ys_ref, src, y_ref, m * LANES + r, sem, groups, groups).start(priority=r % 2)

        pltpu.make_async_copy(ys_ref.at[pl.ds(0, TM * groups)], y_ref.at[pl.ds(0, TM * groups)], sem).wait()
        x2 = x_ref[...] + gt2_ref[0] * _unpack_pairs(_load_slabs(y_ref, TM, groups, groups)).astype(F32)
        if final:
            x2 = _rms(x2, x2.shape[-1]) * gf_ref[...]
        o_ref[...] = x2

    if final:
        pl.when(pl.program_id(0) % tiles_per_batch >= n_ctx_tiles)(body)
    else:
        body()


def _combine_call(off, meta, ys, x1, mod, gf, tiles_per_batch, n_ctx_tiles, bsz, final):
    t, d = x1.shape
    lat = tiles_per_batch - n_ctx_tiles
    mod_row = lambda i, off: (jnp.where(i % tiles_per_batch < n_ctx_tiles, bsz, i // tiles_per_batch), 0, 5)
    if final:
        out_rows = bsz * lat * TM
        out_ix = lambda i, off: ((i // tiles_per_batch) * lat + jnp.maximum(i % tiles_per_batch - n_ctx_tiles, 0), 0)
    else:
        out_rows = t
        out_ix = lambda i, off: (i, 0)
    return pl.pallas_call(
        functools.partial(_combine_kernel, final=final, n_ctx_tiles=n_ctx_tiles, tiles_per_batch=tiles_per_batch),
        grid_spec=pltpu.PrefetchScalarGridSpec(
            num_scalar_prefetch=1,
            grid=(t // TM,),
            in_specs=[
                pl.BlockSpec((TM // LANES, SUBLANES, LANES), lambda i, off: (i, 0, 0), memory_space=pltpu.SMEM),
                pl.BlockSpec(memory_space=pl.ANY),
                pl.BlockSpec((TM, d), lambda i, off: (i, 0)),
                pl.BlockSpec((1, 1, d), mod_row),
                pl.BlockSpec((1, d), lambda i, off: (0, 0)),
            ],
            out_specs=pl.BlockSpec((TM, d), out_ix),
            scratch_shapes=[pltpu.VMEM((TM * d // (2 * LANES), LANES), jnp.uint32), pltpu.SemaphoreType.DMA],
        ),
        out_shape=jax.ShapeDtypeStruct((out_rows, d), F32),
        compiler_params=_params(("arbitrary",)),
        name="moe_combine",
    )(off, meta, ys, x1, mod, gf)


def _moe_call(h2, meta, cnt, x1, mod, gf, wg, wu, wd, n_ctx_tiles, final):
    bsz, j, d = x1.shape
    t = bsz * j
    rows_max = -(-t // MOE_TG) * MOE_TG + N_BUCKETS * MOE_TG
    off, tea, teb, nu = _plan_call(cnt, rows_max // MOE_TG)
    xs = _dispatch_call(off, nu, h2, meta, cnt, rows_max, d // (2 * LANES) + 1)
    ys = _expert_call(tea, teb, nu, xs, wg, wu, wd)
    out = _combine_call(off, meta, ys, x1.reshape(t, d), mod, gf, j // TM, n_ctx_tiles, bsz, final)
    return out.reshape(bsz, -1, d)


def _rope_tables(ctx_len, seq, groups, shift):
    fidx = [0] * LANES
    cols = [False] * LANES
    role = [0] * LANES
    for start, use_cols in groups:
        for k in range(shift):
            for half in range(2):
                lane = start + half * shift + k
                fidx[lane], cols[lane], role[lane] = k, use_cols, half + 1
    fidx = jnp.asarray(fidx, F32)[None, :]
    cols = jnp.asarray(cols)[None, :]
    role = jnp.asarray(role, jnp.int32)[None, :]
    t = jnp.arange(ctx_len + seq, dtype=jnp.int32)[:, None] - ctx_len
    pos = jnp.where(cols, t % GRID_W, t // GRID_W).astype(F32)
    ang = pos * (ROPE_BASE ** (-fidx / shift))
    rot = (t >= 0) & (role > 0)
    c, s = jnp.cos(ang), jnp.sin(ang)
    return (jnp.where(rot, c, 1.0), jnp.where(rot & (role == 1), -s, 0.0), jnp.where(rot & (role == 2), s, 0.0))


def _block_diag(w):
    n, c, d = w.shape
    eye = jnp.eye(n, dtype=w.dtype)
    return (w[:, :, None, :] * eye[:, None, :, None]).reshape(n * c, n * d)


def _prep_layer(l, w_in, g_cq, w_uq, g_ckv, w_ukv, conv_w, conv_b, lru_wa, lru_ba, lru_wx, lru_bx,
                g_grp, w_out, w_g1, b_g1, w_g2, b_g2):
    d = w_in.shape[1]
    wi = w_in[l]
    zeros = lambda n: jnp.zeros((d, n), wi.dtype)
    win = jnp.concatenate([
        wi[:, 0:384], zeros(64), wi[:, 384:416], zeros(32), wi[:, 416:1952]], axis=1).astype(BF16)
    hq = MLA_NOPE + MLA_ROPE
    wuq = jnp.concatenate(
        [jnp.pad(w_uq[l][:, h * hq:(h + 1) * hq], ((0, 0), (0, LANES - hq))) for h in range(MLA_HEADS)],
        axis=1).astype(BF16)
    wkv = w_ukv[l].reshape(MLA_KV_RANK, MLA_HEADS, MLA_NOPE + MLA_V)
    wuk = jnp.pad(wkv[:, :, :MLA_NOPE], ((0, 0), (0, 0), (0, LANES - MLA_NOPE))).reshape(MLA_KV_RANK, -1)
    wuv = wkv[:, :, MLA_NOPE:].reshape(MLA_KV_RANK, -1)
    wukv = jnp.concatenate([wuk, wuv], axis=1).astype(BF16)
    lru = []
    for dr in range(2):
        lru.append(jnp.concatenate([_block_diag(lru_wa[l, dr]), _block_diag(lru_wx[l, dr])], axis=1).astype(BF16))
        lru.append(jnp.concatenate([lru_ba[l, dr], lru_bx[l, dr]])[None, :])
    wr = jnp.concatenate([jnp.moveaxis(w_g2[l], 0, 1).reshape(d, N_EXPERTS), w_g1[l]], axis=1)
    wr = jnp.pad(wr, ((0, 0), (0, LANES - wr.shape[1]))).astype(BF16)
    br = jnp.pad(jnp.concatenate([b_g2[l].reshape(-1), b_g1[l]]), (0, LANES - N_GROUPS - N_EXPERTS))[None, :]
    return dict(win=win, gcq=g_cq[l][None, :], wuq=wuq, gckv=g_ckv[l][None, :], wukv=wukv,
                cw=conv_w[l], cb=conv_b[l][None, :], wf=lru[0], bf=lru[1], wb=lru[2], bb=lru[3],
                gg=g_grp[l][None, :], wout=w_out[l].astype(BF16), wr=wr, br=br)


def kernel(x, c, ctx, c_ctx, w_ada, b_ada, g_norm1, g_norm2, w_in, g_cq, w_uq, g_ckv, w_ukv, swa_sink,
           conv_w, conv_b, lru_wa, lru_ba, lru_wx, lru_bx, lru_lam, g_grp, w_out, w_g1, b_g1, w_g2, b_g2,
           w_e_gate, w_e_up, w_e_down, g_final):
    bsz, seq, d = x.shape
    ctx_len = ctx.shape[1]
    depth = w_ada.shape[0]
    assert seq % TM == 0 and ctx_len % TM == 0 and seq % GRID_W == 0
    n_ctx_tiles = ctx_len // TM

    rows = -(-(bsz + 1) // SUBLANES) * SUBLANES
    cc = jnp.pad(jnp.concatenate([c, c_ctx[None, :]], axis=0), ((0, rows - bsz - 1), (0, 0)))
    mods = _ada_call(cc, w_ada, b_ada)

    q4 = MLA_ROPE // 4
    s4 = SWA_HEAD_DIM // 4
    mla_tabs = _rope_tables(ctx_len, seq, [(MLA_NOPE, False), (MLA_NOPE + 2 * q4, True)], q4)
    swa_tabs = _rope_tables(
        ctx_len, seq,
        [(hh * SWA_HEAD_DIM + ax * 2 * s4, bool(ax)) for hh in range(LANES // SWA_HEAD_DIM) for ax in range(2)], s4)
    tabs = mla_tabs + swa_tabs

    src = (ctx, x)
    for l in range(depth):
        last = l == depth - 1
        p = _prep_layer(l, w_in, g_cq, w_uq, g_ckv, w_ukv, conv_w, conv_b, lru_wa, lru_ba, lru_wx, lru_bx,
                        g_grp, w_out, w_g1, b_g1, w_g2, b_g2)
        mod = mods[l].reshape(rows, 1, 6 * d)
        qm, km, vm, sq, sk, sv, lx, lg = _inproj_call(
            src, mod, g_norm1[l][None, :], p["win"], p["gcq"], p["wuq"], p["gckv"], p["wukv"], tabs, n_ctx_tiles)
        om = _mla_call(qm, km, vm, n_ctx_tiles, ctx_len)
        osw = _swa_call(swa_sink[l], sq, sk, sv, ctx_len)
        ol = _lru_call(lx, lg, p["cw"], p["cb"], p["wf"], p["bf"], p["wb"], p["bb"], lru_lam[l], n_ctx_tiles)
        x1, h2, meta, cnt = _outproj_call(om, osw, ol, src, mod, p["gg"], p["wout"], g_norm2[l][None, :],
                                          p["wr"], p["br"], n_ctx_tiles, last)
        src = _moe_call(h2, meta, cnt, x1, mod, g_final[None, :], w_e_gate[l].astype(BF16),
                        w_e_up[l].astype(BF16), w_e_down[l].astype(BF16), 0 if last else n_ctx_tiles, last)
    return src
```

```python
import functools

import jax
import jax.numpy as jnp
from jax import lax
from jax.experimental import pallas as pl
from jax.experimental.pallas import tpu as pltpu

GRID_W = 64
EPS = 1e-6
ROPE_BASE = 10000.0
NEG_INF = -1e30
MLA_HEADS = 4
MLA_NOPE = 64
MLA_ROPE = 32
MLA_V = 64
MLA_Q_RANK = 256
MLA_KV_RANK = 128
MLA_SCALE = (MLA_NOPE + MLA_ROPE) ** -0.5
SWA_HEADS = 4
SWA_KV_HEADS = 2
SWA_HEAD_DIM = 64
SWA_SCALE = SWA_HEAD_DIM ** -0.5
WINDOW = 128
LRU_WIDTH = 512
LRU_BLOCKS = 8
LRU_BW = LRU_WIDTH // LRU_BLOCKS
CONV_W = 4
LRU_C = 8.0
N_GROUPS = 4
EXPERTS_PER_GROUP = 8
N_EXPERTS = N_GROUPS * EXPERTS_PER_GROUP
D_EXPERT = 256
MLA_OUT = MLA_HEADS * MLA_V
SWA_OUT = SWA_HEADS * SWA_HEAD_DIM

LANES = 128
SUBLANES = 8
TM = 256
ZW = 2048
N_PAIRS = EXPERTS_PER_GROUP * (EXPERTS_PER_GROUP - 1) // 2
N_BUCKETS = N_GROUPS * N_PAIRS
MOE_TG = 256
MOE_TD = 512
SWA_BATCH = 4
LRU_BATCH = 4
LOG2E = 1.4426950408889634
VMEM_LIMIT = 56 * 1024 * 1024

F32 = jnp.float32
BF16 = jnp.bfloat16


def _params(sem):
    return pltpu.CompilerParams(dimension_semantics=sem, vmem_limit_bytes=VMEM_LIMIT)


def _full(shape):
    n = len(shape)
    return pl.BlockSpec(shape, lambda *_: (0,) * n)


def _rms(x, width):
    return x * lax.rsqrt(jnp.sum(x * x, axis=-1, keepdims=True) * (1.0 / width) + EPS)


def _dot(a, b):
    return jnp.dot(a, b, preferred_element_type=F32)


def _sigmoid(x):
    return 0.5 * jnp.tanh(0.5 * x) + 0.5


def _pack_pairs(x):
    w = x.shape[-1] // 2
    bits = pltpu.bitcast(x.astype(F32), jnp.uint32)
    return (bits[:, w:] & jnp.uint32(0xFFFF0000)) | (bits[:, :w] >> 16)


def _unpack_pairs(u):
    lo = pltpu.bitcast(u << 16, F32)
    hi = pltpu.bitcast(u & jnp.uint32(0xFFFF0000), F32)
    return jnp.concatenate([lo, hi], axis=-1).astype(BF16)


def _store_slabs(ref, x, pitch=SUBLANES):
    n, w = x.shape
    for s in range(pitch):
        piece = x[:, s * LANES:(s + 1) * LANES] if (s + 1) * LANES <= w else jnp.zeros((n, LANES), x.dtype)
        ref[pl.ds(s, n, stride=pitch), :] = piece


def _load_slabs(ref, n, k, pitch=SUBLANES):
    return jnp.concatenate([ref[pl.ds(s, n, stride=pitch), :] for s in range(k)], axis=-1)


def _dot_nt(a, b):
    return lax.dot_general(a, b, (((1,), (1,)), ((), ())), preferred_element_type=F32)


def _ada_kernel(c_ref, w_ref, b_ref, o_ref):
    c = c_ref[...]
    s = c * jax.nn.sigmoid(c)
    o_ref[0] = jnp.dot(s, w_ref[0], preferred_element_type=F32,
                       precision=lax.Precision.HIGHEST) + b_ref[0]


def _ada_call(cc, w_ada, b_ada):
    depth, d, n = w_ada.shape
    r = cc.shape[0]
    tn = 1536
    return pl.pallas_call(
        _ada_kernel,
        grid=(depth, n // tn),
        in_specs=[
            pl.BlockSpec((r, d), lambda l, j: (0, 0)),
            pl.BlockSpec((1, d, tn), lambda l, j: (l, 0, j)),
            pl.BlockSpec((1, 1, tn), lambda l, j: (l, 0, j)),
        ],
        out_specs=pl.BlockSpec((1, r, tn), lambda l, j: (l, 0, j)),
        out_shape=jax.ShapeDtypeStruct((depth, r, n), F32),
        compiler_params=_params(("arbitrary", "arbitrary")),
        name="adaln",
    )(cc, w_ada, b_ada.reshape(depth, 1, n))


def _rope(x, cos, sina, sinb, shift):
    n = x.shape[-1]
    reps = n // LANES
    if reps > 1:
        cos = jnp.concatenate([cos] * reps, axis=-1)
        sina = jnp.concatenate([sina] * reps, axis=-1)
        sinb = jnp.concatenate([sinb] * reps, axis=-1)
    return x * cos + pltpu.roll(x, n - shift, 1) * sina + pltpu.roll(x, shift, 1) * sinb


def _inproj_kernel(*refs, n_src, n_ctx_tiles):
    (sh_ref, sc_ref, g1_ref, win_ref, gcq_ref, wuq_ref, gckv_ref, wukv_ref,
     mcos_ref, msa_ref, msb_ref, scos_ref, ssa_ref, ssb_ref,
     qm_ref, km_ref, vm_ref, sq_ref, sk_ref, sv_ref, lx_ref, lg_ref) = refs[n_src:]
    x = _read_stream(refs[:n_src], n_ctx_tiles)
    d = x.shape[-1]
    h = _rms(x, d) * g1_ref[...] * (1.0 + sc_ref[0]) + sh_ref[0]
    z = _dot(h.astype(BF16), win_ref[...])

    mcos, msa, msb = mcos_ref[...], msa_ref[...], msb_ref[...]
    scos, ssa, ssb = scos_ref[...], ssa_ref[...], ssb_ref[...]

    cq = _rms(z[:, 0:256], MLA_Q_RANK) * gcq_ref[...]
    q = _dot(cq.astype(BF16), wuq_ref[...])
    q = _rope(q, mcos, msa, msb, MLA_ROPE // 4)
    qm_ref[0] = (q * (MLA_SCALE * LOG2E)).astype(BF16)

    ckv = _rms(z[:, 256:384], MLA_KV_RANK) * gckv_ref[...]
    kv = _dot(ckv.astype(BF16), wukv_ref[...])
    kr = _rope(z[:, 384:512], mcos, msa, msb, MLA_ROPE // 4)
    km_ref[0] = (kv[:, 0:512] + jnp.concatenate([kr] * MLA_HEADS, axis=-1)).astype(BF16)
    ones = jnp.ones((kv.shape[0], LANES), F32)
    vm_ref[0] = jnp.concatenate([kv[:, 512:640], ones, kv[:, 640:768], ones], axis=-1).astype(BF16)

    sq_ref[0] = (_rope(z[:, 512:768], scos, ssa, ssb, SWA_HEAD_DIM // 4) * (SWA_SCALE * LOG2E)).astype(BF16)
    sk = _rope(z[:, 768:896], scos, ssa, ssb, SWA_HEAD_DIM // 4)
    sv = z[:, 896:1024]
    lane = lax.broadcasted_iota(jnp.int32, sk.shape, 1)
    low = lane < SWA_HEAD_DIM
    k0 = jnp.where(low, sk, 0.0)
    k1 = jnp.where(low, 0.0, sk)
    sk_ref[0] = jnp.concatenate([k0 + pltpu.roll(k0, SWA_HEAD_DIM, 1), k1 + pltpu.roll(k1, SWA_HEAD_DIM, 1)],
                                axis=-1).astype(BF16)
    v0 = jnp.where(low, sv, 0.0)
    v1 = jnp.where(low, 0.0, sv)
    one_hi = jnp.where(low, 0.0, 1.0)
    sv_ref[0] = jnp.concatenate([v0 + one_hi, pltpu.roll(v1, SWA_HEAD_DIM, 1) + one_hi], axis=-1).astype(BF16)
    lx_ref[0] = z[:, 1024:1536].astype(BF16)
    lg_ref[0] = z[:, 1536:2048].astype(BF16)


def _mod_spec(j, b_rows, n_ctx_tiles, d, skip=0):
    return pl.BlockSpec((1, 1, d), lambda i, b: (jnp.where(i + skip < n_ctx_tiles, b_rows, b), 0, j))


def _stream_specs(src, n_ctx_tiles, d, skip=0):
    if not isinstance(src, tuple):
        return [pl.BlockSpec((1, TM, d), lambda i, b: (b, i + skip, 0))], [src]
    ctx, lat = src
    lat_spec = pl.BlockSpec(
        (1, TM, d), lambda i, b: (jnp.where(i + skip < n_ctx_tiles, 0, b), jnp.maximum(i + skip - n_ctx_tiles, 0), 0))
    if skip >= n_ctx_tiles:
        return [lat_spec], [lat]
    ctx_spec = pl.BlockSpec(
        (1, TM, d), lambda i, b: (jnp.where(i + skip < n_ctx_tiles, b, 0), jnp.minimum(i + skip, n_ctx_tiles - 1), 0))
    return [ctx_spec, lat_spec], [ctx, lat]


def _read_stream(refs, n_ctx_tiles, skip=0):
    if len(refs) == 1:
        return refs[0][0]
    return jnp.where(pl.program_id(0) + skip < n_ctx_tiles, refs[0][0], refs[1][0])


def _inproj_call(src, mod, g1, win, gcq, wuq, gckv, wukv, tabs, n_ctx_tiles):
    bsz, _, d = src[1].shape if isinstance(src, tuple) else src.shape
    j = tabs[0].shape[0]
    nt = j // TM
    x_specs, x_arrays = _stream_specs(src, n_ctx_tiles, d)
    tok = lambda w: pl.BlockSpec((1, TM, w), lambda i, b: (b, i, 0))
    tab = pl.BlockSpec((TM, LANES), lambda i, b: (i, 0))
    widths = (512, 512, 512, 256, 256, 256, 512, 512)
    return pl.pallas_call(
        functools.partial(_inproj_kernel, n_src=len(x_arrays), n_ctx_tiles=n_ctx_tiles),
        grid=(nt, bsz),
        in_specs=x_specs + [
            _mod_spec(0, bsz, n_ctx_tiles, d),
            _mod_spec(1, bsz, n_ctx_tiles, d),
            _full(g1.shape), _full(win.shape), _full(gcq.shape), _full(wuq.shape),
            _full(gckv.shape), _full(wukv.shape),
            tab, tab, tab, tab, tab, tab,
        ],
        out_specs=[tok(w) for w in widths],
        out_shape=[jax.ShapeDtypeStruct((bsz, j, w), BF16) for w in widths],
        compiler_params=_params(("arbitrary", "arbitrary")),
        name="inproj",
    )(*x_arrays, mod, mod, g1, win, gcq, wuq, gckv, wukv, *tabs)


def _mla_heads(q_ref, k_ref, v_ref, o_ref, klen):
    lane = lax.broadcasted_iota(jnp.int32, (q_ref.shape[1], LANES), 1)
    for hp in range(MLA_HEADS // 2):
        v = v_ref[0, 0:klen, 2 * hp * LANES:2 * (hp + 1) * LANES]
        outs = []
        for h in range(2 * hp, 2 * hp + 2):
            q = q_ref[0, :, h * LANES:(h + 1) * LANES]
            k = k_ref[0, 0:klen, h * LANES:(h + 1) * LANES]
            s = _dot_nt(q, k)
            m = jnp.max(s, axis=-1, keepdims=True)
            o = _dot(jnp.exp2(s - m).astype(BF16), v)
            outs.append(o[:, 0:LANES] / o[:, LANES:LANES + 1])
        o_ref[0, :, hp * LANES:(hp + 1) * LANES] = jnp.where(lane < MLA_V, outs[0], outs[1]).astype(o_ref.dtype)


def _mla_kernel(q_ref, k_ref, v_ref, o_ref, *, n_ctx_tiles, ctx_len):
    i = pl.program_id(1)

    @pl.when(i < n_ctx_tiles)
    def _():
        _mla_heads(q_ref, k_ref, v_ref, o_ref, ctx_len)

    @pl.when(i >= n_ctx_tiles)
    def _():
        _mla_heads(q_ref, k_ref, v_ref, o_ref, k_ref.shape[1])


def _mla_call(qm, km, vm, n_ctx_tiles, ctx_len):
    bsz, j, w = qm.shape
    nt = j // TM
    return pl.pallas_call(
        functools.partial(_mla_kernel, n_ctx_tiles=n_ctx_tiles, ctx_len=ctx_len),
        grid=(bsz, nt),
        in_specs=[
            pl.BlockSpec((1, TM, w), lambda b, i: (b, i, 0)),
            pl.BlockSpec((1, j, w), lambda b, i: (b, 0, 0)),
            pl.BlockSpec((1, j, w), lambda b, i: (b, 0, 0)),
        ],
        out_specs=pl.BlockSpec((1, TM, MLA_OUT), lambda b, i: (b, i, 0)),
        out_shape=jax.ShapeDtypeStruct((bsz, j, MLA_OUT), BF16),
        compiler_params=_params(("arbitrary", "arbitrary")),
        name="mla_attn",
    )(qm, km, vm)


def _swa_kernel(sink_ref, q_ref, k_ref, v_ref, o_ref, *, ctx_len):
    i = pl.program_id(1)
    j = k_ref.shape[1]
    span = TM + 2 * WINDOW
    row0 = pl.multiple_of(jnp.clip(i * TM - WINDOW, 0, j - span), WINDOW)
    qpos = i * TM - ctx_len + lax.broadcasted_iota(jnp.int32, (TM, span), 0)
    kpos = row0 - ctx_len + lax.broadcasted_iota(jnp.int32, (TM, span), 1)
    valid = (jnp.abs(qpos - kpos) <= WINDOW) & (kpos >= 0) & (qpos >= 0)
    valid2 = jnp.concatenate([valid, valid], axis=0)
    low = lax.broadcasted_iota(jnp.int32, (TM, LANES), 1) < SWA_HEAD_DIM
    top = lax.broadcasted_iota(jnp.int32, (2 * TM, 1), 0) < TM
    for bb, kvh in [(bb, kvh) for bb in range(q_ref.shape[0]) for kvh in range(SWA_KV_HEADS)]:
        kcols = slice(kvh * LANES, (kvh + 1) * LANES)
        vcols = kcols
        qpair = q_ref[bb, :, kcols]
        zero = jnp.zeros_like(qpair)
        q2 = jnp.concatenate([jnp.where(low, qpair, zero), jnp.where(low, zero, qpair)], axis=0)
        s_loc = jnp.where(valid2, _dot_nt(q2, k_ref[bb, pl.ds(row0, span), kcols]), NEG_INF)
        s_ctx = _dot_nt(q2, k_ref[bb, 0:ctx_len, kcols])
        sink = jnp.where(top, sink_ref[2 * kvh], sink_ref[2 * kvh + 1]) * LOG2E
        m = jnp.maximum(jnp.maximum(jnp.max(s_loc, axis=-1, keepdims=True),
                                    jnp.max(s_ctx, axis=-1, keepdims=True)), sink)
        o = (_dot(jnp.exp2(s_loc - m).astype(BF16), v_ref[bb, pl.ds(row0, span), vcols])
             + _dot(jnp.exp2(s_ctx - m).astype(BF16), v_ref[bb, 0:ctx_len, vcols]))
        res = o / (o[:, SWA_HEAD_DIM:SWA_HEAD_DIM + 1] + jnp.exp2(sink - m))
        o_ref[bb, :, kcols] = jnp.where(low, res[0:TM], pltpu.roll(res[TM:2 * TM], SWA_HEAD_DIM, 1)
                                       ).astype(o_ref.dtype)


def _swa_call(sink, sq, sk, sv, ctx_len):
    bsz, j, _ = sq.shape
    nt = j // TM
    nb = SWA_BATCH if bsz % SWA_BATCH == 0 else 1
    return pl.pallas_call(
        functools.partial(_swa_kernel, ctx_len=ctx_len),
        grid=(bsz // nb, nt),
        in_specs=[
            pl.BlockSpec(memory_space=pltpu.SMEM),
            pl.BlockSpec((nb, TM, SWA_OUT), lambda b, i: (b, i, 0)),
            pl.BlockSpec((nb, j, sk.shape[-1]), lambda b, i: (b, 0, 0)),
            pl.BlockSpec((nb, j, sv.shape[-1]), lambda b, i: (b, 0, 0)),
        ],
        out_specs=pl.BlockSpec((nb, TM, SWA_OUT), lambda b, i: (b, i, 0)),
        out_shape=jax.ShapeDtypeStruct((bsz, j, SWA_OUT), BF16),
        compiler_params=_params(("arbitrary", "arbitrary")),
        name="swa_attn",
    )(sink, sq, sk, sv)


def _lru_gates(bb, prev_ref, cur_ref, next_ref, cw_ref, cb_ref, w_ref, b_ref, sp_ref, seg_start, seg_end):
    z = cur_ref[bb].astype(F32)
    row = lax.broadcasted_iota(jnp.int32, z.shape, 0)
    keep_prev = jnp.where(seg_start, 0.0, 1.0)
    keep_next = jnp.where(seg_end, 0.0, 1.0)
    p2 = prev_ref[bb, TM - 2:TM - 1, :].astype(F32) * keep_prev
    p1 = prev_ref[bb, TM - 1:TM, :].astype(F32) * keep_prev
    n0 = next_ref[bb, 0:1, :].astype(F32) * keep_next
    z_m1 = jnp.where(row == 0, p1, pltpu.roll(z, 1, 0))
    z_m2 = jnp.where(row == 0, p2, jnp.where(row == 1, p1, pltpu.roll(z, 2, 0)))
    z_p1 = jnp.where(row == TM - 1, n0, pltpu.roll(z, TM - 1, 0))
    u = cb_ref[...] + z_m2 * cw_ref[0:1, :]
    u = u + z_m1 * cw_ref[1:2, :]
    u = u + z * cw_ref[2:3, :]
    u = u + z_p1 * cw_ref[3:4, :]
    g = _dot(u.astype(BF16), w_ref[...]) + b_ref[...]
    r = _sigmoid(g[:, 0:LRU_WIDTH])
    ig = _sigmoid(g[:, LRU_WIDTH:2 * LRU_WIDTH])
    log_a = (-LRU_C) * r * sp_ref[...]
    a = jnp.exp(log_a)
    om = 1.0 - a * a
    bt = jnp.where(om > 0.0, om * lax.rsqrt(om), 0.0) * (ig * u)
    return a, bt


def _lru_scan(a, b, h0, reverse):
    t, w = a.shape
    g = t // SUBLANES
    a = a.reshape(g, SUBLANES, w)
    b = b.reshape(g, SUBLANES, w)
    sub = lax.broadcasted_iota(jnp.int32, (g, SUBLANES, w), 1)
    d = 1
    while d < SUBLANES:
        if reverse:
            shift, ok = SUBLANES - d, sub < SUBLANES - d
        else:
            shift, ok = d, sub >= d
        a_sh = pltpu.roll(a, shift, 1)
        b_sh = pltpu.roll(b, shift, 1)
        b = jnp.where(ok, a * b_sh + b, b)
        a = jnp.where(ok, a * a_sh, a)
        d *= 2
    hs = [None] * g
    h = h0
    order = range(g - 1, -1, -1) if reverse else range(g)
    for gi in order:
        hg = a[gi] * h + b[gi]
        hs[gi] = hg
        h = hg[0:1, :] if reverse else hg[SUBLANES - 1:SUBLANES, :]
    return jnp.concatenate(hs, axis=0), h


def _softplus_neg(lam):
    x = -lam
    return jnp.maximum(x, 0.0) + jnp.log1p(jnp.exp(-jnp.abs(x)))


def _lru_fwd_kernel(prev_ref, cur_ref, next_ref, cw_ref, cb_ref, w_ref, b_ref, lam_ref,
                    hf_ref, carry_ref, sp_ref, *, n_ctx_tiles):
    c = pl.program_id(1)
    nt = pl.num_programs(1)

    @pl.when(c == 0)
    def _():
        carry_ref[...] = jnp.zeros_like(carry_ref)
        sp_ref[...] = _softplus_neg(lam_ref[...])

    seg_start = (c == 0) | (c == n_ctx_tiles)
    seg_end = (c == n_ctx_tiles - 1) | (c == nt - 1)
    for bb in range(cur_ref.shape[0]):
        a, bt = _lru_gates(bb, prev_ref, cur_ref, next_ref, cw_ref, cb_ref, w_ref, b_ref, sp_ref,
                           seg_start, seg_end)
        hs, h = _lru_scan(a, bt, carry_ref[bb], False)
        carry_ref[bb] = h
        hf_ref[bb] = hs.astype(hf_ref.dtype)


def _bwd_chunk(s, nt, n_ctx_tiles):
    return jnp.where(s < n_ctx_tiles, n_ctx_tiles - 1 - s, nt - 1 - (s - n_ctx_tiles))


def _lru_bwd_kernel(prev_ref, cur_ref, next_ref, cw_ref, cb_ref, w_ref, b_ref, lam_ref,
                    hf_ref, lg_ref, o_ref, carry_ref, sp_ref, *, n_ctx_tiles):
    s = pl.program_id(1)
    nt = pl.num_programs(1)
    c = _bwd_chunk(s, nt, n_ctx_tiles)

    @pl.when(s == 0)
    def _():
        carry_ref[...] = jnp.zeros_like(carry_ref)
        sp_ref[...] = _softplus_neg(lam_ref[...])

    seg_start = (c == 0) | (c == n_ctx_tiles)
    seg_end = (c == n_ctx_tiles - 1) | (c == nt - 1)
    for bb in range(cur_ref.shape[0]):
        a, bt = _lru_gates(bb, prev_ref, cur_ref, next_ref, cw_ref, cb_ref, w_ref, b_ref, sp_ref,
                           seg_start, seg_end)
        hs, h = _lru_scan(a, bt, carry_ref[bb], True)
        carry_ref[bb] = h
        gate = jax.nn.gelu(lg_ref[bb].astype(F32), approximate=True)
        o_ref[bb] = ((hf_ref[bb].astype(F32) + hs) * gate).astype(o_ref.dtype)


def _lru_call(lx, lg, cw, cb, wf, bf, wb, bb, lam, n_ctx_tiles):
    bsz, j, w = lx.shape
    nt = j // TM
    consts = [_full(cw.shape), _full((1, w)), _full(wf.shape), _full(bf.shape), _full((1, w))]
    nb = LRU_BATCH if bsz % LRU_BATCH == 0 else 1
    scratch = [pltpu.VMEM((nb, 1, w), F32), pltpu.VMEM((1, w), F32)]
    blk = lambda f: pl.BlockSpec((nb, TM, w), f)
    hf = pl.pallas_call(
        functools.partial(_lru_fwd_kernel, n_ctx_tiles=n_ctx_tiles),
        grid=(bsz // nb, nt),
        in_specs=[
            blk(lambda b, c: (b, jnp.maximum(c - 1, 0), 0)),
            blk(lambda b, c: (b, c, 0)),
            blk(lambda b, c: (b, jnp.minimum(c + 1, nt - 1), 0)),
        ] + consts,
        out_specs=blk(lambda b, c: (b, c, 0)),
        out_shape=jax.ShapeDtypeStruct((bsz, j, w), BF16),
        scratch_shapes=scratch,
        compiler_params=_params(("arbitrary", "arbitrary")),
        name="lru_fwd",
    )(lx, lx, lx, cw, cb, wf, bf, lam[0:1])
    cix = lambda s: _bwd_chunk(s, nt, n_ctx_tiles)
    return pl.pallas_call(
        functools.partial(_lru_bwd_kernel, n_ctx_tiles=n_ctx_tiles),
        grid=(bsz // nb, nt),
        in_specs=[
            blk(lambda b, s: (b, jnp.maximum(cix(s) - 1, 0), 0)),
            blk(lambda b, s: (b, cix(s), 0)),
            blk(lambda b, s: (b, jnp.minimum(cix(s) + 1, nt - 1), 0)),
        ] + consts + [
            blk(lambda b, s: (b, cix(s), 0)),
            blk(lambda b, s: (b, cix(s), 0)),
        ],
        out_specs=blk(lambda b, s: (b, cix(s), 0)),
        out_shape=jax.ShapeDtypeStruct((bsz, j, w), BF16),
        scratch_shapes=scratch,
        compiler_params=_params(("arbitrary", "arbitrary")),
        name="lru_bwd",
    )(lx, lx, lx, cw, cb, wb, bb, lam[1:2], hf, lg)


def _route(logits, carry):
    lt = logits.T
    tm = lt.shape[1]
    big = jnp.int32(LANES)
    lg = lt[N_EXPERTS:N_EXPERTS + SUBLANES]
    rg = lax.broadcasted_iota(jnp.int32, lg.shape, 0)
    is_g = rg < N_GROUPS
    mg = jnp.max(jnp.where(is_g, lg, -jnp.inf), axis=0, keepdims=True)
    eg = jnp.where(is_g, jnp.exp(lg - mg), 0.0)
    pg = eg / jnp.sum(eg, axis=0, keepdims=True)
    pg_top = jnp.max(pg, axis=0, keepdims=True)
    g_idx = jnp.min(jnp.where(is_g & (pg == pg_top), rg, big), axis=0, keepdims=True)
    le = lt[0:N_EXPERTS]
    re = lax.broadcasted_iota(jnp.int32, le.shape, 0)
    lo = g_idx * EXPERTS_PER_GROUP
    sel = (re >= lo) & (re < lo + EXPERTS_PER_GROUP)
    me = jnp.max(jnp.where(sel, le, -jnp.inf), axis=0, keepdims=True)
    ee = jnp.where(sel, jnp.exp(le - me), 0.0)
    pe = ee / jnp.sum(ee, axis=0, keepdims=True)
    p1 = jnp.max(jnp.where(sel, pe, -1.0), axis=0, keepdims=True)
    i1 = jnp.min(jnp.where(sel & (pe == p1), re, big), axis=0, keepdims=True)
    sel2 = sel & (re != i1)
    p2 = jnp.max(jnp.where(sel2, pe, -1.0), axis=0, keepdims=True)
    i2 = jnp.min(jnp.where(sel2 & (pe == p2), re, big), axis=0, keepdims=True)
    den = p1 + p2
    w1 = pg_top * p1 / den
    w2 = pg_top * p2 / den
    first_lo = i1 < i2
    ia = (jnp.where(first_lo, i1, i2) - lo).astype(F32)
    ib = (jnp.where(first_lo, i2, i1) - lo).astype(F32)
    wa = jnp.where(first_lo, w1, w2)
    wb = jnp.where(first_lo, w2, w1)
    pair = ia * (2 * EXPERTS_PER_GROUP - 1 - ia) * 0.5 + (ib - ia - 1.0)
    bucket = (g_idx.astype(F32) * N_PAIRS + pair).astype(jnp.int32)
    rb = lax.broadcasted_iota(jnp.int32, (LANES, tm), 0)
    mine = rb == bucket
    picks = jnp.where(mine, 1.0, 0.0)
    tri = jnp.where(lax.broadcasted_iota(jnp.int32, (tm, tm), 0) < lax.broadcasted_iota(jnp.int32, (tm, tm), 1),
                    1.0, 0.0).astype(BF16)
    before = _dot(picks.astype(BF16), tri) + carry
    rank = jnp.sum(jnp.where(mine, before, 0.0), axis=0, keepdims=True)
    zeros = jnp.zeros((SUBLANES - 2, tm), F32)
    meta = jnp.concatenate([bucket.astype(F32), rank, zeros], axis=0).astype(jnp.int32)
    wrows = jnp.concatenate([wa, wb, jnp.zeros((LANES - 2, tm), F32)], axis=0)
    return meta, wrows.T, carry + jnp.sum(picks, axis=1, keepdims=True)


def _outproj_kernel(*refs, n_src, n_ctx_tiles, skip):
    (om_ref, os_ref, ol_ref, gt1_ref, sh2_ref, sc2_ref, gg_ref, wout_ref,
     g2_ref, wr_ref, br_ref, x1_ref, h2_ref, meta_ref, cnt_ref, carry_ref) = refs[n_src:]
    x = _read_stream(refs[:n_src], n_ctx_tiles, skip)
    d = x.shape[-1]

    @pl.when((pl.program_id(0) == 0) & (pl.program_id(1) == 0))
    def _():
        carry_ref[...] = jnp.zeros_like(carry_ref)

    gg = gg_ref[...]
    nm = _rms(om_ref[0].astype(F32), MLA_OUT) * gg[:, 0:MLA_OUT]
    ns = _rms(os_ref[0].astype(F32), SWA_OUT) * gg[:, MLA_OUT:MLA_OUT + SWA_OUT]
    nl = _rms(ol_ref[0].astype(F32), LRU_WIDTH) * gg[:, MLA_OUT + SWA_OUT:]
    merged = jnp.concatenate([nm, ns, nl], axis=-1).astype(BF16)
    mix = _dot(merged, wout_ref[...])
    x1 = x + gt1_ref[0] * mix
    x1_ref[0] = x1
    h2 = (_rms(x1, d) * g2_ref[...] * (1.0 + sc2_ref[0]) + sh2_ref[0]).astype(BF16)
    meta, wcols, carry = _route(_dot(h2, wr_ref[...]) + br_ref[...], carry_ref[...])
    _store_slabs(h2_ref, jnp.concatenate([_pack_pairs(h2), pltpu.bitcast(wcols, jnp.uint32)], axis=-1))
    carry_ref[...] = carry
    cnt_ref[...] = jnp.broadcast_to(carry, (LANES, LANES)).T[0:1, :].astype(jnp.int32)
    for m in range(TM // LANES):
        meta_ref[m] = meta[:, m * LANES:(m + 1) * LANES]


def _outproj_call(om, osw, ol, src, mod, gg, wout, g2, wr, br, n_ctx_tiles, latent_only):
    bsz, j_all, _ = om.shape
    d = wout.shape[1]
    skip = n_ctx_tiles if latent_only else 0
    nt = j_all // TM - skip
    j = nt * TM
    tok_in = lambda w: pl.BlockSpec((1, TM, w), lambda i, b: (b, i + skip, 0))
    tok = lambda w: pl.BlockSpec((1, TM, w), lambda i, b: (b, i, 0))
    x_specs, x_arrays = _stream_specs(src, n_ctx_tiles, d, skip)
    return pl.pallas_call(
        functools.partial(_outproj_kernel, n_src=len(x_arrays), n_ctx_tiles=n_ctx_tiles, skip=skip),
        grid=(nt, bsz),
        in_specs=x_specs + [
            tok_in(MLA_OUT), tok_in(SWA_OUT), tok_in(LRU_WIDTH),
            _mod_spec(2, bsz, n_ctx_tiles, d, skip),
            _mod_spec(3, bsz, n_ctx_tiles, d, skip),
            _mod_spec(4, bsz, n_ctx_tiles, d, skip),
            _full(gg.shape), _full(wout.shape), _full(g2.shape), _full(wr.shape), _full(br.shape),
        ],
        out_specs=[
            tok(d), pl.BlockSpec((TM * SUBLANES, LANES), lambda i, b: (b * nt + i, 0)),
            pl.BlockSpec((TM // LANES, SUBLANES, LANES), lambda i, b: (b * nt + i, 0, 0)),
            pl.BlockSpec((1, LANES), lambda i, b: (0, 0)),
        ],
        out_shape=[
            jax.ShapeDtypeStruct((bsz, j, d), F32),
            jax.ShapeDtypeStruct((bsz * j * SUBLANES, LANES), jnp.uint32),
            jax.ShapeDtypeStruct((bsz * j // LANES, SUBLANES, LANES), jnp.int32),
            jax.ShapeDtypeStruct((1, LANES), jnp.int32),
        ],
        scratch_shapes=[pltpu.VMEM((LANES, 1), F32)],
        compiler_params=_params(("arbitrary", "arbitrary")),
        name="outproj_route",
    )(*x_arrays, om, osw, ol, mod, mod, mod, gg, wout, g2, wr, br)


def _padded_tiles(count):
    return lax.shift_right_logical(count + (MOE_TG - 1), MOE_TG.bit_length() - 1)


def _plan_kernel(cnt_ref, off_ref, tea_ref, teb_ref, nu_ref, *, n_tiles_max):
    off = jnp.int32(0)
    ti = jnp.int32(0)
    bucket = 0
    for g in range(N_GROUPS):
        for a in range(EXPERTS_PER_GROUP):
            for b in range(a + 1, EXPERTS_PER_GROUP):
                n_q = _padded_tiles(cnt_ref[0, bucket])
                off_ref[bucket] = off

                def fill(k, _, base=ti, ea=g * EXPERTS_PER_GROUP + a, eb=g * EXPERTS_PER_GROUP + b):
                    tea_ref[base + k] = ea
                    teb_ref[base + k] = eb
                    return 0

                lax.fori_loop(0, n_q, fill, 0)
                off = off + n_q * MOE_TG
                ti = ti + n_q
                bucket += 1
    nu_ref[0] = ti

    def rest(k, _):
        tea_ref[k] = N_EXPERTS - 2
        teb_ref[k] = N_EXPERTS - 1
        return 0

    lax.fori_loop(ti, n_tiles_max, rest, 0)


def _plan_call(cnt, n_tiles_max):
    smem = pl.BlockSpec(memory_space=pltpu.SMEM)
    return pl.pallas_call(
        functools.partial(_plan_kernel, n_tiles_max=n_tiles_max),
        in_specs=[smem],
        out_specs=[smem, smem, smem, smem],
        out_shape=[
            jax.ShapeDtypeStruct((N_BUCKETS,), jnp.int32),
            jax.ShapeDtypeStruct((n_tiles_max,), jnp.int32),
            jax.ShapeDtypeStruct((n_tiles_max,), jnp.int32),
            jax.ShapeDtypeStruct((1,), jnp.int32),
        ],
        name="moe_plan",
    )(cnt)


def _row_copy(src_ref, src_row, dst_ref, dst_row, sem, used=SUBLANES, pitch=SUBLANES):
    slot = lambda row: pl.ds(pl.multiple_of(row * pitch, pitch), used)
    return pltpu.make_async_copy(src_ref.at[slot(src_row)], dst_ref.at[slot(dst_row)], sem)


def _dispatch_kernel(off_ref, nu_ref, h_ref, meta_ref, cnt_ref, xs_ref, zero_ref, sem, tile_sem, *, tile, used):
    tile_rows = MOE_TG * SUBLANES
    n_tiles_max = xs_ref.shape[0] // tile_rows

    @pl.when(pl.program_id(0) == 0)
    def _():
        zero_ref[...] = jnp.zeros_like(zero_ref)

        def per_bucket(q, n):
            c = cnt_ref[0, q]
            padded = _padded_tiles(c) * MOE_TG
            base = off_ref[q]

            def fill(r, _):
                _row_copy(zero_ref, 0, xs_ref, base + r, sem, used).start()
                return 0

            lax.fori_loop(c, padded, fill, 0)
            return n + (padded - c)

        n_pad = lax.fori_loop(0, N_BUCKETS, per_bucket, jnp.int32(0))

        def drain(r, _):
            _row_copy(zero_ref, 0, xs_ref, 0, sem, used).wait()
            return 0

        lax.fori_loop(0, n_pad, drain, 0)

        def unused_tile(ti, _):
            cp = pltpu.make_async_copy(
                zero_ref, xs_ref.at[pl.ds(pl.multiple_of(ti * tile_rows, tile_rows), tile_rows)], tile_sem)
            cp.start()
            cp.wait()
            return 0

        lax.fori_loop(nu_ref[0], n_tiles_max, unused_tile, 0)

    for m in range(tile // LANES):
        for r in range(LANES):
            dst = off_ref[meta_ref[m, 0, r]] + meta_ref[m, 1, r]
            _row_copy(h_ref, m * LANES + r, xs_ref, dst, sem, used).start(priority=r % 2)

    pltpu.make_async_copy(h_ref.at[pl.ds(0, tile * used)], xs_ref.at[pl.ds(0, tile * used)], sem).wait()


def _dispatch_call(off, nu, h2, meta, cnt, rows_max, used):
    t = h2.shape[0] // SUBLANES
    tile = MOE_TD if t % MOE_TD == 0 else TM
    return pl.pallas_call(
        functools.partial(_dispatch_kernel, tile=tile, used=used),
        grid_spec=pltpu.PrefetchScalarGridSpec(
            num_scalar_prefetch=2,
            grid=(t // tile,),
            in_specs=[
                pl.BlockSpec((tile * SUBLANES, LANES), lambda i, off, nu: (i, 0)),
                pl.BlockSpec((tile // LANES, SUBLANES, LANES), lambda i, off, nu: (i, 0, 0),
                             memory_space=pltpu.SMEM),
                pl.BlockSpec(memory_space=pltpu.SMEM),
            ],
            out_specs=pl.BlockSpec(memory_space=pl.ANY),
            scratch_shapes=[pltpu.VMEM((MOE_TG * SUBLANES, LANES), h2.dtype), pltpu.SemaphoreType.DMA,
                            pltpu.SemaphoreType.DMA],
        ),
        out_shape=jax.ShapeDtypeStruct((rows_max * SUBLANES, LANES), h2.dtype),
        compiler_params=_params(("arbitrary",)),
        name="moe_dispatch",
    )(off, nu, h2, meta, cnt)


def _expert_mlp(x, e, wg_ref, wu_ref, wd_ref):
    gate = _dot(x, wg_ref[e])
    act = gate * _sigmoid(gate) * _dot(x, wu_ref[e])
    return _dot(act.astype(BF16), wd_ref[e])


def _expert_kernel(tea_ref, teb_ref, nu_ref, x_ref, wg_ref, wu_ref, wd_ref, y_ref):
    groups = wg_ref.shape[1] // (2 * LANES)
    i = pl.program_id(0)

    @pl.when(i < nu_ref[0])
    def _():
        row = _load_slabs(x_ref, MOE_TG, groups + 1)
        x = _unpack_pairs(row[:, 0:groups * LANES])
        wts = pltpu.bitcast(row[:, groups * LANES:], F32)
        ea = tea_ref[i] & (EXPERTS_PER_GROUP - 1)
        eb = teb_ref[i] & (EXPERTS_PER_GROUP - 1)
        y = (wts[:, 0:1] * _expert_mlp(x, ea, wg_ref, wu_ref, wd_ref)
             + wts[:, 1:2] * _expert_mlp(x, eb, wg_ref, wu_ref, wd_ref))
        _store_slabs(y_ref, _pack_pairs(y.astype(BF16)), groups)

    @pl.when(pl.program_id(0) >= nu_ref[0])
    def _():
        y_ref[...] = jnp.zeros_like(y_ref)


def _expert_call(tea, teb, nu, xs, wg, wu, wd):
    d = wg.shape[1]
    ypitch = d // (2 * LANES)
    n_tiles = xs.shape[0] // (MOE_TG * SUBLANES)
    used = lambda i, nu: jnp.minimum(i, nu[0] - 1)
    shift = EXPERTS_PER_GROUP.bit_length() - 1
    wspec = lambda rows, cols: pl.BlockSpec(
        (EXPERTS_PER_GROUP, rows, cols), lambda i, tea, teb, nu: (tea[used(i, nu)] >> shift, 0, 0))
    return pl.pallas_call(
        _expert_kernel,
        grid_spec=pltpu.PrefetchScalarGridSpec(
            num_scalar_prefetch=3,
            grid=(n_tiles,),
            in_specs=[
                pl.BlockSpec((MOE_TG * SUBLANES, LANES), lambda i, tea, teb, nu: (used(i, nu), 0)),
                wspec(d, D_EXPERT), wspec(d, D_EXPERT), wspec(D_EXPERT, d),
            ],
            out_specs=pl.BlockSpec((MOE_TG * ypitch, LANES), lambda i, tea, teb, nu: (i, 0)),
        ),
        out_shape=jax.ShapeDtypeStruct((n_tiles * MOE_TG * ypitch, LANES), jnp.uint32),
        compiler_params=_params(("arbitrary",)),
        name="moe_experts",
    )(tea, teb, nu, xs, wg, wu, wd)


def _combine_kernel(off_ref, meta_ref, ys_ref, x_ref, gt2_ref, gf_ref, o_ref, y_ref, sem, *, final, n_ctx_tiles,
                    tiles_per_batch):
    def body():
        groups = x_ref.shape[-1] // (2 * LANES)
        for m in range(TM // LANES):
            for r in range(LANES):
                src = off_ref[meta_ref[m, 0, r]] + meta_ref[m, 1, r]
                _row_copy(ys_ref, src, y_ref, m * LANES + r, sem, groups, groups).start(priority=r % 2)

        pltpu.make_async_copy(ys_ref.at[pl.ds(0, TM * groups)], y_ref.at[pl.ds(0, TM * groups)], sem).wait()
        x2 = x_ref[...] + gt2_ref[0] * _unpack_pairs(_load_slabs(y_ref, TM, groups, groups)).astype(F32)
        if final:
            x2 = _rms(x2, x2.shape[-1]) * gf_ref[...]
        o_ref[...] = x2

    if final:
        pl.when(pl.program_id(0) % tiles_per_batch >= n_ctx_tiles)(body)
    else:
        body()


def _combine_call(off, meta, ys, x1, mod, gf, tiles_per_batch, n_ctx_tiles, bsz, final):
    t, d = x1.shape
    lat = tiles_per_batch - n_ctx_tiles
    mod_row = lambda i, off: (jnp.where(i % tiles_per_batch < n_ctx_tiles, bsz, i // tiles_per_batch), 0, 5)
    if final:
        out_rows = bsz * lat * TM
        out_ix = lambda i, off: ((i // tiles_per_batch) * lat + jnp.maximum(i % tiles_per_batch - n_ctx_tiles, 0), 0)
    else:
        out_rows = t
        out_ix = lambda i, off: (i, 0)
    return pl.pallas_call(
        functools.partial(_combine_kernel, final=final, n_ctx_tiles=n_ctx_tiles, tiles_per_batch=tiles_per_batch),
        grid_spec=pltpu.PrefetchScalarGridSpec(
            num_scalar_prefetch=1,
            grid=(t // TM,),
            in_specs=[
                pl.BlockSpec((TM // LANES, SUBLANES, LANES), lambda i, off: (i, 0, 0), memory_space=pltpu.SMEM),
                pl.BlockSpec(memory_space=pl.ANY),
                pl.BlockSpec((TM, d), lambda i, off: (i, 0)),
                pl.BlockSpec((1, 1, d), mod_row),
                pl.BlockSpec((1, d), lambda i, off: (0, 0)),
            ],
            out_specs=pl.BlockSpec((TM, d), out_ix),
            scratch_shapes=[pltpu.VMEM((TM * d // (2 * LANES), LANES), jnp.uint32), pltpu.SemaphoreType.DMA],
        ),
        out_shape=jax.ShapeDtypeStruct((out_rows, d), F32),
        compiler_params=_params(("arbitrary",)),
        name="moe_combine",
    )(off, meta, ys, x1, mod, gf)


def _moe_call(h2, meta, cnt, x1, mod, gf, wg, wu, wd, n_ctx_tiles, final):
    bsz, j, d = x1.shape
    t = bsz * j
    rows_max = -(-t // MOE_TG) * MOE_TG + N_BUCKETS * MOE_TG
    off, tea, teb, nu = _plan_call(cnt, rows_max // MOE_TG)
    xs = _dispatch_call(off, nu, h2, meta, cnt, rows_max, d // (2 * LANES) + 1)
    ys = _expert_call(tea, teb, nu, xs, wg, wu, wd)
    out = _combine_call(off, meta, ys, x1.reshape(t, d), mod, gf, j // TM, n_ctx_tiles, bsz, final)
    return out.reshape(bsz, -1, d)


def _rope_tables(ctx_len, seq, groups, shift):
    fidx = [0] * LANES
    cols = [False] * LANES
    role = [0] * LANES
    for start, use_cols in groups:
        for k in range(shift):
            for half in range(2):
                lane = start + half * shift + k
                fidx[lane], cols[lane], role[lane] = k, use_cols, half + 1
    fidx = jnp.asarray(fidx, F32)[None, :]
    cols = jnp.asarray(cols)[None, :]
    role = jnp.asarray(role, jnp.int32)[None, :]
    t = jnp.arange(ctx_len + seq, dtype=jnp.int32)[:, None] - ctx_len
    pos = jnp.where(cols, t % GRID_W, t // GRID_W).astype(F32)
    ang = pos * (ROPE_BASE ** (-fidx / shift))
    rot = (t >= 0) & (role > 0)
    c, s = jnp.cos(ang), jnp.sin(ang)
    return (jnp.where(rot, c, 1.0), jnp.where(rot & (role == 1), -s, 0.0), jnp.where(rot & (role == 2), s, 0.0))


def _block_diag(w):
    n, c, d = w.shape
    eye = jnp.eye(n, dtype=w.dtype)
    return (w[:, :, None, :] * eye[:, None, :, None]).reshape(n * c, n * d)


def _prep_layer(l, w_in, g_cq, w_uq, g_ckv, w_ukv, conv_w, conv_b, lru_wa, lru_ba, lru_wx, lru_bx,
                g_grp, w_out, w_g1, b_g1, w_g2, b_g2):
    d = w_in.shape[1]
    wi = w_in[l]
    zeros = lambda n: jnp.zeros((d, n), wi.dtype)
    win = jnp.concatenate([
        wi[:, 0:384], zeros(64), wi[:, 384:416], zeros(32), wi[:, 416:1952]], axis=1).astype(BF16)
    hq = MLA_NOPE + MLA_ROPE
    wuq = jnp.concatenate(
        [jnp.pad(w_uq[l][:, h * hq:(h + 1) * hq], ((0, 0), (0, LANES - hq))) for h in range(MLA_HEADS)],
        axis=1).astype(BF16)
    wkv = w_ukv[l].reshape(MLA_KV_RANK, MLA_HEADS, MLA_NOPE + MLA_V)
    wuk = jnp.pad(wkv[:, :, :MLA_NOPE], ((0, 0), (0, 0), (0, LANES - MLA_NOPE))).reshape(MLA_KV_RANK, -1)
    wuv = wkv[:, :, MLA_NOPE:].reshape(MLA_KV_RANK, -1)
    wukv = jnp.concatenate([wuk, wuv], axis=1).astype(BF16)
    lru = []
    for dr in range(2):
        lru.append(jnp.concatenate([_block_diag(lru_wa[l, dr]), _block_diag(lru_wx[l, dr])], axis=1).astype(BF16))
        lru.append(jnp.concatenate([lru_ba[l, dr], lru_bx[l, dr]])[None, :])
    wr = jnp.concatenate([jnp.moveaxis(w_g2[l], 0, 1).reshape(d, N_EXPERTS), w_g1[l]], axis=1)
    wr = jnp.pad(wr, ((0, 0), (0, LANES - wr.shape[1]))).astype(BF16)
    br = jnp.pad(jnp.concatenate([b_g2[l].reshape(-1), b_g1[l]]), (0, LANES - N_GROUPS - N_EXPERTS))[None, :]
    return dict(win=win, gcq=g_cq[l][None, :], wuq=wuq, gckv=g_ckv[l][None, :], wukv=wukv,
                cw=conv_w[l], cb=conv_b[l][None, :], wf=lru[0], bf=lru[1], wb=lru[2], bb=lru[3],
                gg=g_grp[l][None, :], wout=w_out[l].astype(BF16), wr=wr, br=br)


def kernel(x, c, ctx, c_ctx, w_ada, b_ada, g_norm1, g_norm2, w_in, g_cq, w_uq, g_ckv, w_ukv, swa_sink,
           conv_w, conv_b, lru_wa, lru_ba, lru_wx, lru_bx, lru_lam, g_grp, w_out, w_g1, b_g1, w_g2, b_g2,
           w_e_gate, w_e_up, w_e_down, g_final):
    bsz, seq, d = x.shape
    ctx_len = ctx.shape[1]
    depth = w_ada.shape[0]
    assert seq % TM == 0 and ctx_len % TM == 0 and seq % GRID_W == 0
    n_ctx_tiles = ctx_len // TM

    rows = -(-(bsz + 1) // SUBLANES) * SUBLANES
    cc = jnp.pad(jnp.concatenate([c, c_ctx[None, :]], axis=0), ((0, rows - bsz - 1), (0, 0)))
    mods = _ada_call(cc, w_ada, b_ada)

    q4 = MLA_ROPE // 4
    s4 = SWA_HEAD_DIM // 4
    mla_tabs = _rope_tables(ctx_len, seq, [(MLA_NOPE, False), (MLA_NOPE + 2 * q4, True)], q4)
    swa_tabs = _rope_tables(
        ctx_len, seq,
        [(hh * SWA_HEAD_DIM + ax * 2 * s4, bool(ax)) for hh in range(LANES // SWA_HEAD_DIM) for ax in range(2)], s4)
    tabs = mla_tabs + swa_tabs

    src = (ctx, x)
    for l in range(depth):
        last = l == depth - 1
        p = _prep_layer(l, w_in, g_cq, w_uq, g_ckv, w_ukv, conv_w, conv_b, lru_wa, lru_ba, lru_wx, lru_bx,
                        g_grp, w_out, w_g1, b_g1, w_g2, b_g2)
        mod = mods[l].reshape(rows, 1, 6 * d)
        qm, km, vm, sq, sk, sv, lx, lg = _inproj_call(
            src, mod, g_norm1[l][None, :], p["win"], p["gcq"], p["wuq"], p["gckv"], p["wukv"], tabs, n_ctx_tiles)
        om = _mla_call(qm, km, vm, n_ctx_tiles, ctx_len)
        osw = _swa_call(swa_sink[l], sq, sk, sv, ctx_len)
        ol = _lru_call(lx, lg, p["cw"], p["cb"], p["wf"], p["bf"], p["wb"], p["bb"], lru_lam[l], n_ctx_tiles)
        x1, h2, meta, cnt = _outproj_call(om, osw, ol, src, mod, p["gg"], p["wout"], g_norm2[l][None, :],
                                          p["wr"], p["br"], n_ctx_tiles, last)
        src = _moe_call(h2, meta, cnt, x1, mod, g_final[None, :], w_e_gate[l].astype(BF16),
                        w_e_up[l].astype(BF16), w_e_down[l].astype(BF16), 0 if last else n_ctx_tiles, last)
    return src
```

```python
import functools

import jax
import jax.numpy as jnp
from jax import lax
from jax.experimental import pallas as pl
from jax.experimental.pallas import tpu as pltpu

GRID_W = 64
EPS = 1e-6
ROPE_BASE = 10000.0
NEG_INF = -1e30
MLA_HEADS = 4
MLA_NOPE = 64
MLA_ROPE = 32
MLA_V = 64
MLA_Q_RANK = 256
MLA_KV_RANK = 128
MLA_SCALE = (MLA_NOPE + MLA_ROPE) ** -0.5
SWA_HEADS = 4
SWA_KV_HEADS = 2
SWA_HEAD_DIM = 64
SWA_SCALE = SWA_HEAD_DIM ** -0.5
WINDOW = 128
LRU_WIDTH = 512
LRU_BLOCKS = 8
LRU_BW = LRU_WIDTH // LRU_BLOCKS
CONV_W = 4
LRU_C = 8.0
N_GROUPS = 4
EXPERTS_PER_GROUP = 8
N_EXPERTS = N_GROUPS * EXPERTS_PER_GROUP
D_EXPERT = 256
MLA_OUT = MLA_HEADS * MLA_V
SWA_OUT = SWA_HEADS * SWA_HEAD_DIM

LANES = 128
SUBLANES = 8
TM = 256
ZW = 2048
N_PAIRS = EXPERTS_PER_GROUP * (EXPERTS_PER_GROUP - 1) // 2
N_BUCKETS = N_GROUPS * N_PAIRS
MOE_TG = 256
MOE_TD = 512
SWA_BATCH = 4
LRU_BATCH = 4
TOK_BATCH = 2
LOG2E = 1.4426950408889634
VMEM_LIMIT = 56 * 1024 * 1024

F32 = jnp.float32
BF16 = jnp.bfloat16


def _params(sem):
    return pltpu.CompilerParams(dimension_semantics=sem, vmem_limit_bytes=VMEM_LIMIT)


def _full(shape):
    n = len(shape)
    return pl.BlockSpec(shape, lambda *_: (0,) * n)


def _rms(x, width):
    return x * lax.rsqrt(jnp.sum(x * x, axis=-1, keepdims=True) * (1.0 / width) + EPS)


def _dot(a, b):
    return jnp.dot(a, b, preferred_element_type=F32)


def _sigmoid(x):
    return 0.5 * jnp.tanh(0.5 * x) + 0.5


def _pack_pairs(x):
    w = x.shape[-1] // 2
    bits = pltpu.bitcast(x.astype(F32), jnp.uint32)
    return (bits[:, w:] & jnp.uint32(0xFFFF0000)) | (bits[:, :w] >> 16)


def _unpack_pairs(u):
    lo = pltpu.bitcast(u << 16, F32)
    hi = pltpu.bitcast(u & jnp.uint32(0xFFFF0000), F32)
    return jnp.concatenate([lo, hi], axis=-1).astype(BF16)


def _store_slabs(ref, x, pitch=SUBLANES):
    n, w = x.shape
    for s in range(pitch):
        piece = x[:, s * LANES:(s + 1) * LANES] if (s + 1) * LANES <= w else jnp.zeros((n, LANES), x.dtype)
        ref[pl.ds(s, n, stride=pitch), :] = piece


def _load_slabs(ref, n, k, pitch=SUBLANES):
    return jnp.concatenate([ref[pl.ds(s, n, stride=pitch), :] for s in range(k)], axis=-1)


def _dot_nt(a, b):
    return lax.dot_general(a, b, (((1,), (1,)), ((), ())), preferred_element_type=F32)


def _ada_kernel(c_ref, w_ref, b_ref, o_ref):
    c = c_ref[...]
    s = c * jax.nn.sigmoid(c)
    o_ref[0] = jnp.dot(s, w_ref[0], preferred_element_type=F32,
                       precision=lax.Precision.HIGHEST) + b_ref[0]


def _ada_call(cc, w_ada, b_ada):
    depth, d, n = w_ada.shape
    r = cc.shape[0]
    tn = 1536
    return pl.pallas_call(
        _ada_kernel,
        grid=(depth, n // tn),
        in_specs=[
            pl.BlockSpec((r, d), lambda l, j: (0, 0)),
            pl.BlockSpec((1, d, tn), lambda l, j: (l, 0, j)),
            pl.BlockSpec((1, 1, tn), lambda l, j: (l, 0, j)),
        ],
        out_specs=pl.BlockSpec((1, r, tn), lambda l, j: (l, 0, j)),
        out_shape=jax.ShapeDtypeStruct((depth, r, n), F32),
        compiler_params=_params(("arbitrary", "arbitrary")),
        name="adaln",
    )(cc, w_ada, b_ada.reshape(depth, 1, n))


def _rope(x, cos, sina, sinb, shift):
    n = x.shape[-1]
    reps = n // LANES
    if reps > 1:
        cos = jnp.concatenate([cos] * reps, axis=-1)
        sina = jnp.concatenate([sina] * reps, axis=-1)
        sinb = jnp.concatenate([sinb] * reps, axis=-1)
    return x * cos + pltpu.roll(x, n - shift, 1) * sina + pltpu.roll(x, shift, 1) * sinb


def _batch_views(refs, bb, n_src, n_ctx_tiles, skip, n_mod, n_const):
    one = lambda r: r.at[pl.ds(bb, 1)]
    row = jnp.where(pl.program_id(0) + skip < n_ctx_tiles, 0, bb)
    stream = [one(r) for r in refs[:n_src]]
    mods = [r.at[pl.ds(row, 1)] for r in refs[n_src:n_src + n_mod]]
    consts = list(refs[n_src + n_mod:n_src + n_mod + n_const])
    return stream, mods, consts, refs[n_src + n_mod + n_const:]


def _inproj_kernel(*refs, n_src, n_ctx_tiles):
    for bb in range(refs[0].shape[0]):
        stream, mods, consts, outs = _batch_views(refs, bb, n_src, n_ctx_tiles, 0, 2, 12)
        _inproj_one(*stream, *mods, *consts, *[o.at[pl.ds(bb, 1)] for o in outs],
                    n_src=n_src, n_ctx_tiles=n_ctx_tiles)


def _inproj_one(*refs, n_src, n_ctx_tiles):
    (sh_ref, sc_ref, g1_ref, win_ref, gcq_ref, wuq_ref, gckv_ref, wukv_ref,
     mcos_ref, msa_ref, msb_ref, scos_ref, ssa_ref, ssb_ref,
     qm_ref, km_ref, vm_ref, sq_ref, sk_ref, sv_ref, lx_ref, lg_ref) = refs[n_src:]
    x = _read_stream(refs[:n_src], n_ctx_tiles)
    d = x.shape[-1]
    h = _rms(x, d) * g1_ref[...] * (1.0 + sc_ref[0]) + sh_ref[0]
    z = _dot(h.astype(BF16), win_ref[...])

    mcos, msa, msb = mcos_ref[...], msa_ref[...], msb_ref[...]
    scos, ssa, ssb = scos_ref[...], ssa_ref[...], ssb_ref[...]

    cq = _rms(z[:, 0:256], MLA_Q_RANK) * gcq_ref[...]
    q = _dot(cq.astype(BF16), wuq_ref[...])
    q = _rope(q, mcos, msa, msb, MLA_ROPE // 4)
    qm_ref[0] = (q * (MLA_SCALE * LOG2E)).astype(BF16)

    ckv = _rms(z[:, 256:384], MLA_KV_RANK) * gckv_ref[...]
    kv = _dot(ckv.astype(BF16), wukv_ref[...])
    kr = _rope(z[:, 384:512], mcos, msa, msb, MLA_ROPE // 4)
    km_ref[0] = (kv[:, 0:512] + jnp.concatenate([kr] * MLA_HEADS, axis=-1)).astype(BF16)
    ones = jnp.ones((kv.shape[0], LANES), F32)
    vm_ref[0] = jnp.concatenate([kv[:, 512:640], ones, kv[:, 640:768], ones], axis=-1).astype(BF16)

    sq_ref[0] = (_rope(z[:, 512:768], scos, ssa, ssb, SWA_HEAD_DIM // 4) * (SWA_SCALE * LOG2E)).astype(BF16)
    sk = _rope(z[:, 768:896], scos, ssa, ssb, SWA_HEAD_DIM // 4)
    sv = z[:, 896:1024]
    lane = lax.broadcasted_iota(jnp.int32, sk.shape, 1)
    low = lane < SWA_HEAD_DIM
    k0 = jnp.where(low, sk, 0.0)
    k1 = jnp.where(low, 0.0, sk)
    sk_ref[0] = jnp.concatenate([k0 + pltpu.roll(k0, SWA_HEAD_DIM, 1), k1 + pltpu.roll(k1, SWA_HEAD_DIM, 1)],
                                axis=-1).astype(BF16)
    v0 = jnp.where(low, sv, 0.0)
    v1 = jnp.where(low, 0.0, sv)
    one_hi = jnp.where(low, 0.0, 1.0)
    sv_ref[0] = jnp.concatenate([v0 + one_hi, pltpu.roll(v1, SWA_HEAD_DIM, 1) + one_hi], axis=-1).astype(BF16)
    lx_ref[0] = z[:, 1024:1536].astype(BF16)
    lg_ref[0] = z[:, 1536:2048].astype(BF16)


def _mod_spec(j, b_rows, n_ctx_tiles, d, skip=0, nb=1):
    return pl.BlockSpec((nb, 1, d), lambda i, b: (jnp.where(i + skip < n_ctx_tiles, b_rows // nb, b), 0, j))


def _stream_specs(src, n_ctx_tiles, d, skip=0, nb=1):
    if not isinstance(src, tuple):
        return [pl.BlockSpec((nb, TM, d), lambda i, b: (b, i + skip, 0))], [src]
    ctx, lat = src
    lat_spec = pl.BlockSpec(
        (nb, TM, d), lambda i, b: (jnp.where(i + skip < n_ctx_tiles, 0, b), jnp.maximum(i + skip - n_ctx_tiles, 0), 0))
    if skip >= n_ctx_tiles:
        return [lat_spec], [lat]
    ctx_spec = pl.BlockSpec(
        (nb, TM, d), lambda i, b: (jnp.where(i + skip < n_ctx_tiles, b, 0), jnp.minimum(i + skip, n_ctx_tiles - 1), 0))
    return [ctx_spec, lat_spec], [ctx, lat]


def _read_stream(refs, n_ctx_tiles, skip=0):
    if len(refs) == 1:
        return refs[0][0]
    return jnp.where(pl.program_id(0) + skip < n_ctx_tiles, refs[0][0], refs[1][0])


def _inproj_call(src, mod, g1, win, gcq, wuq, gckv, wukv, tabs, n_ctx_tiles):
    bsz, _, d = src[1].shape if isinstance(src, tuple) else src.shape
    j = tabs[0].shape[0]
    nt = j // TM
    nb = TOK_BATCH if bsz % TOK_BATCH == 0 else 1
    x_specs, x_arrays = _stream_specs(src, n_ctx_tiles, d, 0, nb)
    tok = lambda w: pl.BlockSpec((nb, TM, w), lambda i, b: (b, i, 0))
    tab = pl.BlockSpec((TM, LANES), lambda i, b: (i, 0))
    widths = (512, 512, 512, 256, 256, 256, 512, 512)
    return pl.pallas_call(
        functools.partial(_inproj_kernel, n_src=len(x_arrays), n_ctx_tiles=n_ctx_tiles),
        grid=(nt, bsz // nb),
        in_specs=x_specs + [
            _mod_spec(0, bsz, n_ctx_tiles, d, 0, nb),
            _mod_spec(1, bsz, n_ctx_tiles, d, 0, nb),
            _full(g1.shape), _full(win.shape), _full(gcq.shape), _full(wuq.shape),
            _full(gckv.shape), _full(wukv.shape),
            tab, tab, tab, tab, tab, tab,
        ],
        out_specs=[tok(w) for w in widths],
        out_shape=[jax.ShapeDtypeStruct((bsz, j, w), BF16) for w in widths],
        compiler_params=_params(("arbitrary", "arbitrary")),
        name="inproj",
    )(*x_arrays, mod, mod, g1, win, gcq, wuq, gckv, wukv, *tabs)


def _mla_heads(q_ref, k_ref, v_ref, o_ref, klen):
    lane = lax.broadcasted_iota(jnp.int32, (q_ref.shape[1], LANES), 1)
    for hp in range(MLA_HEADS // 2):
        v = v_ref[0, 0:klen, 2 * hp * LANES:2 * (hp + 1) * LANES]
        outs = []
        for h in range(2 * hp, 2 * hp + 2):
            q = q_ref[0, :, h * LANES:(h + 1) * LANES]
            k = k_ref[0, 0:klen, h * LANES:(h + 1) * LANES]
            s = _dot_nt(q, k)
            m = jnp.max(s, axis=-1, keepdims=True)
            o = _dot(jnp.exp2(s - m).astype(BF16), v)
            outs.append(o[:, 0:LANES] / o[:, LANES:LANES + 1])
        o_ref[0, :, hp * LANES:(hp + 1) * LANES] = jnp.where(lane < MLA_V, outs[0], outs[1]).astype(o_ref.dtype)


def _mla_kernel(q_ref, k_ref, v_ref, o_ref, *, n_ctx_tiles, ctx_len):
    i = pl.program_id(1)

    @pl.when(i < n_ctx_tiles)
    def _():
        _mla_heads(q_ref, k_ref, v_ref, o_ref, ctx_len)

    @pl.when(i >= n_ctx_tiles)
    def _():
        _mla_heads(q_ref, k_ref, v_ref, o_ref, k_ref.shape[1])


def _mla_call(qm, km, vm, n_ctx_tiles, ctx_len):
    bsz, j, w = qm.shape
    nt = j // TM
    return pl.pallas_call(
        functools.partial(_mla_kernel, n_ctx_tiles=n_ctx_tiles, ctx_len=ctx_len),
        grid=(bsz, nt),
        in_specs=[
            pl.BlockSpec((1, TM, w), lambda b, i: (b, i, 0)),
            pl.BlockSpec((1, j, w), lambda b, i: (b, 0, 0)),
            pl.BlockSpec((1, j, w), lambda b, i: (b, 0, 0)),
        ],
        out_specs=pl.BlockSpec((1, TM, MLA_OUT), lambda b, i: (b, i, 0)),
        out_shape=jax.ShapeDtypeStruct((bsz, j, MLA_OUT), BF16),
        compiler_params=_params(("arbitrary", "arbitrary")),
        name="mla_attn",
    )(qm, km, vm)


def _swa_kernel(sink_ref, q_ref, k_ref, v_ref, o_ref, *, ctx_len):
    i = pl.program_id(1)
    j = k_ref.shape[1]
    span = TM + 2 * WINDOW
    row0 = pl.multiple_of(jnp.clip(i * TM - WINDOW, 0, j - span), WINDOW)
    qpos = i * TM - ctx_len + lax.broadcasted_iota(jnp.int32, (TM, span), 0)
    kpos = row0 - ctx_len + lax.broadcasted_iota(jnp.int32, (TM, span), 1)
    valid = (jnp.abs(qpos - kpos) <= WINDOW) & (kpos >= 0) & (qpos >= 0)
    valid2 = jnp.concatenate([valid, valid], axis=0)
    low = lax.broadcasted_iota(jnp.int32, (TM, LANES), 1) < SWA_HEAD_DIM
    top = lax.broadcasted_iota(jnp.int32, (2 * TM, 1), 0) < TM
    for bb, kvh in [(bb, kvh) for bb in range(q_ref.shape[0]) for kvh in range(SWA_KV_HEADS)]:
        kcols = slice(kvh * LANES, (kvh + 1) * LANES)
        vcols = kcols
        qpair = q_ref[bb, :, kcols]
        zero = jnp.zeros_like(qpair)
        q2 = jnp.concatenate([jnp.where(low, qpair, zero), jnp.where(low, zero, qpair)], axis=0)
        s_loc = jnp.where(valid2, _dot_nt(q2, k_ref[bb, pl.ds(row0, span), kcols]), NEG_INF)
        s_ctx = _dot_nt(q2, k_ref[bb, 0:ctx_len, kcols])
        sink = jnp.where(top, sink_ref[2 * kvh], sink_ref[2 * kvh + 1]) * LOG2E
        m = jnp.maximum(jnp.maximum(jnp.max(s_loc, axis=-1, keepdims=True),
                                    jnp.max(s_ctx, axis=-1, keepdims=True)), sink)
        o = (_dot(jnp.exp2(s_loc - m).astype(BF16), v_ref[bb, pl.ds(row0, span), vcols])
             + _dot(jnp.exp2(s_ctx - m).astype(BF16), v_ref[bb, 0:ctx_len, vcols]))
        res = o / (o[:, SWA_HEAD_DIM:SWA_HEAD_DIM + 1] + jnp.exp2(sink - m))
        o_ref[bb, :, kcols] = jnp.where(low, res[0:TM], pltpu.roll(res[TM:2 * TM], SWA_HEAD_DIM, 1)
                                       ).astype(o_ref.dtype)


def _swa_call(sink, sq, sk, sv, ctx_len):
    bsz, j, _ = sq.shape
    nt = j // TM
    nb = SWA_BATCH if bsz % SWA_BATCH == 0 else 1
    return pl.pallas_call(
        functools.partial(_swa_kernel, ctx_len=ctx_len),
        grid=(bsz // nb, nt),
        in_specs=[
            pl.BlockSpec(memory_space=pltpu.SMEM),
            pl.BlockSpec((nb, TM, SWA_OUT), lambda b, i: (b, i, 0)),
            pl.BlockSpec((nb, j, sk.shape[-1]), lambda b, i: (b, 0, 0)),
            pl.BlockSpec((nb, j, sv.shape[-1]), lambda b, i: (b, 0, 0)),
        ],
        out_specs=pl.BlockSpec((nb, TM, SWA_OUT), lambda b, i: (b, i, 0)),
        out_shape=jax.ShapeDtypeStruct((bsz, j, SWA_OUT), BF16),
        compiler_params=_params(("arbitrary", "arbitrary")),
        name="swa_attn",
    )(sink, sq, sk, sv)


def _lru_gates(bb, prev_ref, cur_ref, next_ref, cw_ref, cb_ref, w_ref, b_ref, sp_ref, seg_start, seg_end):
    z = cur_ref[bb].astype(F32)
    row = lax.broadcasted_iota(jnp.int32, z.shape, 0)
    keep_prev = jnp.where(seg_start, 0.0, 1.0)
    keep_next = jnp.where(seg_end, 0.0, 1.0)
    p2 = prev_ref[bb, TM - 2:TM - 1, :].astype(F32) * keep_prev
    p1 = prev_ref[bb, TM - 1:TM, :].astype(F32) * keep_prev
    n0 = next_ref[bb, 0:1, :].astype(F32) * keep_next
    z_m1 = jnp.where(row == 0, p1, pltpu.roll(z, 1, 0))
    z_m2 = jnp.where(row == 0, p2, jnp.where(row == 1, p1, pltpu.roll(z, 2, 0)))
    z_p1 = jnp.where(row == TM - 1, n0, pltpu.roll(z, TM - 1, 0))
    u = cb_ref[...] + z_m2 * cw_ref[0:1, :]
    u = u + z_m1 * cw_ref[1:2, :]
    u = u + z * cw_ref[2:3, :]
    u = u + z_p1 * cw_ref[3:4, :]
    g = _dot(u.astype(BF16), w_ref[...]) + b_ref[...]
    r = _sigmoid(g[:, 0:LRU_WIDTH])
    ig = _sigmoid(g[:, LRU_WIDTH:2 * LRU_WIDTH])
    log_a = (-LRU_C) * r * sp_ref[...]
    a = jnp.exp(log_a)
    om = 1.0 - a * a
    bt = jnp.where(om > 0.0, om * lax.rsqrt(om), 0.0) * (ig * u)
    return a, bt


def _lru_scan(a, b, h0, reverse):
    t, w = a.shape
    g = t // SUBLANES
    a = a.reshape(g, SUBLANES, w)
    b = b.reshape(g, SUBLANES, w)
    sub = lax.broadcasted_iota(jnp.int32, (g, SUBLANES, w), 1)
    d = 1
    while d < SUBLANES:
        if reverse:
            shift, ok = SUBLANES - d, sub < SUBLANES - d
        else:
            shift, ok = d, sub >= d
        a_sh = pltpu.roll(a, shift, 1)
        b_sh = pltpu.roll(b, shift, 1)
        b = jnp.where(ok, a * b_sh + b, b)
        a = jnp.where(ok, a * a_sh, a)
        d *= 2
    hs = [None] * g
    h = h0
    order = range(g - 1, -1, -1) if reverse else range(g)
    for gi in order:
        hg = a[gi] * h + b[gi]
        hs[gi] = hg
        h = hg[0:1, :] if reverse else hg[SUBLANES - 1:SUBLANES, :]
    return jnp.concatenate(hs, axis=0), h


def _softplus_neg(lam):
    x = -lam
    return jnp.maximum(x, 0.0) + jnp.log1p(jnp.exp(-jnp.abs(x)))


def _lru_fwd_kernel(prev_ref, cur_ref, next_ref, cw_ref, cb_ref, w_ref, b_ref, lam_ref,
                    hf_ref, carry_ref, sp_ref, *, n_ctx_tiles):
    c = pl.program_id(1)
    nt = pl.num_programs(1)

    @pl.when(c == 0)
    def _():
        carry_ref[...] = jnp.zeros_like(carry_ref)
        sp_ref[...] = _softplus_neg(lam_ref[...])

    seg_start = (c == 0) | (c == n_ctx_tiles)
    seg_end = (c == n_ctx_tiles - 1) | (c == nt - 1)
    for bb in range(cur_ref.shape[0]):
        a, bt = _lru_gates(bb, prev_ref, cur_ref, next_ref, cw_ref, cb_ref, w_ref, b_ref, sp_ref,
                           seg_start, seg_end)
        hs, h = _lru_scan(a, bt, carry_ref[bb], False)
        carry_ref[bb] = h
        hf_ref[bb] = hs.astype(hf_ref.dtype)


def _bwd_chunk(s, nt, n_ctx_tiles):
    return jnp.where(s < n_ctx_tiles, n_ctx_tiles - 1 - s, nt - 1 - (s - n_ctx_tiles))


def _lru_bwd_kernel(prev_ref, cur_ref, next_ref, cw_ref, cb_ref, w_ref, b_ref, lam_ref,
                    hf_ref, lg_ref, o_ref, carry_ref, sp_ref, *, n_ctx_tiles):
    s = pl.program_id(1)
    nt = pl.num_programs(1)
    c = _bwd_chunk(s, nt, n_ctx_tiles)

    @pl.when(s == 0)
    def _():
        carry_ref[...] = jnp.zeros_like(carry_ref)
        sp_ref[...] = _softplus_neg(lam_ref[...])

    seg_start = (c == 0) | (c == n_ctx_tiles)
    seg_end = (c == n_ctx_tiles - 1) | (c == nt - 1)
    for bb in range(cur_ref.shape[0]):
        a, bt = _lru_gates(bb, prev_ref, cur_ref, next_ref, cw_ref, cb_ref, w_ref, b_ref, sp_ref,
                           seg_start, seg_end)
        hs, h = _lru_scan(a, bt, carry_ref[bb], True)
        carry_ref[bb] = h
        gate = jax.nn.gelu(lg_ref[bb].astype(F32), approximate=True)
        o_ref[bb] = ((hf_ref[bb].astype(F32) + hs) * gate).astype(o_ref.dtype)


def _lru_call(lx, lg, cw, cb, wf, bf, wb, bb, lam, n_ctx_tiles):
    bsz, j, w = lx.shape
    nt = j // TM
    consts = [_full(cw.shape), _full((1, w)), _full(wf.shape), _full(bf.shape), _full((1, w))]
    nb = LRU_BATCH if bsz % LRU_BATCH == 0 else 1
    scratch = [pltpu.VMEM((nb, 1, w), F32), pltpu.VMEM((1, w), F32)]
    blk = lambda f: pl.BlockSpec((nb, TM, w), f)
    hf = pl.pallas_call(
        functools.partial(_lru_fwd_kernel, n_ctx_tiles=n_ctx_tiles),
        grid=(bsz // nb, nt),
        in_specs=[
            blk(lambda b, c: (b, jnp.maximum(c - 1, 0), 0)),
            blk(lambda b, c: (b, c, 0)),
            blk(lambda b, c: (b, jnp.minimum(c + 1, nt - 1), 0)),
        ] + consts,
        out_specs=blk(lambda b, c: (b, c, 0)),
        out_shape=jax.ShapeDtypeStruct((bsz, j, w), BF16),
        scratch_shapes=scratch,
        compiler_params=_params(("arbitrary", "arbitrary")),
        name="lru_fwd",
    )(lx, lx, lx, cw, cb, wf, bf, lam[0:1])
    cix = lambda s: _bwd_chunk(s, nt, n_ctx_tiles)
    return pl.pallas_call(
        functools.partial(_lru_bwd_kernel, n_ctx_tiles=n_ctx_tiles),
        grid=(bsz // nb, nt),
        in_specs=[
            blk(lambda b, s: (b, jnp.maximum(cix(s) - 1, 0), 0)),
            blk(lambda b, s: (b, cix(s), 0)),
            blk(lambda b, s: (b, jnp.minimum(cix(s) + 1, nt - 1), 0)),
        ] + consts + [
            blk(lambda b, s: (b, cix(s), 0)),
            blk(lambda b, s: (b, cix(s), 0)),
        ],
        out_specs=blk(lambda b, s: (b, cix(s), 0)),
        out_shape=jax.ShapeDtypeStruct((bsz, j, w), BF16),
        scratch_shapes=scratch,
        compiler_params=_params(("arbitrary", "arbitrary")),
        name="lru_bwd",
    )(lx, lx, lx, cw, cb, wb, bb, lam[1:2], hf, lg)


def _route(logits, carry):
    lt = logits.T
    tm = lt.shape[1]
    big = jnp.int32(LANES)
    lg = lt[N_EXPERTS:N_EXPERTS + SUBLANES]
    rg = lax.broadcasted_iota(jnp.int32, lg.shape, 0)
    is_g = rg < N_GROUPS
    mg = jnp.max(jnp.where(is_g, lg, -jnp.inf), axis=0, keepdims=True)
    eg = jnp.where(is_g, jnp.exp(lg - mg), 0.0)
    pg = eg / jnp.sum(eg, axis=0, keepdims=True)
    pg_top = jnp.max(pg, axis=0, keepdims=True)
    g_idx = jnp.min(jnp.where(is_g & (pg == pg_top), rg, big), axis=0, keepdims=True)
    le = lt[0:N_EXPERTS]
    re = lax.broadcasted_iota(jnp.int32, le.shape, 0)
    lo = g_idx * EXPERTS_PER_GROUP
    sel = (re >= lo) & (re < lo + EXPERTS_PER_GROUP)
    me = jnp.max(jnp.where(sel, le, -jnp.inf), axis=0, keepdims=True)
    ee = jnp.where(sel, jnp.exp(le - me), 0.0)
    pe = ee / jnp.sum(ee, axis=0, keepdims=True)
    p1 = jnp.max(jnp.where(sel, pe, -1.0), axis=0, keepdims=True)
    i1 = jnp.min(jnp.where(sel & (pe == p1), re, big), axis=0, keepdims=True)
    sel2 = sel & (re != i1)
    p2 = jnp.max(jnp.where(sel2, pe, -1.0), axis=0, keepdims=True)
    i2 = jnp.min(jnp.where(sel2 & (pe == p2), re, big), axis=0, keepdims=True)
    den = p1 + p2
    w1 = pg_top * p1 / den
    w2 = pg_top * p2 / den
    first_lo = i1 < i2
    ia = (jnp.where(first_lo, i1, i2) - lo).astype(F32)
    ib = (jnp.where(first_lo, i2, i1) - lo).astype(F32)
    wa = jnp.where(first_lo, w1, w2)
    wb = jnp.where(first_lo, w2, w1)
    pair = ia * (2 * EXPERTS_PER_GROUP - 1 - ia) * 0.5 + (ib - ia - 1.0)
    bucket = (g_idx.astype(F32) * N_PAIRS + pair).astype(jnp.int32)
    rb = lax.broadcasted_iota(jnp.int32, (LANES, tm), 0)
    mine = rb == bucket
    picks = jnp.where(mine, 1.0, 0.0)
    tri = jnp.where(lax.broadcasted_iota(jnp.int32, (tm, tm), 0) < lax.broadcasted_iota(jnp.int32, (tm, tm), 1),
                    1.0, 0.0).astype(BF16)
    before = _dot(picks.astype(BF16), tri) + carry
    rank = jnp.sum(jnp.where(mine, before, 0.0), axis=0, keepdims=True)
    zeros = jnp.zeros((SUBLANES - 2, tm), F32)
    meta = jnp.concatenate([bucket.astype(F32), rank, zeros], axis=0).astype(jnp.int32)
    wrows = jnp.concatenate([wa, wb, jnp.zeros((LANES - 2, tm), F32)], axis=0)
    return meta, wrows.T, carry + jnp.sum(picks, axis=1, keepdims=True)


def _outproj_kernel(*refs, n_src, n_ctx_tiles, skip):
    carry_ref = refs[-1]

    @pl.when((pl.program_id(0) == 0) & (pl.program_id(1) == 0))
    def _():
        carry_ref[...] = jnp.zeros_like(carry_ref)

    one = lambda r, bb: r.at[pl.ds(bb, 1)]
    for bb in range(refs[0].shape[0]):
        stream, _, _, rest = _batch_views(refs, bb, n_src, n_ctx_tiles, skip, 0, 0)
        om, osw, ol, gt1, sh2, sc2 = rest[0:6]
        row = jnp.where(pl.program_id(0) + skip < n_ctx_tiles, 0, bb)
        mods = [r.at[pl.ds(row, 1)] for r in (gt1, sh2, sc2)]
        x1_ref, h2_ref, meta_ref, cnt_ref = rest[11:15]
        _outproj_one(*stream, one(om, bb), one(osw, bb), one(ol, bb), *mods, *rest[6:11],
                     one(x1_ref, bb), h2_ref.at[bb], meta_ref.at[bb], cnt_ref, carry_ref,
                     n_src=n_src, n_ctx_tiles=n_ctx_tiles, skip=skip)


def _outproj_one(*refs, n_src, n_ctx_tiles, skip):
    (om_ref, os_ref, ol_ref, gt1_ref, sh2_ref, sc2_ref, gg_ref, wout_ref,
     g2_ref, wr_ref, br_ref, x1_ref, h2_ref, meta_ref, cnt_ref, carry_ref) = refs[n_src:]
    x = _read_stream(refs[:n_src], n_ctx_tiles, skip)
    d = x.shape[-1]
    gg = gg_ref[...]
    nm = _rms(om_ref[0].astype(F32), MLA_OUT) * gg[:, 0:MLA_OUT]
    ns = _rms(os_ref[0].astype(F32), SWA_OUT) * gg[:, MLA_OUT:MLA_OUT + SWA_OUT]
    nl = _rms(ol_ref[0].astype(F32), LRU_WIDTH) * gg[:, MLA_OUT + SWA_OUT:]
    merged = jnp.concatenate([nm, ns, nl], axis=-1).astype(BF16)
    mix = _dot(merged, wout_ref[...])
    x1 = x + gt1_ref[0] * mix
    x1_ref[0] = x1
    h2 = (_rms(x1, d) * g2_ref[...] * (1.0 + sc2_ref[0]) + sh2_ref[0]).astype(BF16)
    meta, wcols, carry = _route(_dot(h2, wr_ref[...]) + br_ref[...], carry_ref[...])
    _store_slabs(h2_ref, jnp.concatenate([_pack_pairs(h2), pltpu.bitcast(wcols, jnp.uint32)], axis=-1))
    carry_ref[...] = carry
    cnt_ref[...] = jnp.broadcast_to(carry, (LANES, LANES)).T[0:1, :].astype(jnp.int32)
    for m in range(TM // LANES):
        meta_ref[m] = meta[:, m * LANES:(m + 1) * LANES]


def _outproj_call(om, osw, ol, src, mod, gg, wout, g2, wr, br, n_ctx_tiles, latent_only):
    bsz, j_all, _ = om.shape
    d = wout.shape[1]
    skip = n_ctx_tiles if latent_only else 0
    nt = j_all // TM - skip
    j = nt * TM
    nb = TOK_BATCH if bsz % TOK_BATCH == 0 else 1
    tok_in = lambda w: pl.BlockSpec((nb, TM, w), lambda i, b: (b, i + skip, 0))
    tok = lambda w: pl.BlockSpec((nb, TM, w), lambda i, b: (b, i, 0))
    x_specs, x_arrays = _stream_specs(src, n_ctx_tiles, d, skip, nb)
    x1, h2, meta, cnt = pl.pallas_call(
        functools.partial(_outproj_kernel, n_src=len(x_arrays), n_ctx_tiles=n_ctx_tiles, skip=skip),
        grid=(nt, bsz // nb),
        in_specs=x_specs + [
            tok_in(MLA_OUT), tok_in(SWA_OUT), tok_in(LRU_WIDTH),
            _mod_spec(2, bsz, n_ctx_tiles, d, skip, nb),
            _mod_spec(3, bsz, n_ctx_tiles, d, skip, nb),
            _mod_spec(4, bsz, n_ctx_tiles, d, skip, nb),
            _full(gg.shape), _full(wout.shape), _full(g2.shape), _full(wr.shape), _full(br.shape),
        ],
        out_specs=[
            tok(d), pl.BlockSpec((nb, TM * SUBLANES, LANES), lambda i, b: (b, i, 0)),
            pl.BlockSpec((nb, TM // LANES, SUBLANES, LANES), lambda i, b: (b, i, 0, 0)),
            pl.BlockSpec((1, LANES), lambda i, b: (0, 0)),
        ],
        out_shape=[
            jax.ShapeDtypeStruct((bsz, j, d), F32),
            jax.ShapeDtypeStruct((bsz, j * SUBLANES, LANES), jnp.uint32),
            jax.ShapeDtypeStruct((bsz, j // LANES, SUBLANES, LANES), jnp.int32),
            jax.ShapeDtypeStruct((1, LANES), jnp.int32),
        ],
        scratch_shapes=[pltpu.VMEM((LANES, 1), F32)],
        compiler_params=_params(("arbitrary", "arbitrary")),
        name="outproj_route",
    )(*x_arrays, om, osw, ol, mod, mod, mod, gg, wout, g2, wr, br)
    return x1, h2.reshape(bsz * j * SUBLANES, LANES), meta.reshape(bsz * j // LANES, SUBLANES, LANES), cnt


def _padded_tiles(count):
    return lax.shift_right_logical(count + (MOE_TG - 1), MOE_TG.bit_length() - 1)


def _plan_kernel(cnt_ref, off_ref, tea_ref, teb_ref, nu_ref, *, n_tiles_max):
    off = jnp.int32(0)
    ti = jnp.int32(0)
    bucket = 0
    for g in range(N_GROUPS):
        for a in range(EXPERTS_PER_GROUP):
            for b in range(a + 1, EXPERTS_PER_GROUP):
                n_q = _padded_tiles(cnt_ref[0, bucket])
                off_ref[bucket] = off

                def fill(k, _, base=ti, ea=g * EXPERTS_PER_GROUP + a, eb=g * EXPERTS_PER_GROUP + b):
                    tea_ref[base + k] = ea
                    teb_ref[base + k] = eb
                    return 0

                lax.fori_loop(0, n_q, fill, 0)
                off = off + n_q * MOE_TG
                ti = ti + n_q
                bucket += 1
    nu_ref[0] = ti

    def rest(k, _):
        tea_ref[k] = N_EXPERTS - 2
        teb_ref[k] = N_EXPERTS - 1
        return 0

    lax.fori_loop(ti, n_tiles_max, rest, 0)


def _plan_call(cnt, n_tiles_max):
    smem = pl.BlockSpec(memory_space=pltpu.SMEM)
    return pl.pallas_call(
        functools.partial(_plan_kernel, n_tiles_max=n_tiles_max),
        in_specs=[smem],
        out_specs=[smem, smem, smem, smem],
        out_shape=[
            jax.ShapeDtypeStruct((N_BUCKETS,), jnp.int32),
            jax.ShapeDtypeStruct((n_tiles_max,), jnp.int32),
            jax.ShapeDtypeStruct((n_tiles_max,), jnp.int32),
            jax.ShapeDtypeStruct((1,), jnp.int32),
        ],
        name="moe_plan",
    )(cnt)


def _row_copy(src_ref, src_row, dst_ref, dst_row, sem, used=SUBLANES, pitch=SUBLANES):
    slot = lambda row: pl.ds(pl.multiple_of(row * pitch, pitch), used)
    return pltpu.make_async_copy(src_ref.at[slot(src_row)], dst_ref.at[slot(dst_row)], sem)


def _dispatch_kernel(off_ref, nu_ref, h_ref, meta_ref, cnt_ref, xs_ref, zero_ref, sem, tile_sem, *, tile, used):
    tile_rows = MOE_TG * SUBLANES
    n_tiles_max = xs_ref.shape[0] // tile_rows

    @pl.when(pl.program_id(0) == 0)
    def _():
        zero_ref[...] = jnp.zeros_like(zero_ref)

        def per_bucket(q, n):
            c = cnt_ref[0, q]
            padded = _padded_tiles(c) * MOE_TG
            base = off_ref[q]

            def fill(r, _):
                _row_copy(zero_ref, 0, xs_ref, base + r, sem, used).start()
                return 0

            lax.fori_loop(c, padded, fill, 0)
            return n + (padded - c)

        n_pad = lax.fori_loop(0, N_BUCKETS, per_bucket, jnp.int32(0))

        def drain(r, _):
            _row_copy(zero_ref, 0, xs_ref, 0, sem, used).wait()
            return 0

        lax.fori_loop(0, n_pad, drain, 0)

        def unused_tile(ti, _):
            cp = pltpu.make_async_copy(
                zero_ref, xs_ref.at[pl.ds(pl.multiple_of(ti * tile_rows, tile_rows), tile_rows)], tile_sem)
            cp.start()
            cp.wait()
            return 0

        lax.fori_loop(nu_ref[0], n_tiles_max, unused_tile, 0)

    for m in range(tile // LANES):
        for r in range(LANES):
            dst = off_ref[meta_ref[m, 0, r]] + meta_ref[m, 1, r]
            _row_copy(h_ref, m * LANES + r, xs_ref, dst, sem, used).start(priority=r % 2)

    pltpu.make_async_copy(h_ref.at[pl.ds(0, tile * used)], xs_ref.at[pl.ds(0, tile * used)], sem).wait()


def _dispatch_call(off, nu, h2, meta, cnt, rows_max, used):
    t = h2.shape[0] // SUBLANES
    tile = MOE_TD if t % MOE_TD == 0 else TM
    return pl.pallas_call(
        functools.partial(_dispatch_kernel, tile=tile, used=used),
        grid_spec=pltpu.PrefetchScalarGridSpec(
            num_scalar_prefetch=2,
            grid=(t // tile,),
            in_specs=[
                pl.BlockSpec((tile * SUBLANES, LANES), lambda i, off, nu: (i, 0)),
                pl.BlockSpec((tile // LANES, SUBLANES, LANES), lambda i, off, nu: (i, 0, 0),
                             memory_space=pltpu.SMEM),
                pl.BlockSpec(memory_space=pltpu.SMEM),
            ],
            out_specs=pl.BlockSpec(memory_space=pl.ANY),
            scratch_shapes=[pltpu.VMEM((MOE_TG * SUBLANES, LANES), h2.dtype), pltpu.SemaphoreType.DMA,
                            pltpu.SemaphoreType.DMA],
        ),
        out_shape=jax.ShapeDtypeStruct((rows_max * SUBLANES, LANES), h2.dtype),
        compiler_params=_params(("arbitrary",)),
        name="moe_dispatch",
    )(off, nu, h2, meta, cnt)


def _expert_mlp(x, e, wg_ref, wu_ref, wd_ref):
    gate = _dot(x, wg_ref[e])
    act = gate * _sigmoid(gate) * _dot(x, wu_ref[e])
    return _dot(act.astype(BF16), wd_ref[e])


def _expert_kernel(tea_ref, teb_ref, nu_ref, x_ref, wg_ref, wu_ref, wd_ref, y_ref):
    groups = wg_ref.shape[1] // (2 * LANES)
    i = pl.program_id(0)

    @pl.when(i < nu_ref[0])
    def _():
        row = _load_slabs(x_ref, MOE_TG, groups + 1)
        x = _unpack_pairs(row[:, 0:groups * LANES])
        wts = pltpu.bitcast(row[:, groups * LANES:], F32)
        ea = tea_ref[i] & (EXPERTS_PER_GROUP - 1)
        eb = teb_ref[i] & (EXPERTS_PER_GROUP - 1)
        y = (wts[:, 0:1] * _expert_mlp(x, ea, wg_ref, wu_ref, wd_ref)
             + wts[:, 1:2] * _expert_mlp(x, eb, wg_ref, wu_ref, wd_ref))
        _store_slabs(y_ref, _pack_pairs(y.astype(BF16)), groups)

    @pl.when(pl.program_id(0) >= nu_ref[0])
    def _():
        y_ref[...] = jnp.zeros_like(y_ref)


def _expert_call(tea, teb, nu, xs, wg, wu, wd):
    d = wg.shape[1]
    ypitch = d // (2 * LANES)
    n_tiles = xs.shape[0] // (MOE_TG * SUBLANES)
    used = lambda i, nu: jnp.minimum(i, nu[0] - 1)
    shift = EXPERTS_PER_GROUP.bit_length() - 1
    wspec = lambda rows, cols: pl.BlockSpec(
        (EXPERTS_PER_GROUP, rows, cols), lambda i, tea, teb, nu: (tea[used(i, nu)] >> shift, 0, 0))
    return pl.pallas_call(
        _expert_kernel,
        grid_spec=pltpu.PrefetchScalarGridSpec(
            num_scalar_prefetch=3,
            grid=(n_tiles,),
            in_specs=[
                pl.BlockSpec((MOE_TG * SUBLANES, LANES), lambda i, tea, teb, nu: (used(i, nu), 0)),
                wspec(d, D_EXPERT), wspec(d, D_EXPERT), wspec(D_EXPERT, d),
            ],
            out_specs=pl.BlockSpec((MOE_TG * ypitch, LANES), lambda i, tea, teb, nu: (i, 0)),
        ),
        out_shape=jax.ShapeDtypeStruct((n_tiles * MOE_TG * ypitch, LANES), jnp.uint32),
        compiler_params=_params(("arbitrary",)),
        name="moe_experts",
    )(tea, teb, nu, xs, wg, wu, wd)


def _combine_kernel(off_ref, meta_ref, ys_ref, x_ref, gt2_ref, gf_ref, o_ref, y_ref, sem, *, final, n_ctx_tiles,
                    tiles_per_batch):
    def body():
        groups = x_ref.shape[-1] // (2 * LANES)
        for m in range(TM // LANES):
            for r in range(LANES):
                src = off_ref[meta_ref[m, 0, r]] + meta_ref[m, 1, r]
                _row_copy(ys_ref, src, y_ref, m * LANES + r, sem, groups, groups).start(priority=r % 2)

        pltpu.make_async_copy(ys_ref.at[pl.ds(0, TM * groups)], y_ref.at[pl.ds(0, TM * groups)], sem).wait()
        x2 = x_ref[...] + gt2_ref[0] * _unpack_pairs(_load_slabs(y_ref, TM, groups, groups)).astype(F32)
        if final:
            x2 = _rms(x2, x2.shape[-1]) * gf_ref[...]
        o_ref[...] = x2

    if final:
        pl.when(pl.program_id(0) % tiles_per_batch >= n_ctx_tiles)(body)
    else:
        body()


def _combine_call(off, meta, ys, x1, mod, gf, tiles_per_batch, n_ctx_tiles, bsz, final):
    t, d = x1.shape
    lat = tiles_per_batch - n_ctx_tiles
    mod_row = lambda i, off: (jnp.where(i % tiles_per_batch < n_ctx_tiles, bsz, i // tiles_per_batch), 0, 5)
    if final:
        out_rows = bsz * lat * TM
        out_ix = lambda i, off: ((i // tiles_per_batch) * lat + jnp.maximum(i % tiles_per_batch - n_ctx_tiles, 0), 0)
    else:
        out_rows = t
        out_ix = lambda i, off: (i, 0)
    return pl.pallas_call(
        functools.partial(_combine_kernel, final=final, n_ctx_tiles=n_ctx_tiles, tiles_per_batch=tiles_per_batch),
        grid_spec=pltpu.PrefetchScalarGridSpec(
            num_scalar_prefetch=1,
            grid=(t // TM,),
            in_specs=[
                pl.BlockSpec((TM // LANES, SUBLANES, LANES), lambda i, off: (i, 0, 0), memory_space=pltpu.SMEM),
                pl.BlockSpec(memory_space=pl.ANY),
                pl.BlockSpec((TM, d), lambda i, off: (i, 0)),
                pl.BlockSpec((1, 1, d), mod_row),
                pl.BlockSpec((1, d), lambda i, off: (0, 0)),
            ],
            out_specs=pl.BlockSpec((TM, d), out_ix),
            scratch_shapes=[pltpu.VMEM((TM * d // (2 * LANES), LANES), jnp.uint32), pltpu.SemaphoreType.DMA],
        ),
        out_shape=jax.ShapeDtypeStruct((out_rows, d), F32),
        compiler_params=_params(("arbitrary",)),
        name="moe_combine",
    )(off, meta, ys, x1, mod, gf)


def _moe_call(h2, meta, cnt, x1, mod, gf, wg, wu, wd, n_ctx_tiles, final):
    bsz, j, d = x1.shape
    t = bsz * j
    rows_max = -(-t // MOE_TG) * MOE_TG + N_BUCKETS * MOE_TG
    off, tea, teb, nu = _plan_call(cnt, rows_max // MOE_TG)
    xs = _dispatch_call(off, nu, h2, meta, cnt, rows_max, d // (2 * LANES) + 1)
    ys = _expert_call(tea, teb, nu, xs, wg, wu, wd)
    out = _combine_call(off, meta, ys, x1.reshape(t, d), mod, gf, j // TM, n_ctx_tiles, bsz, final)
    return out.reshape(bsz, -1, d)


def _rope_tables(ctx_len, seq, groups, shift):
    fidx = [0] * LANES
    cols = [False] * LANES
    role = [0] * LANES
    for start, use_cols in groups:
        for k in range(shift):
            for half in range(2):
                lane = start + half * shift + k
                fidx[lane], cols[lane], role[lane] = k, use_cols, half + 1
    fidx = jnp.asarray(fidx, F32)[None, :]
    cols = jnp.asarray(cols)[None, :]
    role = jnp.asarray(role, jnp.int32)[None, :]
    t = jnp.arange(ctx_len + seq, dtype=jnp.int32)[:, None] - ctx_len
    pos = jnp.where(cols, t % GRID_W, t // GRID_W).astype(F32)
    ang = pos * (ROPE_BASE ** (-fidx / shift))
    rot = (t >= 0) & (role > 0)
    c, s = jnp.cos(ang), jnp.sin(ang)
    return (jnp.where(rot, c, 1.0), jnp.where(rot & (role == 1), -s, 0.0), jnp.where(rot & (role == 2), s, 0.0))


def _block_diag(w):
    n, c, d = w.shape
    eye = jnp.eye(n, dtype=w.dtype)
    return (w[:, :, None, :] * eye[:, None, :, None]).reshape(n * c, n * d)


def _prep_layer(l, w_in, g_cq, w_uq, g_ckv, w_ukv, conv_w, conv_b, lru_wa, lru_ba, lru_wx, lru_bx,
                g_grp, w_out, w_g1, b_g1, w_g2, b_g2):
    d = w_in.shape[1]
    wi = w_in[l]
    zeros = lambda n: jnp.zeros((d, n), wi.dtype)
    win = jnp.concatenate([
        wi[:, 0:384], zeros(64), wi[:, 384:416], zeros(32), wi[:, 416:1952]], axis=1).astype(BF16)
    hq = MLA_NOPE + MLA_ROPE
    wuq = jnp.concatenate(
        [jnp.pad(w_uq[l][:, h * hq:(h + 1) * hq], ((0, 0), (0, LANES - hq))) for h in range(MLA_HEADS)],
        axis=1).astype(BF16)
    wkv = w_ukv[l].reshape(MLA_KV_RANK, MLA_HEADS, MLA_NOPE + MLA_V)
    wuk = jnp.pad(wkv[:, :, :MLA_NOPE], ((0, 0), (0, 0), (0, LANES - MLA_NOPE))).reshape(MLA_KV_RANK, -1)
    wuv = wkv[:, :, MLA_NOPE:].reshape(MLA_KV_RANK, -1)
    wukv = jnp.concatenate([wuk, wuv], axis=1).astype(BF16)
    lru = []
    for dr in range(2):
        lru.append(jnp.concatenate([_block_diag(lru_wa[l, dr]), _block_diag(lru_wx[l, dr])], axis=1).astype(BF16))
        lru.append(jnp.concatenate([lru_ba[l, dr], lru_bx[l, dr]])[None, :])
    wr = jnp.concatenate([jnp.moveaxis(w_g2[l], 0, 1).reshape(d, N_EXPERTS), w_g1[l]], axis=1)
    wr = jnp.pad(wr, ((0, 0), (0, LANES - wr.shape[1]))).astype(BF16)
    br = jnp.pad(jnp.concatenate([b_g2[l].reshape(-1), b_g1[l]]), (0, LANES - N_GROUPS - N_EXPERTS))[None, :]
    return dict(win=win, gcq=g_cq[l][None, :], wuq=wuq, gckv=g_ckv[l][None, :], wukv=wukv,
                cw=conv_w[l], cb=conv_b[l][None, :], wf=lru[0], bf=lru[1], wb=lru[2], bb=lru[3],
                gg=g_grp[l][None, :], wout=w_out[l].astype(BF16), wr=wr, br=br)


def kernel(x, c, ctx, c_ctx, w_ada, b_ada, g_norm1, g_norm2, w_in, g_cq, w_uq, g_ckv, w_ukv, swa_sink,
           conv_w, conv_b, lru_wa, lru_ba, lru_wx, lru_bx, lru_lam, g_grp, w_out, w_g1, b_g1, w_g2, b_g2,
           w_e_gate, w_e_up, w_e_down, g_final):
    bsz, seq, d = x.shape
    ctx_len = ctx.shape[1]
    depth = w_ada.shape[0]
    assert seq % TM == 0 and ctx_len % TM == 0 and seq % GRID_W == 0
    n_ctx_tiles = ctx_len // TM

    rows = -(-(bsz + 1) // SUBLANES) * SUBLANES
    cc = jnp.pad(jnp.concatenate([c, c_ctx[None, :]], axis=0), ((0, rows - bsz - 1), (0, 0)))
    mods = _ada_call(cc, w_ada, b_ada)

    q4 = MLA_ROPE // 4
    s4 = SWA_HEAD_DIM // 4
    mla_tabs = _rope_tables(ctx_len, seq, [(MLA_NOPE, False), (MLA_NOPE + 2 * q4, True)], q4)
    swa_tabs = _rope_tables(
        ctx_len, seq,
        [(hh * SWA_HEAD_DIM + ax * 2 * s4, bool(ax)) for hh in range(LANES // SWA_HEAD_DIM) for ax in range(2)], s4)
    tabs = mla_tabs + swa_tabs

    src = (ctx, x)
    for l in range(depth):
        last = l == depth - 1
        p = _prep_layer(l, w_in, g_cq, w_uq, g_ckv, w_ukv, conv_w, conv_b, lru_wa, lru_ba, lru_wx, lru_bx,
                        g_grp, w_out, w_g1, b_g1, w_g2, b_g2)
        mod = mods[l].reshape(rows, 1, 6 * d)
        qm, km, vm, sq, sk, sv, lx, lg = _inproj_call(
            src, mod, g_norm1[l][None, :], p["win"], p["gcq"], p["wuq"], p["gckv"], p["wukv"], tabs, n_ctx_tiles)
        om = _mla_call(qm, km, vm, n_ctx_tiles, ctx_len)
        osw = _swa_call(swa_sink[l], sq, sk, sv, ctx_len)
        ol = _lru_call(lx, lg, p["cw"], p["cb"], p["wf"], p["bf"], p["wb"], p["bb"], lru_lam[l], n_ctx_tiles)
        x1, h2, meta, cnt = _outproj_call(om, osw, ol, src, mod, p["gg"], p["wout"], g_norm2[l][None, :],
                                          p["wr"], p["br"], n_ctx_tiles, last)
        src = _moe_call(h2, meta, cnt, x1, mod, g_final[None, :], w_e_gate[l].astype(BF16),
                        w_e_up[l].astype(BF16), w_e_down[l].astype(BF16), 0 if last else n_ctx_tiles, last)
    return src
```

```python
import functools

import jax
import jax.numpy as jnp
from jax import lax
from jax.experimental import pallas as pl
from jax.experimental.pallas import tpu as pltpu

GRID_W = 64
EPS = 1e-6
ROPE_BASE = 10000.0
NEG_INF = -1e30
MLA_HEADS = 4
MLA_NOPE = 64
MLA_ROPE = 32
MLA_V = 64
MLA_Q_RANK = 256
MLA_KV_RANK = 128
MLA_SCALE = (MLA_NOPE + MLA_ROPE) ** -0.5
SWA_HEADS = 4
SWA_KV_HEADS = 2
SWA_HEAD_DIM = 64
SWA_SCALE = SWA_HEAD_DIM ** -0.5
WINDOW = 128
LRU_WIDTH = 512
LRU_BLOCKS = 8
LRU_BW = LRU_WIDTH // LRU_BLOCKS
CONV_W = 4
LRU_C = 8.0
N_GROUPS = 4
EXPERTS_PER_GROUP = 8
N_EXPERTS = N_GROUPS * EXPERTS_PER_GROUP
D_EXPERT = 256
MLA_OUT = MLA_HEADS * MLA_V
SWA_OUT = SWA_HEADS * SWA_HEAD_DIM

LANES = 128
SUBLANES = 8
TM = 256
ZW = 2048
N_PAIRS = EXPERTS_PER_GROUP * (EXPERTS_PER_GROUP - 1) // 2
N_BUCKETS = N_GROUPS * N_PAIRS
MOE_TG = 256
MOE_TD = 512
SWA_BATCH = 4
LRU_BATCH = 4
TOK_BATCH = 4
LOG2E = 1.4426950408889634
VMEM_LIMIT = 56 * 1024 * 1024

F32 = jnp.float32
BF16 = jnp.bfloat16


def _params(sem):
    return pltpu.CompilerParams(dimension_semantics=sem, vmem_limit_bytes=VMEM_LIMIT)


def _full(shape):
    n = len(shape)
    return pl.BlockSpec(shape, lambda *_: (0,) * n)


def _rms(x, width):
    return x * lax.rsqrt(jnp.sum(x * x, axis=-1, keepdims=True) * (1.0 / width) + EPS)


def _dot(a, b):
    return jnp.dot(a, b, preferred_element_type=F32)


def _sigmoid(x):
    return 0.5 * jnp.tanh(0.5 * x) + 0.5


def _pack_pairs(x):
    w = x.shape[-1] // 2
    bits = pltpu.bitcast(x.astype(F32), jnp.uint32)
    return (bits[:, w:] & jnp.uint32(0xFFFF0000)) | (bits[:, :w] >> 16)


def _unpack_pairs(u):
    lo = pltpu.bitcast(u << 16, F32)
    hi = pltpu.bitcast(u & jnp.uint32(0xFFFF0000), F32)
    return jnp.concatenate([lo, hi], axis=-1).astype(BF16)


def _store_slabs(ref, x, pitch=SUBLANES):
    n, w = x.shape
    for s in range(pitch):
        piece = x[:, s * LANES:(s + 1) * LANES] if (s + 1) * LANES <= w else jnp.zeros((n, LANES), x.dtype)
        ref[pl.ds(s, n, stride=pitch), :] = piece


def _load_slabs(ref, n, k, pitch=SUBLANES):
    return jnp.concatenate([ref[pl.ds(s, n, stride=pitch), :] for s in range(k)], axis=-1)


def _dot_nt(a, b):
    return lax.dot_general(a, b, (((1,), (1,)), ((), ())), preferred_element_type=F32)


def _ada_kernel(c_ref, w_ref, b_ref, o_ref):
    c = c_ref[...]
    s = c * jax.nn.sigmoid(c)
    o_ref[0] = jnp.dot(s, w_ref[0], preferred_element_type=F32,
                       precision=lax.Precision.HIGHEST) + b_ref[0]


def _ada_call(cc, w_ada, b_ada):
    depth, d, n = w_ada.shape
    r = cc.shape[0]
    tn = 1536
    return pl.pallas_call(
        _ada_kernel,
        grid=(depth, n // tn),
        in_specs=[
            pl.BlockSpec((r, d), lambda l, j: (0, 0)),
            pl.BlockSpec((1, d, tn), lambda l, j: (l, 0, j)),
            pl.BlockSpec((1, 1, tn), lambda l, j: (l, 0, j)),
        ],
        out_specs=pl.BlockSpec((1, r, tn), lambda l, j: (l, 0, j)),
        out_shape=jax.ShapeDtypeStruct((depth, r, n), F32),
        compiler_params=_params(("arbitrary", "arbitrary")),
        name="adaln",
    )(cc, w_ada, b_ada.reshape(depth, 1, n))


def _rope(x, cos, sina, sinb, shift):
    n = x.shape[-1]
    reps = n // LANES
    if reps > 1:
        cos = jnp.concatenate([cos] * reps, axis=-1)
        sina = jnp.concatenate([sina] * reps, axis=-1)
        sinb = jnp.concatenate([sinb] * reps, axis=-1)
    return x * cos + pltpu.roll(x, n - shift, 1) * sina + pltpu.roll(x, shift, 1) * sinb


def _batch_views(refs, bb, n_src, n_ctx_tiles, skip, n_mod, n_const):
    one = lambda r: r.at[pl.ds(bb, 1)]
    row = jnp.where(pl.program_id(0) + skip < n_ctx_tiles, 0, bb)
    stream = [one(r) for r in refs[:n_src]]
    mods = [r.at[pl.ds(row, 1)] for r in refs[n_src:n_src + n_mod]]
    consts = list(refs[n_src + n_mod:n_src + n_mod + n_const])
    return stream, mods, consts, refs[n_src + n_mod + n_const:]


def _inproj_kernel(*refs, n_src, n_ctx_tiles):
    for bb in range(refs[0].shape[0]):
        stream, mods, consts, outs = _batch_views(refs, bb, n_src, n_ctx_tiles, 0, 2, 12)
        _inproj_one(*stream, *mods, *consts, *[o.at[pl.ds(bb, 1)] for o in outs],
                    n_src=n_src, n_ctx_tiles=n_ctx_tiles)


def _inproj_one(*refs, n_src, n_ctx_tiles):
    (sh_ref, sc_ref, g1_ref, win_ref, gcq_ref, wuq_ref, gckv_ref, wukv_ref,
     mcos_ref, msa_ref, msb_ref, scos_ref, ssa_ref, ssb_ref,
     qm_ref, km_ref, vm_ref, sq_ref, sk_ref, sv_ref, lx_ref, lg_ref) = refs[n_src:]
    x = _read_stream(refs[:n_src], n_ctx_tiles)
    d = x.shape[-1]
    h = _rms(x, d) * g1_ref[...] * (1.0 + sc_ref[0]) + sh_ref[0]
    z = _dot(h.astype(BF16), win_ref[...])

    mcos, msa, msb = mcos_ref[...], msa_ref[...], msb_ref[...]
    scos, ssa, ssb = scos_ref[...], ssa_ref[...], ssb_ref[...]

    cq = _rms(z[:, 0:256], MLA_Q_RANK) * gcq_ref[...]
    q = _dot(cq.astype(BF16), wuq_ref[...])
    q = _rope(q, mcos, msa, msb, MLA_ROPE // 4)
    qm_ref[0] = (q * (MLA_SCALE * LOG2E)).astype(BF16)

    ckv = _rms(z[:, 256:384], MLA_KV_RANK) * gckv_ref[...]
    kv = _dot(ckv.astype(BF16), wukv_ref[...])
    kr = _rope(z[:, 384:512], mcos, msa, msb, MLA_ROPE // 4)
    km_ref[0] = (kv[:, 0:512] + jnp.concatenate([kr] * MLA_HEADS, axis=-1)).astype(BF16)
    ones = jnp.ones((kv.shape[0], LANES), F32)
    vm_ref[0] = jnp.concatenate([kv[:, 512:640], ones, kv[:, 640:768], ones], axis=-1).astype(BF16)

    sq_ref[0] = (_rope(z[:, 512:768], scos, ssa, ssb, SWA_HEAD_DIM // 4) * (SWA_SCALE * LOG2E)).astype(BF16)
    sk = _rope(z[:, 768:896], scos, ssa, ssb, SWA_HEAD_DIM // 4)
    sv = z[:, 896:1024]
    lane = lax.broadcasted_iota(jnp.int32, sk.shape, 1)
    low = lane < SWA_HEAD_DIM
    k0 = jnp.where(low, sk, 0.0)
    k1 = jnp.where(low, 0.0, sk)
    sk_ref[0] = jnp.concatenate([k0 + pltpu.roll(k0, SWA_HEAD_DIM, 1), k1 + pltpu.roll(k1, SWA_HEAD_DIM, 1)],
                                axis=-1).astype(BF16)
    v0 = jnp.where(low, sv, 0.0)
    v1 = jnp.where(low, 0.0, sv)
    one_hi = jnp.where(low, 0.0, 1.0)
    sv_ref[0] = jnp.concatenate([v0 + one_hi, pltpu.roll(v1, SWA_HEAD_DIM, 1) + one_hi], axis=-1).astype(BF16)
    lx_ref[0] = z[:, 1024:1536].astype(BF16)
    lg_ref[0] = z[:, 1536:2048].astype(BF16)


def _mod_spec(j, b_rows, n_ctx_tiles, d, skip=0, nb=1):
    return pl.BlockSpec((nb, 1, d), lambda i, b: (jnp.where(i + skip < n_ctx_tiles, b_rows // nb, b), 0, j))


def _stream_specs(src, n_ctx_tiles, d, skip=0, nb=1):
    if not isinstance(src, tuple):
        return [pl.BlockSpec((nb, TM, d), lambda i, b: (b, i + skip, 0))], [src]
    ctx, lat = src
    lat_spec = pl.BlockSpec(
        (nb, TM, d), lambda i, b: (jnp.where(i + skip < n_ctx_tiles, 0, b), jnp.maximum(i + skip - n_ctx_tiles, 0), 0))
    if skip >= n_ctx_tiles:
        return [lat_spec], [lat]
    ctx_spec = pl.BlockSpec(
        (nb, TM, d), lambda i, b: (jnp.where(i + skip < n_ctx_tiles, b, 0), jnp.minimum(i + skip, n_ctx_tiles - 1), 0))
    return [ctx_spec, lat_spec], [ctx, lat]


def _read_stream(refs, n_ctx_tiles, skip=0):
    if len(refs) == 1:
        return refs[0][0]
    return jnp.where(pl.program_id(0) + skip < n_ctx_tiles, refs[0][0], refs[1][0])


def _inproj_call(src, mod, g1, win, gcq, wuq, gckv, wukv, tabs, n_ctx_tiles):
    bsz, _, d = src[1].shape if isinstance(src, tuple) else src.shape
    j = tabs[0].shape[0]
    nt = j // TM
    nb = TOK_BATCH if bsz % TOK_BATCH == 0 else 1
    x_specs, x_arrays = _stream_specs(src, n_ctx_tiles, d, 0, nb)
    tok = lambda w: pl.BlockSpec((nb, TM, w), lambda i, b: (b, i, 0))
    tab = pl.BlockSpec((TM, LANES), lambda i, b: (i, 0))
    widths = (512, 512, 512, 256, 256, 256, 512, 512)
    return pl.pallas_call(
        functools.partial(_inproj_kernel, n_src=len(x_arrays), n_ctx_tiles=n_ctx_tiles),
        grid=(nt, bsz // nb),
        in_specs=x_specs + [
            _mod_spec(0, bsz, n_ctx_tiles, d, 0, nb),
            _mod_spec(1, bsz, n_ctx_tiles, d, 0, nb),
            _full(g1.shape), _full(win.shape), _full(gcq.shape), _full(wuq.shape),
            _full(gckv.shape), _full(wukv.shape),
            tab, tab, tab, tab, tab, tab,
        ],
        out_specs=[tok(w) for w in widths],
        out_shape=[jax.ShapeDtypeStruct((bsz, j, w), BF16) for w in widths],
        compiler_params=_params(("arbitrary", "arbitrary")),
        name="inproj",
    )(*x_arrays, mod, mod, g1, win, gcq, wuq, gckv, wukv, *tabs)


def _mla_heads(q_ref, k_ref, v_ref, o_ref, klen):
    lane = lax.broadcasted_iota(jnp.int32, (q_ref.shape[1], LANES), 1)
    for hp in range(MLA_HEADS // 2):
        v = v_ref[0, 0:klen, 2 * hp * LANES:2 * (hp + 1) * LANES]
        outs = []
        for h in range(2 * hp, 2 * hp + 2):
            q = q_ref[0, :, h * LANES:(h + 1) * LANES]
            k = k_ref[0, 0:klen, h * LANES:(h + 1) * LANES]
            s = _dot_nt(q, k)
            m = jnp.max(s, axis=-1, keepdims=True)
            o = _dot(jnp.exp2(s - m).astype(BF16), v)
            outs.append(o[:, 0:LANES] / o[:, LANES:LANES + 1])
        o_ref[0, :, hp * LANES:(hp + 1) * LANES] = jnp.where(lane < MLA_V, outs[0], outs[1]).astype(o_ref.dtype)


def _mla_kernel(q_ref, k_ref, v_ref, o_ref, *, n_ctx_tiles, ctx_len):
    i = pl.program_id(1)

    @pl.when(i < n_ctx_tiles)
    def _():
        _mla_heads(q_ref, k_ref, v_ref, o_ref, ctx_len)

    @pl.when(i >= n_ctx_tiles)
    def _():
        _mla_heads(q_ref, k_ref, v_ref, o_ref, k_ref.shape[1])


def _mla_call(qm, km, vm, n_ctx_tiles, ctx_len):
    bsz, j, w = qm.shape
    nt = j // TM
    return pl.pallas_call(
        functools.partial(_mla_kernel, n_ctx_tiles=n_ctx_tiles, ctx_len=ctx_len),
        grid=(bsz, nt),
        in_specs=[
            pl.BlockSpec((1, TM, w), lambda b, i: (b, i, 0)),
            pl.BlockSpec((1, j, w), lambda b, i: (b, 0, 0)),
            pl.BlockSpec((1, j, w), lambda b, i: (b, 0, 0)),
        ],
        out_specs=pl.BlockSpec((1, TM, MLA_OUT), lambda b, i: (b, i, 0)),
        out_shape=jax.ShapeDtypeStruct((bsz, j, MLA_OUT), BF16),
        compiler_params=_params(("arbitrary", "arbitrary")),
        name="mla_attn",
    )(qm, km, vm)


def _swa_kernel(sink_ref, q_ref, k_ref, v_ref, o_ref, *, ctx_len):
    i = pl.program_id(1)
    j = k_ref.shape[1]
    span = TM + 2 * WINDOW
    row0 = pl.multiple_of(jnp.clip(i * TM - WINDOW, 0, j - span), WINDOW)
    qpos = i * TM - ctx_len + lax.broadcasted_iota(jnp.int32, (TM, span), 0)
    kpos = row0 - ctx_len + lax.broadcasted_iota(jnp.int32, (TM, span), 1)
    valid = (jnp.abs(qpos - kpos) <= WINDOW) & (kpos >= 0) & (qpos >= 0)
    valid2 = jnp.concatenate([valid, valid], axis=0)
    low = lax.broadcasted_iota(jnp.int32, (TM, LANES), 1) < SWA_HEAD_DIM
    top = lax.broadcasted_iota(jnp.int32, (2 * TM, 1), 0) < TM
    for bb, kvh in [(bb, kvh) for bb in range(q_ref.shape[0]) for kvh in range(SWA_KV_HEADS)]:
        kcols = slice(kvh * LANES, (kvh + 1) * LANES)
        vcols = kcols
        qpair = q_ref[bb, :, kcols]
        zero = jnp.zeros_like(qpair)
        q2 = jnp.concatenate([jnp.where(low, qpair, zero), jnp.where(low, zero, qpair)], axis=0)
        s_loc = jnp.where(valid2, _dot_nt(q2, k_ref[bb, pl.ds(row0, span), kcols]), NEG_INF)
        s_ctx = _dot_nt(q2, k_ref[bb, 0:ctx_len, kcols])
        sink = jnp.where(top, sink_ref[2 * kvh], sink_ref[2 * kvh + 1]) * LOG2E
        m = jnp.maximum(jnp.maximum(jnp.max(s_loc, axis=-1, keepdims=True),
                                    jnp.max(s_ctx, axis=-1, keepdims=True)), sink)
        o = (_dot(jnp.exp2(s_loc - m).astype(BF16), v_ref[bb, pl.ds(row0, span), vcols])
             + _dot(jnp.exp2(s_ctx - m).astype(BF16), v_ref[bb, 0:ctx_len, vcols]))
        res = o / (o[:, SWA_HEAD_DIM:SWA_HEAD_DIM + 1] + jnp.exp2(sink - m))
        o_ref[bb, :, kcols] = jnp.where(low, res[0:TM], pltpu.roll(res[TM:2 * TM], SWA_HEAD_DIM, 1)
                                       ).astype(o_ref.dtype)


def _swa_call(sink, sq, sk, sv, ctx_len):
    bsz, j, _ = sq.shape
    nt = j // TM
    nb = SWA_BATCH if bsz % SWA_BATCH == 0 else 1
    return pl.pallas_call(
        functools.partial(_swa_kernel, ctx_len=ctx_len),
        grid=(bsz // nb, nt),
        in_specs=[
            pl.BlockSpec(memory_space=pltpu.SMEM),
            pl.BlockSpec((nb, TM, SWA_OUT), lambda b, i: (b, i, 0)),
            pl.BlockSpec((nb, j, sk.shape[-1]), lambda b, i: (b, 0, 0)),
            pl.BlockSpec((nb, j, sv.shape[-1]), lambda b, i: (b, 0, 0)),
        ],
        out_specs=pl.BlockSpec((nb, TM, SWA_OUT), lambda b, i: (b, i, 0)),
        out_shape=jax.ShapeDtypeStruct((bsz, j, SWA_OUT), BF16),
        compiler_params=_params(("arbitrary", "arbitrary")),
        name="swa_attn",
    )(sink, sq, sk, sv)


def _lru_gates(bb, prev_ref, cur_ref, next_ref, cw_ref, cb_ref, w_ref, b_ref, sp_ref, seg_start, seg_end):
    z = cur_ref[bb].astype(F32)
    row = lax.broadcasted_iota(jnp.int32, z.shape, 0)
    keep_prev = jnp.where(seg_start, 0.0, 1.0)
    keep_next = jnp.where(seg_end, 0.0, 1.0)
    p2 = prev_ref[bb, TM - 2:TM - 1, :].astype(F32) * keep_prev
    p1 = prev_ref[bb, TM - 1:TM, :].astype(F32) * keep_prev
    n0 = next_ref[bb, 0:1, :].astype(F32) * keep_next
    z_m1 = jnp.where(row == 0, p1, pltpu.roll(z, 1, 0))
    z_m2 = jnp.where(row == 0, p2, jnp.where(row == 1, p1, pltpu.roll(z, 2, 0)))
    z_p1 = jnp.where(row == TM - 1, n0, pltpu.roll(z, TM - 1, 0))
    u = cb_ref[...] + z_m2 * cw_ref[0:1, :]
    u = u + z_m1 * cw_ref[1:2, :]
    u = u + z * cw_ref[2:3, :]
    u = u + z_p1 * cw_ref[3:4, :]
    g = _dot(u.astype(BF16), w_ref[...]) + b_ref[...]
    r = _sigmoid(g[:, 0:LRU_WIDTH])
    ig = _sigmoid(g[:, LRU_WIDTH:2 * LRU_WIDTH])
    log_a = (-LRU_C) * r * sp_ref[...]
    a = jnp.exp(log_a)
    om = 1.0 - a * a
    bt = jnp.where(om > 0.0, om * lax.rsqrt(om), 0.0) * (ig * u)
    return a, bt


def _lru_scan(a, b, h0, reverse):
    t, w = a.shape
    g = t // SUBLANES
    a = a.reshape(g, SUBLANES, w)
    b = b.reshape(g, SUBLANES, w)
    sub = lax.broadcasted_iota(jnp.int32, (g, SUBLANES, w), 1)
    d = 1
    while d < SUBLANES:
        if reverse:
            shift, ok = SUBLANES - d, sub < SUBLANES - d
        else:
            shift, ok = d, sub >= d
        a_sh = pltpu.roll(a, shift, 1)
        b_sh = pltpu.roll(b, shift, 1)
        b = jnp.where(ok, a * b_sh + b, b)
        a = jnp.where(ok, a * a_sh, a)
        d *= 2
    hs = [None] * g
    h = h0
    order = range(g - 1, -1, -1) if reverse else range(g)
    for gi in order:
        hg = a[gi] * h + b[gi]
        hs[gi] = hg
        h = hg[0:1, :] if reverse else hg[SUBLANES - 1:SUBLANES, :]
    return jnp.concatenate(hs, axis=0), h


def _softplus_neg(lam):
    x = -lam
    return jnp.maximum(x, 0.0) + jnp.log1p(jnp.exp(-jnp.abs(x)))


def _lru_fwd_kernel(prev_ref, cur_ref, next_ref, cw_ref, cb_ref, w_ref, b_ref, lam_ref,
                    hf_ref, carry_ref, sp_ref, *, n_ctx_tiles):
    c = pl.program_id(1)
    nt = pl.num_programs(1)

    @pl.when(c == 0)
    def _():
        carry_ref[...] = jnp.zeros_like(carry_ref)
        sp_ref[...] = _softplus_neg(lam_ref[...])

    seg_start = (c == 0) | (c == n_ctx_tiles)
    seg_end = (c == n_ctx_tiles - 1) | (c == nt - 1)
    for bb in range(cur_ref.shape[0]):
        a, bt = _lru_gates(bb, prev_ref, cur_ref, next_ref, cw_ref, cb_ref, w_ref, b_ref, sp_ref,
                           seg_start, seg_end)
        hs, h = _lru_scan(a, bt, carry_ref[bb], False)
        carry_ref[bb] = h
        hf_ref[bb] = hs.astype(hf_ref.dtype)


def _bwd_chunk(s, nt, n_ctx_tiles):
    return jnp.where(s < n_ctx_tiles, n_ctx_tiles - 1 - s, nt - 1 - (s - n_ctx_tiles))


def _lru_bwd_kernel(prev_ref, cur_ref, next_ref, cw_ref, cb_ref, w_ref, b_ref, lam_ref,
                    hf_ref, lg_ref, o_ref, carry_ref, sp_ref, *, n_ctx_tiles):
    s = pl.program_id(1)
    nt = pl.num_programs(1)
    c = _bwd_chunk(s, nt, n_ctx_tiles)

    @pl.when(s == 0)
    def _():
        carry_ref[...] = jnp.zeros_like(carry_ref)
        sp_ref[...] = _softplus_neg(lam_ref[...])

    seg_start = (c == 0) | (c == n_ctx_tiles)
    seg_end = (c == n_ctx_tiles - 1) | (c == nt - 1)
    for bb in range(cur_ref.shape[0]):
        a, bt = _lru_gates(bb, prev_ref, cur_ref, next_ref, cw_ref, cb_ref, w_ref, b_ref, sp_ref,
                           seg_start, seg_end)
        hs, h = _lru_scan(a, bt, carry_ref[bb], True)
        carry_ref[bb] = h
        gate = jax.nn.gelu(lg_ref[bb].astype(F32), approximate=True)
        o_ref[bb] = ((hf_ref[bb].astype(F32) + hs) * gate).astype(o_ref.dtype)


def _lru_call(lx, lg, cw, cb, wf, bf, wb, bb, lam, n_ctx_tiles):
    bsz, j, w = lx.shape
    nt = j // TM
    consts = [_full(cw.shape), _full((1, w)), _full(wf.shape), _full(bf.shape), _full((1, w))]
    nb = LRU_BATCH if bsz % LRU_BATCH == 0 else 1
    scratch = [pltpu.VMEM((nb, 1, w), F32), pltpu.VMEM((1, w), F32)]
    blk = lambda f: pl.BlockSpec((nb, TM, w), f)
    hf = pl.pallas_call(
        functools.partial(_lru_fwd_kernel, n_ctx_tiles=n_ctx_tiles),
        grid=(bsz // nb, nt),
        in_specs=[
            blk(lambda b, c: (b, jnp.maximum(c - 1, 0), 0)),
            blk(lambda b, c: (b, c, 0)),
            blk(lambda b, c: (b, jnp.minimum(c + 1, nt - 1), 0)),
        ] + consts,
        out_specs=blk(lambda b, c: (b, c, 0)),
        out_shape=jax.ShapeDtypeStruct((bsz, j, w), BF16),
        scratch_shapes=scratch,
        compiler_params=_params(("arbitrary", "arbitrary")),
        name="lru_fwd",
    )(lx, lx, lx, cw, cb, wf, bf, lam[0:1])
    cix = lambda s: _bwd_chunk(s, nt, n_ctx_tiles)
    return pl.pallas_call(
        functools.partial(_lru_bwd_kernel, n_ctx_tiles=n_ctx_tiles),
        grid=(bsz // nb, nt),
        in_specs=[
            blk(lambda b, s: (b, jnp.maximum(cix(s) - 1, 0), 0)),
            blk(lambda b, s: (b, cix(s), 0)),
            blk(lambda b, s: (b, jnp.minimum(cix(s) + 1, nt - 1), 0)),
        ] + consts + [
            blk(lambda b, s: (b, cix(s), 0)),
            blk(lambda b, s: (b, cix(s), 0)),
        ],
        out_specs=blk(lambda b, s: (b, cix(s), 0)),
        out_shape=jax.ShapeDtypeStruct((bsz, j, w), BF16),
        scratch_shapes=scratch,
        compiler_params=_params(("arbitrary", "arbitrary")),
        name="lru_bwd",
    )(lx, lx, lx, cw, cb, wb, bb, lam[1:2], hf, lg)


def _route(logits, carry):
    lt = logits.T
    tm = lt.shape[1]
    big = jnp.int32(LANES)
    lg = lt[N_EXPERTS:N_EXPERTS + SUBLANES]
    rg = lax.broadcasted_iota(jnp.int32, lg.shape, 0)
    is_g = rg < N_GROUPS
    mg = jnp.max(jnp.where(is_g, lg, -jnp.inf), axis=0, keepdims=True)
    eg = jnp.where(is_g, jnp.exp(lg - mg), 0.0)
    pg = eg / jnp.sum(eg, axis=0, keepdims=True)
    pg_top = jnp.max(pg, axis=0, keepdims=True)
    g_idx = jnp.min(jnp.where(is_g & (pg == pg_top), rg, big), axis=0, keepdims=True)
    le = lt[0:N_EXPERTS]
    re = lax.broadcasted_iota(jnp.int32, le.shape, 0)
    lo = g_idx * EXPERTS_PER_GROUP
    sel = (re >= lo) & (re < lo + EXPERTS_PER_GROUP)
    me = jnp.max(jnp.where(sel, le, -jnp.inf), axis=0, keepdims=True)
    ee = jnp.where(sel, jnp.exp(le - me), 0.0)
    pe = ee / jnp.sum(ee, axis=0, keepdims=True)
    p1 = jnp.max(jnp.where(sel, pe, -1.0), axis=0, keepdims=True)
    i1 = jnp.min(jnp.where(sel & (pe == p1), re, big), axis=0, keepdims=True)
    sel2 = sel & (re != i1)
    p2 = jnp.max(jnp.where(sel2, pe, -1.0), axis=0, keepdims=True)
    i2 = jnp.min(jnp.where(sel2 & (pe == p2), re, big), axis=0, keepdims=True)
    den = p1 + p2
    w1 = pg_top * p1 / den
    w2 = pg_top * p2 / den
    first_lo = i1 < i2
    ia = (jnp.where(first_lo, i1, i2) - lo).astype(F32)
    ib = (jnp.where(first_lo, i2, i1) - lo).astype(F32)
    wa = jnp.where(first_lo, w1, w2)
    wb = jnp.where(first_lo, w2, w1)
    pair = ia * (2 * EXPERTS_PER_GROUP - 1 - ia) * 0.5 + (ib - ia - 1.0)
    bucket = (g_idx.astype(F32) * N_PAIRS + pair).astype(jnp.int32)
    rb = lax.broadcasted_iota(jnp.int32, (LANES, tm), 0)
    mine = rb == bucket
    picks = jnp.where(mine, 1.0, 0.0)
    tri = jnp.where(lax.broadcasted_iota(jnp.int32, (tm, tm), 0) < lax.broadcasted_iota(jnp.int32, (tm, tm), 1),
                    1.0, 0.0).astype(BF16)
    before = _dot(picks.astype(BF16), tri) + carry
    rank = jnp.sum(jnp.where(mine, before, 0.0), axis=0, keepdims=True)
    zeros = jnp.zeros((SUBLANES - 2, tm), F32)
    meta = jnp.concatenate([bucket.astype(F32), rank, zeros], axis=0).astype(jnp.int32)
    wrows = jnp.concatenate([wa, wb, jnp.zeros((LANES - 2, tm), F32)], axis=0)
    return meta, wrows.T, carry + jnp.sum(picks, axis=1, keepdims=True)


def _outproj_kernel(*refs, n_src, n_ctx_tiles, skip):
    carry_ref = refs[-1]

    @pl.when((pl.program_id(0) == 0) & (pl.program_id(1) == 0))
    def _():
        carry_ref[...] = jnp.zeros_like(carry_ref)

    one = lambda r, bb: r.at[pl.ds(bb, 1)]
    for bb in range(refs[0].shape[0]):
        stream, _, _, rest = _batch_views(refs, bb, n_src, n_ctx_tiles, skip, 0, 0)
        om, osw, ol, gt1, sh2, sc2 = rest[0:6]
        row = jnp.where(pl.program_id(0) + skip < n_ctx_tiles, 0, bb)
        mods = [r.at[pl.ds(row, 1)] for r in (gt1, sh2, sc2)]
        x1_ref, h2_ref, meta_ref, cnt_ref = rest[11:15]
        _outproj_one(*stream, one(om, bb), one(osw, bb), one(ol, bb), *mods, *rest[6:11],
                     one(x1_ref, bb), h2_ref.at[bb], meta_ref.at[bb], cnt_ref, carry_ref,
                     n_src=n_src, n_ctx_tiles=n_ctx_tiles, skip=skip)


def _outproj_one(*refs, n_src, n_ctx_tiles, skip):
    (om_ref, os_ref, ol_ref, gt1_ref, sh2_ref, sc2_ref, gg_ref, wout_ref,
     g2_ref, wr_ref, br_ref, x1_ref, h2_ref, meta_ref, cnt_ref, carry_ref) = refs[n_src:]
    x = _read_stream(refs[:n_src], n_ctx_tiles, skip)
    d = x.shape[-1]
    gg = gg_ref[...]
    nm = _rms(om_ref[0].astype(F32), MLA_OUT) * gg[:, 0:MLA_OUT]
    ns = _rms(os_ref[0].astype(F32), SWA_OUT) * gg[:, MLA_OUT:MLA_OUT + SWA_OUT]
    nl = _rms(ol_ref[0].astype(F32), LRU_WIDTH) * gg[:, MLA_OUT + SWA_OUT:]
    merged = jnp.concatenate([nm, ns, nl], axis=-1).astype(BF16)
    mix = _dot(merged, wout_ref[...])
    x1 = x + gt1_ref[0] * mix
    x1_ref[0] = x1
    h2 = (_rms(x1, d) * g2_ref[...] * (1.0 + sc2_ref[0]) + sh2_ref[0]).astype(BF16)
    meta, wcols, carry = _route(_dot(h2, wr_ref[...]) + br_ref[...], carry_ref[...])
    _store_slabs(h2_ref, jnp.concatenate([_pack_pairs(h2), pltpu.bitcast(wcols, jnp.uint32)], axis=-1))
    carry_ref[...] = carry
    cnt_ref[...] = jnp.broadcast_to(carry, (LANES, LANES)).T[0:1, :].astype(jnp.int32)
    for m in range(TM // LANES):
        meta_ref[m] = meta[:, m * LANES:(m + 1) * LANES]


def _outproj_call(om, osw, ol, src, mod, gg, wout, g2, wr, br, n_ctx_tiles, latent_only):
    bsz, j_all, _ = om.shape
    d = wout.shape[1]
    skip = n_ctx_tiles if latent_only else 0
    nt = j_all // TM - skip
    j = nt * TM
    nb = TOK_BATCH if bsz % TOK_BATCH == 0 else 1
    tok_in = lambda w: pl.BlockSpec((nb, TM, w), lambda i, b: (b, i + skip, 0))
    tok = lambda w: pl.BlockSpec((nb, TM, w), lambda i, b: (b, i, 0))
    x_specs, x_arrays = _stream_specs(src, n_ctx_tiles, d, skip, nb)
    x1, h2, meta, cnt = pl.pallas_call(
        functools.partial(_outproj_kernel, n_src=len(x_arrays), n_ctx_tiles=n_ctx_tiles, skip=skip),
        grid=(nt, bsz // nb),
        in_specs=x_specs + [
            tok_in(MLA_OUT), tok_in(SWA_OUT), tok_in(LRU_WIDTH),
            _mod_spec(2, bsz, n_ctx_tiles, d, skip, nb),
            _mod_spec(3, bsz, n_ctx_tiles, d, skip, nb),
            _mod_spec(4, bsz, n_ctx_tiles, d, skip, nb),
            _full(gg.shape), _full(wout.shape), _full(g2.shape), _full(wr.shape), _full(br.shape),
        ],
        out_specs=[
            tok(d), pl.BlockSpec((nb, TM * SUBLANES, LANES), lambda i, b: (b, i, 0)),
            pl.BlockSpec((nb, TM // LANES, SUBLANES, LANES), lambda i, b: (b, i, 0, 0)),
            pl.BlockSpec((1, LANES), lambda i, b: (0, 0)),
        ],
        out_shape=[
            jax.ShapeDtypeStruct((bsz, j, d), F32),
            jax.ShapeDtypeStruct((bsz, j * SUBLANES, LANES), jnp.uint32),
            jax.ShapeDtypeStruct((bsz, j // LANES, SUBLANES, LANES), jnp.int32),
            jax.ShapeDtypeStruct((1, LANES), jnp.int32),
        ],
        scratch_shapes=[pltpu.VMEM((LANES, 1), F32)],
        compiler_params=_params(("arbitrary", "arbitrary")),
        name="outproj_route",
    )(*x_arrays, om, osw, ol, mod, mod, mod, gg, wout, g2, wr, br)
    return x1, h2.reshape(bsz * j * SUBLANES, LANES), meta.reshape(bsz * j // LANES, SUBLANES, LANES), cnt


def _padded_tiles(count):
    return lax.shift_right_logical(count + (MOE_TG - 1), MOE_TG.bit_length() - 1)


def _plan_kernel(cnt_ref, off_ref, tea_ref, teb_ref, nu_ref, *, n_tiles_max):
    off = jnp.int32(0)
    ti = jnp.int32(0)
    bucket = 0
    for g in range(N_GROUPS):
        for a in range(EXPERTS_PER_GROUP):
            for b in range(a + 1, EXPERTS_PER_GROUP):
                n_q = _padded_tiles(cnt_ref[0, bucket])
                off_ref[bucket] = off

                def fill(k, _, base=ti, ea=g * EXPERTS_PER_GROUP + a, eb=g * EXPERTS_PER_GROUP + b):
                    tea_ref[base + k] = ea
                    teb_ref[base + k] = eb
                    return 0

                lax.fori_loop(0, n_q, fill, 0)
                off = off + n_q * MOE_TG
                ti = ti + n_q
                bucket += 1
    nu_ref[0] = ti

    def rest(k, _):
        tea_ref[k] = N_EXPERTS - 2
        teb_ref[k] = N_EXPERTS - 1
        return 0

    lax.fori_loop(ti, n_tiles_max, rest, 0)


def _plan_call(cnt, n_tiles_max):
    smem = pl.BlockSpec(memory_space=pltpu.SMEM)
    return pl.pallas_call(
        functools.partial(_plan_kernel, n_tiles_max=n_tiles_max),
        in_specs=[smem],
        out_specs=[smem, smem, smem, smem],
        out_shape=[
            jax.ShapeDtypeStruct((N_BUCKETS,), jnp.int32),
            jax.ShapeDtypeStruct((n_tiles_max,), jnp.int32),
            jax.ShapeDtypeStruct((n_tiles_max,), jnp.int32),
            jax.ShapeDtypeStruct((1,), jnp.int32),
        ],
        name="moe_plan",
    )(cnt)


def _row_copy(src_ref, src_row, dst_ref, dst_row, sem, used=SUBLANES, pitch=SUBLANES):
    slot = lambda row: pl.ds(pl.multiple_of(row * pitch, pitch), used)
    return pltpu.make_async_copy(src_ref.at[slot(src_row)], dst_ref.at[slot(dst_row)], sem)


def _dispatch_kernel(off_ref, nu_ref, h_ref, meta_ref, cnt_ref, xs_ref, zero_ref, sem, tile_sem, *, tile, used):
    tile_rows = MOE_TG * SUBLANES
    n_tiles_max = xs_ref.shape[0] // tile_rows

    @pl.when(pl.program_id(0) == 0)
    def _():
        zero_ref[...] = jnp.zeros_like(zero_ref)

        def per_bucket(q, n):
            c = cnt_ref[0, q]
            padded = _padded_tiles(c) * MOE_TG
            base = off_ref[q]

            def fill(r, _):
                _row_copy(zero_ref, 0, xs_ref, base + r, sem, used).start()
                return 0

            lax.fori_loop(c, padded, fill, 0)
            return n + (padded - c)

        n_pad = lax.fori_loop(0, N_BUCKETS, per_bucket, jnp.int32(0))

        def drain(r, _):
            _row_copy(zero_ref, 0, xs_ref, 0, sem, used).wait()
            return 0

        lax.fori_loop(0, n_pad, drain, 0)

        def unused_tile(ti, _):
            cp = pltpu.make_async_copy(
                zero_ref, xs_ref.at[pl.ds(pl.multiple_of(ti * tile_rows, tile_rows), tile_rows)], tile_sem)
            cp.start()
            cp.wait()
            return 0

        lax.fori_loop(nu_ref[0], n_tiles_max, unused_tile, 0)

    for m in range(tile // LANES):
        for r in range(LANES):
            dst = off_ref[meta_ref[m, 0, r]] + meta_ref[m, 1, r]
            _row_copy(h_ref, m * LANES + r, xs_ref, dst, sem, used).start(priority=r % 2)

    pltpu.make_async_copy(h_ref.at[pl.ds(0, tile * used)], xs_ref.at[pl.ds(0, tile * used)], sem).wait()


def _dispatch_call(off, nu, h2, meta, cnt, rows_max, used):
    t = h2.shape[0] // SUBLANES
    tile = MOE_TD if t % MOE_TD == 0 else TM
    return pl.pallas_call(
        functools.partial(_dispatch_kernel, tile=tile, used=used),
        grid_spec=pltpu.PrefetchScalarGridSpec(
            num_scalar_prefetch=2,
            grid=(t // tile,),
            in_specs=[
                pl.BlockSpec((tile * SUBLANES, LANES), lambda i, off, nu: (i, 0)),
                pl.BlockSpec((tile // LANES, SUBLANES, LANES), lambda i, off, nu: (i, 0, 0),
                             memory_space=pltpu.SMEM),
                pl.BlockSpec(memory_space=pltpu.SMEM),
            ],
            out_specs=pl.BlockSpec(memory_space=pl.ANY),
            scratch_shapes=[pltpu.VMEM((MOE_TG * SUBLANES, LANES), h2.dtype), pltpu.SemaphoreType.DMA,
                            pltpu.SemaphoreType.DMA],
        ),
        out_shape=jax.ShapeDtypeStruct((rows_max * SUBLANES, LANES), h2.dtype),
        compiler_params=_params(("arbitrary",)),
        name="moe_dispatch",
    )(off, nu, h2, meta, cnt)


def _expert_mlp(x, e, wg_ref, wu_ref, wd_ref):
    gate = _dot(x, wg_ref[e])
    act = gate * _sigmoid(gate) * _dot(x, wu_ref[e])
    return _dot(act.astype(BF16), wd_ref[e])


def _expert_kernel(tea_ref, teb_ref, nu_ref, x_ref, wg_ref, wu_ref, wd_ref, y_ref):
    groups = wg_ref.shape[1] // (2 * LANES)
    i = pl.program_id(0)

    @pl.when(i < nu_ref[0])
    def _():
        row = _load_slabs(x_ref, MOE_TG, groups + 1)
        x = _unpack_pairs(row[:, 0:groups * LANES])
        wts = pltpu.bitcast(row[:, groups * LANES:], F32)
        ea = tea_ref[i] & (EXPERTS_PER_GROUP - 1)
        eb = teb_ref[i] & (EXPERTS_PER_GROUP - 1)
        y = (wts[:, 0:1] * _expert_mlp(x, ea, wg_ref, wu_ref, wd_ref)
             + wts[:, 1:2] * _expert_mlp(x, eb, wg_ref, wu_ref, wd_ref))
        _store_slabs(y_ref, _pack_pairs(y.astype(BF16)), groups)

    @pl.when(pl.program_id(0) >= nu_ref[0])
    def _():
        y_ref[...] = jnp.zeros_like(y_ref)


def _expert_call(tea, teb, nu, xs, wg, wu, wd):
    d = wg.shape[1]
    ypitch = d // (2 * LANES)
    n_tiles = xs.shape[0] // (MOE_TG * SUBLANES)
    used = lambda i, nu: jnp.minimum(i, nu[0] - 1)
    shift = EXPERTS_PER_GROUP.bit_length() - 1
    wspec = lambda rows, cols: pl.BlockSpec(
        (EXPERTS_PER_GROUP, rows, cols), lambda i, tea, teb, nu: (tea[used(i, nu)] >> shift, 0, 0))
    return pl.pallas_call(
        _expert_kernel,
        grid_spec=pltpu.PrefetchScalarGridSpec(
            num_scalar_prefetch=3,
            grid=(n_tiles,),
            in_specs=[
                pl.BlockSpec((MOE_TG * SUBLANES, LANES), lambda i, tea, teb, nu: (used(i, nu), 0)),
                wspec(d, D_EXPERT), wspec(d, D_EXPERT), wspec(D_EXPERT, d),
            ],
            out_specs=pl.BlockSpec((MOE_TG * ypitch, LANES), lambda i, tea, teb, nu: (i, 0)),
        ),
        out_shape=jax.ShapeDtypeStruct((n_tiles * MOE_TG * ypitch, LANES), jnp.uint32),
        compiler_params=_params(("arbitrary",)),
        name="moe_experts",
    )(tea, teb, nu, xs, wg, wu, wd)


def _combine_kernel(off_ref, meta_ref, ys_ref, x_ref, gt2_ref, gf_ref, o_ref, y_ref, sem, *, final, n_ctx_tiles,
                    tiles_per_batch):
    def body():
        groups = x_ref.shape[-1] // (2 * LANES)
        for m in range(TM // LANES):
            for r in range(LANES):
                src = off_ref[meta_ref[m, 0, r]] + meta_ref[m, 1, r]
                _row_copy(ys_ref, src, y_ref, m * LANES + r, sem, groups, groups).start(priority=r % 2)

        pltpu.make_async_copy(ys_ref.at[pl.ds(0, TM * groups)], y_ref.at[pl.ds(0, TM * groups)], sem).wait()
        x2 = x_ref[...] + gt2_ref[0] * _unpack_pairs(_load_slabs(y_ref, TM, groups, groups)).astype(F32)
        if final:
            x2 = _rms(x2, x2.shape[-1]) * gf_ref[...]
        o_ref[...] = x2

    if final:
        pl.when(pl.program_id(0) % tiles_per_batch >= n_ctx_tiles)(body)
    else:
        body()


def _combine_call(off, meta, ys, x1, mod, gf, tiles_per_batch, n_ctx_tiles, bsz, final):
    t, d = x1.shape
    lat = tiles_per_batch - n_ctx_tiles
    mod_row = lambda i, off: (jnp.where(i % tiles_per_batch < n_ctx_tiles, bsz, i // tiles_per_batch), 0, 5)
    if final:
        out_rows = bsz * lat * TM
        out_ix = lambda i, off: ((i // tiles_per_batch) * lat + jnp.maximum(i % tiles_per_batch - n_ctx_tiles, 0), 0)
    else:
        out_rows = t
        out_ix = lambda i, off: (i, 0)
    return pl.pallas_call(
        functools.partial(_combine_kernel, final=final, n_ctx_tiles=n_ctx_tiles, tiles_per_batch=tiles_per_batch),
        grid_spec=pltpu.PrefetchScalarGridSpec(
            num_scalar_prefetch=1,
            grid=(t // TM,),
            in_specs=[
                pl.BlockSpec((TM // LANES, SUBLANES, LANES), lambda i, off: (i, 0, 0), memory_space=pltpu.SMEM),
                pl.BlockSpec(memory_space=pl.ANY),
                pl.BlockSpec((TM, d), lambda i, off: (i, 0)),
                pl.BlockSpec((1, 1, d), mod_row),
                pl.BlockSpec((1, d), lambda i, off: (0, 0)),
            ],
            out_specs=pl.BlockSpec((TM, d), out_ix),
            scratch_shapes=[pltpu.VMEM((TM * d // (2 * LANES), LANES), jnp.uint32), pltpu.SemaphoreType.DMA],
        ),
        out_shape=jax.ShapeDtypeStruct((out_rows, d), F32),
        compiler_params=_params(("arbitrary",)),
        name="moe_combine",
    )(off, meta, ys, x1, mod, gf)


def _moe_call(h2, meta, cnt, x1, mod, gf, wg, wu, wd, n_ctx_tiles, final):
    bsz, j, d = x1.shape
    t = bsz * j
    rows_max = -(-t // MOE_TG) * MOE_TG + N_BUCKETS * MOE_TG
    off, tea, teb, nu = _plan_call(cnt, rows_max // MOE_TG)
    xs = _dispatch_call(off, nu, h2, meta, cnt, rows_max, d // (2 * LANES) + 1)
    ys = _expert_call(tea, teb, nu, xs, wg, wu, wd)
    out = _combine_call(off, meta, ys, x1.reshape(t, d), mod, gf, j // TM, n_ctx_tiles, bsz, final)
    return out.reshape(bsz, -1, d)


def _rope_tables(ctx_len, seq, groups, shift):
    fidx = [0] * LANES
    cols = [False] * LANES
    role = [0] * LANES
    for start, use_cols in groups:
        for k in range(shift):
            for half in range(2):
                lane = start + half * shift + k
                fidx[lane], cols[lane], role[lane] = k, use_cols, half + 1
    fidx = jnp.asarray(fidx, F32)[None, :]
    cols = jnp.asarray(cols)[None, :]
    role = jnp.asarray(role, jnp.int32)[None, :]
    t = jnp.arange(ctx_len + seq, dtype=jnp.int32)[:, None] - ctx_len
    pos = jnp.where(cols, t % GRID_W, t // GRID_W).astype(F32)
    ang = pos * (ROPE_BASE ** (-fidx / shift))
    rot = (t >= 0) & (role > 0)
    c, s = jnp.cos(ang), jnp.sin(ang)
    return (jnp.where(rot, c, 1.0), jnp.where(rot & (role == 1), -s, 0.0), jnp.where(rot & (role == 2), s, 0.0))


def _block_diag(w):
    n, c, d = w.shape
    eye = jnp.eye(n, dtype=w.dtype)
    return (w[:, :, None, :] * eye[:, None, :, None]).reshape(n * c, n * d)


def _prep_layer(l, w_in, g_cq, w_uq, g_ckv, w_ukv, conv_w, conv_b, lru_wa, lru_ba, lru_wx, lru_bx,
                g_grp, w_out, w_g1, b_g1, w_g2, b_g2):
    d = w_in.shape[1]
    wi = w_in[l]
    zeros = lambda n: jnp.zeros((d, n), wi.dtype)
    win = jnp.concatenate([
        wi[:, 0:384], zeros(64), wi[:, 384:416], zeros(32), wi[:, 416:1952]], axis=1).astype(BF16)
    hq = MLA_NOPE + MLA_ROPE
    wuq = jnp.concatenate(
        [jnp.pad(w_uq[l][:, h * hq:(h + 1) * hq], ((0, 0), (0, LANES - hq))) for h in range(MLA_HEADS)],
        axis=1).astype(BF16)
    wkv = w_ukv[l].reshape(MLA_KV_RANK, MLA_HEADS, MLA_NOPE + MLA_V)
    wuk = jnp.pad(wkv[:, :, :MLA_NOPE], ((0, 0), (0, 0), (0, LANES - MLA_NOPE))).reshape(MLA_KV_RANK, -1)
    wuv = wkv[:, :, MLA_NOPE:].reshape(MLA_KV_RANK, -1)
    wukv = jnp.concatenate([wuk, wuv], axis=1).astype(BF16)
    lru = []
    for dr in range(2):
        lru.append(jnp.concatenate([_block_diag(lru_wa[l, dr]), _block_diag(lru_wx[l, dr])], axis=1).astype(BF16))
        lru.append(jnp.concatenate([lru_ba[l, dr], lru_bx[l, dr]])[None, :])
    wr = jnp.concatenate([jnp.moveaxis(w_g2[l], 0, 1).reshape(d, N_EXPERTS), w_g1[l]], axis=1)
    wr = jnp.pad(wr, ((0, 0), (0, LANES - wr.shape[1]))).astype(BF16)
    br = jnp.pad(jnp.concatenate([b_g2[l].reshape(-1), b_g1[l]]), (0, LANES - N_GROUPS - N_EXPERTS))[None, :]
    return dict(win=win, gcq=g_cq[l][None, :], wuq=wuq, gckv=g_ckv[l][None, :], wukv=wukv,
                cw=conv_w[l], cb=conv_b[l][None, :], wf=lru[0], bf=lru[1], wb=lru[2], bb=lru[3],
                gg=g_grp[l][None, :], wout=w_out[l].astype(BF16), wr=wr, br=br)


def kernel(x, c, ctx, c_ctx, w_ada, b_ada, g_norm1, g_norm2, w_in, g_cq, w_uq, g_ckv, w_ukv, swa_sink,
           conv_w, conv_b, lru_wa, lru_ba, lru_wx, lru_bx, lru_lam, g_grp, w_out, w_g1, b_g1, w_g2, b_g2,
           w_e_gate, w_e_up, w_e_down, g_final):
    bsz, seq, d = x.shape
    ctx_len = ctx.shape[1]
    depth = w_ada.shape[0]
    assert seq % TM == 0 and ctx_len % TM == 0 and seq % GRID_W == 0
    n_ctx_tiles = ctx_len // TM

    rows = -(-(bsz + 1) // SUBLANES) * SUBLANES
    cc = jnp.pad(jnp.concatenate([c, c_ctx[None, :]], axis=0), ((0, rows - bsz - 1), (0, 0)))
    mods = _ada_call(cc, w_ada, b_ada)

    q4 = MLA_ROPE // 4
    s4 = SWA_HEAD_DIM // 4
    mla_tabs = _rope_tables(ctx_len, seq, [(MLA_NOPE, False), (MLA_NOPE + 2 * q4, True)], q4)
    swa_tabs = _rope_tables(
        ctx_len, seq,
        [(hh * SWA_HEAD_DIM + ax * 2 * s4, bool(ax)) for hh in range(LANES // SWA_HEAD_DIM) for ax in range(2)], s4)
    tabs = mla_tabs + swa_tabs

    src = (ctx, x)
    for l in range(depth):
        last = l == depth - 1
        p = _prep_layer(l, w_in, g_cq, w_uq, g_ckv, w_ukv, conv_w, conv_b, lru_wa, lru_ba, lru_wx, lru_bx,
                        g_grp, w_out, w_g1, b_g1, w_g2, b_g2)
        mod = mods[l].reshape(rows, 1, 6 * d)
        qm, km, vm, sq, sk, sv, lx, lg = _inproj_call(
            src, mod, g_norm1[l][None, :], p["win"], p["gcq"], p["wuq"], p["gckv"], p["wukv"], tabs, n_ctx_tiles)
        om = _mla_call(qm, km, vm, n_ctx_tiles, ctx_len)
        osw = _swa_call(swa_sink[l], sq, sk, sv, ctx_len)
        ol = _lru_call(lx, lg, p["cw"], p["cb"], p["wf"], p["bf"], p["wb"], p["bb"], lru_lam[l], n_ctx_tiles)
        x1, h2, meta, cnt = _outproj_call(om, osw, ol, src, mod, p["gg"], p["wout"], g_norm2[l][None, :],
                                          p["wr"], p["br"], n_ctx_tiles, last)
        src = _moe_call(h2, meta, cnt, x1, mod, g_final[None, :], w_e_gate[l].astype(BF16),
                        w_e_up[l].astype(BF16), w_e_down[l].astype(BF16), 0 if last else n_ctx_tiles, last)
    return src
```
